```python
import functools
import numpy as np
import jax
import jax.numpy as jnp
from jax import lax

D_MODEL = 2048
BATCH = 4
SEQ = 2048
DEPTH = 4
DEC_BATCH = 8
DEC_SEQ = 4
PAST_LEN = 16384
PAGE_SIZE = 128

MIX_WIDTH = D_MODEL
D_FF = 5632
PLE_DIM = 256
HEAD_DIM = 64
NSA_WIDTH = MIX_WIDTH // 2
NSA_HEADS = NSA_WIDTH // HEAD_DIM
NSA_KV_HEADS = 4
NSA_GROUP = NSA_HEADS // NSA_KV_HEADS
CMP_BLOCK = 32
CMP_STRIDE = 16
CMP_HIDDEN = 128
SEL_BLOCK = 64
N_SEL = 16
WINDOW = 512
Q_BLOCK = 128
OVERLAP_W = (0.5, 1.0, 1.0, 1.0, 0.5)
GLA_WIDTH = MIX_WIDTH // 4
GLA_HEADS = 4
GLA_DV = GLA_WIDTH // GLA_HEADS
GLA_DK = GLA_DV // 2
GLA_GATE_RANK = 16
GLA_GATE_TAU = 16.0
GLA_CHUNK = 64
RET_WIDTH = MIX_WIDTH // 4
RET_HEADS = 4
RET_DV = RET_WIDTH // RET_HEADS
RET_DK = RET_DV
RET_CHUNK = 64
NSA_KV_WIDTH = NSA_KV_HEADS * 2 * HEAD_DIM
IN_SIZES = (NSA_WIDTH, NSA_KV_WIDTH, NSA_KV_WIDTH, NSA_KV_WIDTH, 3 * NSA_HEADS,
            GLA_HEADS * GLA_DK, GLA_HEADS * GLA_DK, GLA_WIDTH, GLA_GATE_RANK, GLA_WIDTH,
            RET_HEADS * RET_DK, RET_HEADS * RET_DK, RET_WIDTH, RET_WIDTH)
N_IN = sum(IN_SIZES)
NEG_INF = -1e30
FORCE = 1e9
EPS = 1e-6

kernel_name = "hymba_nsa_gla_retnet_macaron_step"


def _rms(x, g):
    xf = x.astype(jnp.float32)
    y = xf * lax.rsqrt(jnp.mean(xf * xf, axis=-1, keepdims=True) + EPS)
    return (y * g.astype(jnp.float32)).astype(x.dtype)


def _rms_plain(x):
    xf = x.astype(jnp.float32)
    return (xf * lax.rsqrt(jnp.mean(xf * xf, axis=-1, keepdims=True) + EPS)).astype(x.dtype)


def _group_norm(x, g):
    xf = x.astype(jnp.float32)
    mu = jnp.mean(xf, axis=-1, keepdims=True)
    var = jnp.mean(jnp.square(xf - mu), axis=-1, keepdims=True)
    return (xf - mu) * lax.rsqrt(var + EPS) * g.astype(jnp.float32)


def _swiglu(x, wg, wu, wd):
    return (jax.nn.silu(x @ wg) * (x @ wu)) @ wd


def _alibi_slopes():
    h = jnp.arange(1, NSA_HEADS + 1, dtype=jnp.float32)
    return jnp.exp2(-8.0 * h / NSA_HEADS).reshape(NSA_KV_HEADS, NSA_GROUP)


def _chunk_len(T, C):
    return C if T % C == 0 else T


def _masked_softmax(s, mask):
    s = jnp.where(mask, s, NEG_INF)
    m = jnp.max(s, axis=-1, keepdims=True)
    e = jnp.exp(s - m) * mask
    return e / jnp.maximum(jnp.sum(e, axis=-1, keepdims=True), 1e-30)


def _split_in(u):
    B, T = u.shape[:2]
    cuts = np.cumsum(IN_SIZES)[:-1].tolist()
    (nq, kvc, kvs, kvw, ng, gq, gk, gv, glr, gog, rq, rk, rv, rg) = jnp.split(u, cuts, axis=-1)
    kv = lambda a: a.reshape(B, T, NSA_KV_HEADS, 2, HEAD_DIM)
    return dict(
        nsa_q=nq.reshape(B, T, NSA_KV_HEADS, NSA_GROUP, HEAD_DIM),
        kv_c=kv(kvc), kv_s=kv(kvs), kv_w=kv(kvw),
        nsa_gate=jax.nn.sigmoid(ng.reshape(B, T, 3, NSA_KV_HEADS, NSA_GROUP)),
        gla_q=gq.reshape(B, T, GLA_HEADS, GLA_DK), gla_k=gk.reshape(B, T, GLA_HEADS, GLA_DK),
        gla_v=gv.reshape(B, T, GLA_HEADS, GLA_DV), gla_lr=glr,
        gla_og=gog.reshape(B, T, GLA_HEADS, GLA_DV),
        ret_q=rq.reshape(B, T, RET_HEADS, RET_DK), ret_k=rk.reshape(B, T, RET_HEADS, RET_DK),
        ret_v=rv.reshape(B, T, RET_HEADS, RET_DV), ret_g=rg.reshape(B, T, RET_HEADS, RET_DV))


def _attend(q, k, v, q_pos, k_pos, slopes, window=None):
    s = jnp.einsum('...thgd,...shd->...hgts', q, k).astype(jnp.float32) * HEAD_DIM ** -0.5
    dist = q_pos[..., :, None] - k_pos[..., None, :]
    mask = (dist >= 0) & (k_pos[..., None, :] >= 0)
    if window is not None:
        mask = mask & (dist < window)
    s = s - slopes[:, :, None, None] * dist[..., None, None, :, :].astype(jnp.float32)
    p = _masked_softmax(s, mask[..., None, None, :, :])
    o = jnp.einsum('...hgts,...shd->...thgd', p, v.astype(jnp.float32))
    return o.astype(q.dtype), p


def _compress(kv, w1, w2, pos):
    B, L = kv.shape[:2]
    n_cmp = (L - CMP_BLOCK) // CMP_STRIDE + 1
    half = CMP_STRIDE * HEAD_DIM
    chunks = kv[:, :(n_cmp + 1) * CMP_STRIDE].reshape(B, n_cmp + 1, CMP_STRIDE, NSA_KV_HEADS, 2, HEAD_DIM)
    flat = chunks.transpose(0, 1, 3, 4, 2, 5).reshape(B, n_cmp + 1, NSA_KV_HEADS, 2, half)
    first = jnp.einsum('bnhcf,cfe->bnhce', flat, w1[:, :half])
    second = jnp.einsum('bnhcf,cfe->bnhce', flat, w1[:, half:])
    bias = jnp.einsum('cf,cfe->ce', pos.reshape(2, -1), w1)
    hidden = jax.nn.gelu(first[:, :-1] + second[:, 1:] + bias)
    out = jnp.einsum('bnhce,ced->bnhcd', hidden, w2)
    ends = jnp.arange(n_cmp) * CMP_STRIDE + (CMP_BLOCK - 1)
    return out, ends


def _select(p_cmp, t_pos, L):
    imp = jnp.sum(p_cmp, axis=2)
    ratio = SEL_BLOCK // CMP_STRIDE
    n_slc = -(-L // SEL_BLOCK)
    n_cmp = imp.shape[-1]
    total = ratio * (n_slc + 1)
    imp = jnp.pad(imp, ((0, 0), (0, 0), (0, 0), (1, total - 1 - n_cmp)))
    score = sum(w * imp[..., k::ratio][..., :n_slc] for k, w in enumerate(OVERLAP_W))
    blk = jnp.arange(n_slc)[None, :]
    cur = (t_pos // SEL_BLOCK)[:, None]
    valid = blk <= cur
    forced = valid & ((blk == 0) | (blk == cur) | (blk == cur - 1))
    score = jnp.where(forced, FORCE, jnp.where(valid, score, -FORCE))
    return lax.top_k(score, min(N_SEL, n_slc))[1]


def _sel_attend(q, k, v, pos, t_pos, slopes):
    k = k.reshape(k.shape[:-3] + (-1, HEAD_DIM))
    v = v.reshape(v.shape[:-3] + (-1, HEAD_DIM))
    pos = pos.reshape(pos.shape[:-2] + (-1,))
    s = jnp.einsum('...thgd,...htsd->...hgts', q, k).astype(jnp.float32) * HEAD_DIM ** -0.5
    dist = t_pos[:, None] - pos
    s = s - slopes[:, :, None, None] * dist[..., None, :, :].astype(jnp.float32)
    p = _masked_softmax(s, (dist >= 0)[..., None, :, :])
    return jnp.einsum('...hgts,...htsd->...thgd', p, v.astype(jnp.float32)).astype(q.dtype)


def _slc_prompt(q, kv_s, sel, slopes):
    B, T = q.shape[:2]
    nqb = T // Q_BLOCK
    h_idx = jnp.arange(NSA_KV_HEADS)[:, None, None, None]
    offs = jnp.arange(SEL_BLOCK)

    def block(args):
        b, i = args
        t0 = i * Q_BLOCK
        q_blk = lax.dynamic_slice_in_dim(q[b], t0, Q_BLOCK, axis=0)
        sel_blk = lax.dynamic_slice_in_dim(sel[b], t0, Q_BLOCK, axis=1)
        pos = sel_blk[..., None] * SEL_BLOCK + offs
        rows = kv_s[b][pos, h_idx]
        return _sel_attend(q_blk, rows[..., 0, :], rows[..., 1, :], pos, t0 + jnp.arange(Q_BLOCK), slopes)

    b_ids = jnp.repeat(jnp.arange(B), nqb)
    i_ids = jnp.tile(jnp.arange(nqb), B)
    out = lax.map(block, (b_ids, i_ids))
    return out.reshape(B, T, NSA_KV_HEADS, NSA_GROUP, HEAD_DIM)


def _slc_sample(q, kv_s, cache_slc_kv, page_table, layer, sel, t_pos, slopes):
    B, T = q.shape[:2]
    past_len = page_table.shape[1] * PAGE_SIZE
    pos = sel[..., None] * SEL_BLOCK + jnp.arange(SEL_BLOCK)
    b_idx = jnp.arange(B)[:, None, None, None, None]
    h_idx = jnp.arange(NSA_KV_HEADS)[None, :, None, None, None]
    pc = jnp.minimum(pos, past_len - 1)
    page = page_table[b_idx, pc // PAGE_SIZE]
    rows_past = cache_slc_kv[page, layer, pc % PAGE_SIZE, h_idx]
    rows_new = kv_s[b_idx, jnp.clip(pos - past_len, 0, T - 1), h_idx]
    rows = jnp.where((pos < past_len)[..., None, None], rows_past, rows_new)
    return _sel_attend(q, rows[..., 0, :], rows[..., 1, :], pos, t_pos, slopes)


def _win_prompt(q, kv_w, slopes):
    B, T = q.shape[:2]
    nqb = T // Q_BLOCK
    nb = WINDOW // Q_BLOCK
    kv_pad = jnp.pad(kv_w, ((0, 0), (WINDOW, 0), (0, 0), (0, 0), (0, 0)))
    blocks = kv_pad.reshape(B, nb + nqb, Q_BLOCK, NSA_KV_HEADS, 2, HEAD_DIM)
    band = jnp.concatenate([blocks[:, r:r + nqb] for r in range(nb + 1)], axis=2)
    qb = q.reshape(B, nqb, Q_BLOCK, NSA_KV_HEADS, NSA_GROUP, HEAD_DIM)
    q_pos = jnp.arange(T).reshape(nqb, Q_BLOCK)
    k_pos = jnp.arange(nqb)[:, None] * Q_BLOCK - WINDOW + jnp.arange((nb + 1) * Q_BLOCK)[None, :]
    o, _ = _attend(qb, band[..., 0, :], band[..., 1, :], q_pos, k_pos, slopes, WINDOW)
    return o.reshape(B, T, NSA_KV_HEADS, NSA_GROUP, HEAD_DIM)


def _nsa_combine(g, o_c, o_s, o_w):
    return g[:, :, 0, :, :, None] * o_c + g[:, :, 1, :, :, None] * o_s + g[:, :, 2, :, :, None] * o_w


def _gla(q, k, v, log_a, s0):
    B, T, H, _ = q.shape
    DV = v.shape[-1]
    C = _chunk_len(T, GLA_CHUNK)
    n = T // C

    def split(a):
        return a.astype(jnp.float32).reshape(B, n, C, H, a.shape[-1]).transpose(1, 0, 3, 2, 4)

    qc, kc, vc = split(q), split(k), split(v)
    b = jnp.cumsum(split(log_a), axis=3)
    causal = jnp.tril(jnp.ones((C, C), dtype=bool))
    decay = jnp.exp(jnp.where(causal[:, :, None], b[..., :, None, :] - b[..., None, :, :], NEG_INF))
    attn = jnp.einsum('nbhtk,nbhsk,nbhtsk->nbhts', qc, kc, decay)
    o_intra = jnp.einsum('nbhts,nbhsv->nbhtv', attn, vc)

    def step(S, inp):
        q_, k_, v_, b_ = inp
        o_inter = jnp.einsum('bhtk,bhkv->bhtv', q_ * jnp.exp(b_), S)
        b_last = b_[:, :, -1:, :]
        S = jnp.exp(b_last[:, :, 0, :, None]) * S + jnp.einsum('bhtk,bhtv->bhkv', k_ * jnp.exp(b_last - b_), v_)
        return S, o_inter

    S, o_inter = lax.scan(step, s0.astype(jnp.float32), (qc, kc, vc, b))
    o = (o_intra + o_inter).transpose(1, 0, 3, 2, 4).reshape(B, T, H, DV)
    return o, S


def _retention(q, k, v, log_gamma, r0):
    B, T, H, _ = q.shape
    DV = v.shape[-1]
    C = _chunk_len(T, RET_CHUNK)
    n = T // C

    def split(a):
        return a.astype(jnp.float32).reshape(B, n, C, H, a.shape[-1]).transpose(1, 0, 3, 2, 4)

    qc, kc, vc = split(q), split(k), split(v)
    idx = jnp.arange(C, dtype=jnp.float32)
    rel = idx[:, None] - idx[None, :]
    D = jnp.where(rel >= 0, jnp.exp(jnp.maximum(rel, 0.0) * log_gamma[:, None, None]), 0.0)
    attn = jnp.einsum('nbhtk,nbhsk->nbhts', qc, kc) * D
    o_intra = jnp.einsum('nbhts,nbhsv->nbhtv', attn, vc)
    q_dec = jnp.exp((idx + 1.0)[None, :, None] * log_gamma[:, None, None])
    k_dec = jnp.exp((C - 1.0 - idx)[None, :, None] * log_gamma[:, None, None])
    c_dec = jnp.exp(C * log_gamma)[:, None, None]

    def step(R, inp):
        q_, k_, v_ = inp
        o_inter = jnp.einsum('bhtk,bhkv->bhtv', q_ * q_dec, R)
        R = c_dec * R + jnp.einsum('bhtk,bhtv->bhkv', k_ * k_dec, v_)
        return R, o_inter

    R, o_inter = lax.scan(step, r0.astype(jnp.float32), (qc, kc, vc))
    o = (o_intra + o_inter).transpose(1, 0, 3, 2, 4).reshape(B, T, H, DV)
    return o, R


def _merge_groups(pc, o_nsa, s_gla0, s_ret0, w_gla_decay, b_gla_decay, nsa_norm, gla_norm, ret_norm):
    B, T = o_nsa.shape[:2]
    dt = o_nsa.dtype
    log_a = jax.nn.log_sigmoid((pc['gla_lr'] @ w_gla_decay + b_gla_decay).astype(jnp.float32)) / GLA_GATE_TAU
    log_a = log_a.reshape(B, T, GLA_HEADS, GLA_DK)
    o_g, s_g = _gla(pc['gla_q'] * GLA_DK ** -0.5, pc['gla_k'], pc['gla_v'], log_a, s_gla0)
    log_gamma = jnp.log1p(-jnp.exp2(-5.0 - jnp.arange(RET_HEADS, dtype=jnp.float32)))
    o_r, s_r = _retention(pc['ret_q'], pc['ret_k'] * RET_DK ** -0.5, pc['ret_v'], log_gamma, s_ret0)
    nsa_out = _rms(o_nsa.reshape(B, T, NSA_WIDTH), nsa_norm)
    gla_out = _rms(o_g, gla_norm) * jax.nn.silu(pc['gla_og'].astype(jnp.float32))
    ret_out = _group_norm(o_r, ret_norm.reshape(RET_HEADS, RET_DV)) * jax.nn.silu(pc['ret_g'].astype(jnp.float32))
    mixed = jnp.concatenate([nsa_out, gla_out.reshape(B, T, GLA_WIDTH).astype(dt),
                             ret_out.reshape(B, T, RET_WIDTH).astype(dt)], axis=-1)
    return mixed, s_g, s_r


def _mixer_prompt(u, slopes, mw):
    w_cmp1, w_cmp2, cmp_pos, nsa_norm, w_gla_decay, b_gla_decay, gla_norm, ret_norm = mw
    dt = u.dtype
    B, T = u.shape[:2]
    pc = _split_in(u)
    q = pc['nsa_q']
    t_pos = jnp.arange(T)
    kc, ends = _compress(pc['kv_c'], w_cmp1, w_cmp2, cmp_pos)
    o_c, p_c = _attend(q, kc[..., 0, :], kc[..., 1, :], t_pos, ends, slopes)
    sel = _select(p_c, t_pos, T)
    o_s = _slc_prompt(q, pc['kv_s'], sel, slopes)
    o_w = _win_prompt(q, pc['kv_w'], slopes)
    o_nsa = _nsa_combine(pc['nsa_gate'], o_c, o_s, o_w).astype(dt)
    s_g0 = jnp.zeros((B, GLA_HEADS, GLA_DK, GLA_DV), jnp.float32)
    s_r0 = jnp.zeros((B, RET_HEADS, RET_DK, RET_DV), jnp.float32)
    mixed, s_g, s_r = _merge_groups(pc, o_nsa, s_g0, s_r0, w_gla_decay, b_gla_decay, nsa_norm, gla_norm, ret_norm)
    wb = min(WINDOW, T)
    return mixed.astype(dt), (pc['kv_c'], pc['kv_s'], pc['kv_w'][:, T - wb:], s_g.astype(dt), s_r.astype(dt))


def _mixer_sample(u, slopes, cache_cmp_kv, cache_slc_kv, win_buf, s_g0, s_r0, page_table, layer, mw):
    w_cmp1, w_cmp2, cmp_pos, nsa_norm, w_gla_decay, b_gla_decay, gla_norm, ret_norm = mw
    dt = u.dtype
    B, T = u.shape[:2]
    past_len = page_table.shape[1] * PAGE_SIZE
    pc = _split_in(u)
    q = pc['nsa_q']
    t_pos = past_len + jnp.arange(T)
    past_c = cache_cmp_kv[page_table, layer].reshape(B, past_len, NSA_KV_HEADS, 2, HEAD_DIM)
    kv_c_all = jnp.concatenate([past_c, pc['kv_c']], axis=1)
    kc, ends = _compress(kv_c_all, w_cmp1, w_cmp2, cmp_pos)
    o_c, p_c = _attend(q, kc[..., 0, :], kc[..., 1, :], t_pos, ends, slopes)
    sel = _select(p_c, t_pos, past_len + T)
    o_s = _slc_sample(q, pc['kv_s'], cache_slc_kv, page_table, layer, sel, t_pos, slopes)
    wb = win_buf.shape[1]
    kv_w_all = jnp.concatenate([win_buf, pc['kv_w']], axis=1)
    k_pos = past_len - wb + jnp.arange(wb + T)
    o_w, _ = _attend(q, kv_w_all[..., 0, :], kv_w_all[..., 1, :], t_pos, k_pos, slopes, WINDOW)
    o_nsa = _nsa_combine(pc['nsa_gate'], o_c, o_s, o_w).astype(dt)
    mixed, s_g, s_r = _merge_groups(pc, o_nsa, s_g0, s_r0, w_gla_decay, b_gla_decay, nsa_norm, gla_norm, ret_norm)
    return mixed.astype(dt), (pc['kv_c'], pc['kv_s'], kv_w_all[:, T:], s_g.astype(dt), s_r.astype(dt))


def _layer(x, p_emb, mixer, g_pre, g_post, wg, wu, wd, w_in, w_out, w_ple, w_ple_gate):
    x = x + 0.5 * _rms(_swiglu(_rms(x, g_pre[0]), wg[0], wu[0], wd[0]), g_post[0])
    mixed, states = mixer(_rms(x, g_pre[1]) @ w_in)
    x = x + _rms(mixed @ w_out, g_post[1])
    x = x + 0.5 * _rms(_swiglu(_rms(x, g_pre[2]), wg[1], wu[1], wd[1]), g_post[2])
    x = x + jax.nn.sigmoid(_rms_plain(x) @ w_ple_gate) * (p_emb @ w_ple)
    return x, states


def setup_inputs(seed: int = 0) -> dict:
    key = jax.random.key(seed)
    ks = iter(jax.random.split(key, 40))
    n_pages = PAST_LEN // PAGE_SIZE
    n_pool = (DEC_BATCH * n_pages * 5) // 4
    win_buf = min(WINDOW, PAST_LEN)

    def nrm(shape, scale=1.0):
        return jax.random.normal(next(ks), shape, jnp.float32) * scale

    def gain(shape):
        return 1.0 + nrm(shape, 0.05)

    page_table = jax.random.permutation(next(ks), n_pool)[:DEC_BATCH * n_pages]
    page_table = page_table.reshape(DEC_BATCH, n_pages).astype(jnp.int32)
    return {
        'x_prompt': nrm((BATCH, SEQ, D_MODEL)),
        'x_sample': nrm((DEC_BATCH, DEC_SEQ, D_MODEL)),
        'cache_cmp_kv': nrm((n_pool, DEPTH, PAGE_SIZE, NSA_KV_HEADS, 2, HEAD_DIM)),
        'cache_slc_kv': nrm((n_pool, DEPTH, PAGE_SIZE, NSA_KV_HEADS, 2, HEAD_DIM)),
        'cache_win_kv': nrm((DEC_BATCH, DEPTH, win_buf, NSA_KV_HEADS, 2, HEAD_DIM)),
        'state_gla': nrm((DEC_BATCH, DEPTH, GLA_HEADS, GLA_DK, GLA_DV), 0.5),
        'state_ret': nrm((DEC_BATCH, DEPTH, RET_HEADS, RET_DK, RET_DV), 0.5),
        'page_table': page_table,
        'p_prompt': nrm((DEPTH, BATCH, SEQ, PLE_DIM)),
        'p_sample': nrm((DEPTH, DEC_BATCH, DEC_SEQ, PLE_DIM)),
        'norm_pre': gain((DEPTH, 3, D_MODEL)),
        'norm_post': gain((DEPTH, 3, D_MODEL)),
        'w_ffn_gate': nrm((DEPTH, 2, D_MODEL, D_FF), D_MODEL ** -0.5),
        'w_ffn_up': nrm((DEPTH, 2, D_MODEL, D_FF), D_MODEL ** -0.5),
        'w_ffn_down': nrm((DEPTH, 2, D_FF, D_MODEL), D_FF ** -0.5),
        'w_in': nrm((DEPTH, D_MODEL, N_IN), D_MODEL ** -0.5),
        'w_out': nrm((DEPTH, MIX_WIDTH, D_MODEL), MIX_WIDTH ** -0.5),
        'w_cmp1': nrm((DEPTH, 2, CMP_BLOCK * HEAD_DIM, CMP_HIDDEN), (CMP_BLOCK * HEAD_DIM) ** -0.5),
        'w_cmp2': nrm((DEPTH, 2, CMP_HIDDEN, HEAD_DIM), CMP_HIDDEN ** -0.5),
        'cmp_pos': nrm((DEPTH, 2, CMP_BLOCK, HEAD_DIM), 0.1),
        'nsa_norm': gain((DEPTH, NSA_WIDTH)),
        'w_gla_decay': nrm((DEPTH, GLA_GATE_RANK, GLA_HEADS * GLA_DK), GLA_GATE_RANK ** -0.5),
        'b_gla_decay': nrm((DEPTH, GLA_HEADS * GLA_DK), 0.1),
        'gla_norm': gain((DEPTH, GLA_DV)),
        'ret_norm': gain((DEPTH, RET_WIDTH)),
        'w_ple': nrm((DEPTH, PLE_DIM, D_MODEL), PLE_DIM ** -0.5),
        'w_ple_gate': nrm((DEPTH, D_MODEL, D_MODEL), D_MODEL ** -0.5),
    }


def reference(x_prompt, x_sample, cache_cmp_kv, cache_slc_kv, cache_win_kv, state_gla, state_ret, page_table,
              p_prompt, p_sample, norm_pre, norm_post, w_ffn_gate, w_ffn_up, w_ffn_down, w_in, w_out,
              w_cmp1, w_cmp2, cmp_pos, nsa_norm, w_gla_decay, b_gla_decay, gla_norm, ret_norm, w_ple, w_ple_gate):
    slopes = _alibi_slopes()
    y_p, y_s = x_prompt, x_sample
    st_p = [[] for _ in range(5)]
    st_s = [[] for _ in range(5)]
    for l in range(DEPTH):
        lw = (norm_pre[l], norm_post[l], w_ffn_gate[l], w_ffn_up[l], w_ffn_down[l], w_in[l], w_out[l],
              w_ple[l], w_ple_gate[l])
        mw = (w_cmp1[l], w_cmp2[l], cmp_pos[l], nsa_norm[l], w_gla_decay[l], b_gla_decay[l], gla_norm[l], ret_norm[l])
        mix_p = functools.partial(_mixer_prompt, slopes=slopes, mw=mw)
        y_p, sp = _layer(y_p, p_prompt[l], mix_p, *lw)
        mix_s = functools.partial(_mixer_sample, slopes=slopes, cache_cmp_kv=cache_cmp_kv, cache_slc_kv=cache_slc_kv,
                                  win_buf=cache_win_kv[:, l], s_g0=state_gla[:, l], s_r0=state_ret[:, l],
                                  page_table=page_table, layer=l, mw=mw)
        y_s, ss = _layer(y_s, p_sample[l], mix_s, *lw)
        for j in range(5):
            st_p[j].append(sp[j])
            st_s[j].append(ss[j])
    new_cmp_kv_prompt = jnp.stack(st_p[0], axis=1)
    new_slc_kv_prompt = jnp.stack(st_p[1], axis=1)
    new_win_kv_prompt = jnp.stack(st_p[2], axis=1)
    new_state_gla_prompt = jnp.stack(st_p[3], axis=1)
    new_state_ret_prompt = jnp.stack(st_p[4], axis=1)
    new_cmp_kv_sample = jnp.stack(st_s[0], axis=1)
    new_slc_kv_sample = jnp.stack(st_s[1], axis=1)
    new_win_kv_sample = jnp.stack(st_s[2], axis=1)
    new_state_gla_sample = jnp.stack(st_s[3], axis=1)
    new_state_ret_sample = jnp.stack(st_s[4], axis=1)
    return (y_p, y_s, new_cmp_kv_prompt, new_slc_kv_prompt, new_win_kv_prompt, new_state_gla_prompt,
            new_state_ret_prompt, new_cmp_kv_sample, new_slc_kv_sample, new_win_kv_sample,
            new_state_gla_sample, new_state_ret_sample)
```

```python
import functools
import math

import numpy as np
import jax
import jax.numpy as jnp
from jax import lax
from jax.experimental import pallas as pl
from jax.experimental.pallas import tpu as pltpu

F32 = jnp.float32
BF16 = jnp.bfloat16
HIGHEST = lax.Precision.HIGHEST

HEAD_DIM = 64
NSA_HEADS = 16
NSA_KV_HEADS = 4
NSA_GROUP = 4
CMP_BLOCK = 32
CMP_STRIDE = 16
CMP_HIDDEN = 128
SEL_BLOCK = 64
N_SEL = 16
WINDOW = 512
Q_BLOCK = 128
OVERLAP_W = (0.5, 1.0, 1.0, 1.0, 0.5)
GLA_HEADS = 4
GLA_DK = 64
GLA_DV = 128
GLA_GATE_RANK = 16
GLA_GATE_TAU = 16.0
GLA_CHUNK = 64
RET_HEADS = 4
RET_DK = 128
RET_DV = 128
RET_CHUNK = 64
PAGE_SIZE = 128
NEG_INF = -1e30
FORCE = 1e9
EPS = 1e-6
IN_SIZES = (1024, 512, 512, 512, 48, 256, 256, 512, 16, 512, 512, 512, 512, 512)

LANES = 128
SUBLANES = 8
VMEM_LIMIT = 48 * 1024 * 1024

U_WIDTH = 8192
COL_Q = 0
BLK_KVC, BLK_KVS, BLK_KVW = 4, 5, 6
BLK_GQ, BLK_GK, BLK_GV, BLK_GOG = 7, 8, 9, 10
BLK_RQ, BLK_RK, BLK_RV, BLK_RG = 11, 12, 13, 14
BLK_MISC = 60
MISC_GLR = 48

ALIBI = [[2.0 ** (-8.0 * (h * NSA_GROUP + g + 1) / NSA_HEADS) for g in range(NSA_GROUP)]
         for h in range(NSA_KV_HEADS)]
RET_LOG_GAMMA = [math.log1p(-(2.0 ** (-5.0 - h))) for h in range(RET_HEADS)]


def _cparams(sem):
    return pltpu.CompilerParams(dimension_semantics=sem, vmem_limit_bytes=VMEM_LIMIT)


def _rms(x, g=None):
    y = x * lax.rsqrt(jnp.mean(x * x, axis=-1, keepdims=True) + EPS)
    return y if g is None else y * g


def _dot(a, b):
    return jnp.dot(a, b, preferred_element_type=F32)


def _dot_nt(a, b):
    return lax.dot_general(a, b, (((1,), (1,)), ((), ())), preferred_element_type=F32)


def _ffn_body(x_ref, gpre_ref, wg_ref, wu_ref, wd_ref, gpost_ref, o_ref, xn_ref, acc_ref):
    f = pl.program_id(1)

    @pl.when(f == 0)
    def _():
        xn_ref[...] = _rms(x_ref[...], gpre_ref[...]).astype(BF16)
        acc_ref[...] = jnp.zeros_like(acc_ref)

    xn = xn_ref[...]
    h = jax.nn.silu(_dot(xn, wg_ref[...])) * _dot(xn, wu_ref[...])
    acc_ref[...] += _dot(h.astype(BF16), wd_ref[...])

    @pl.when(f == pl.num_programs(1) - 1)
    def _():
        o_ref[...] = x_ref[...] + 0.5 * _rms(acc_ref[...], gpost_ref[...])


def _ffn(x, g_pre, g_post, wg, wu, wd, layer, which, norm_idx, tm, tf=512):
    m, d = x.shape
    ff = wg.shape[-1]
    assert m % tm == 0 and ff % tf == 0
    return pl.pallas_call(
        _ffn_body,
        out_shape=jax.ShapeDtypeStruct((m, d), F32),
        grid=(m // tm, ff // tf),
        in_specs=[
            pl.BlockSpec((tm, d), lambda i, f: (i, 0)),
            pl.BlockSpec((None, None, 1, d), lambda i, f: (layer, norm_idx, 0, 0)),
            pl.BlockSpec((None, None, d, tf), lambda i, f: (layer, which, 0, f)),
            pl.BlockSpec((None, None, d, tf), lambda i, f: (layer, which, 0, f)),
            pl.BlockSpec((None, None, tf, d), lambda i, f: (layer, which, f, 0)),
            pl.BlockSpec((None, None, 1, d), lambda i, f: (layer, norm_idx, 0, 0)),
        ],
        out_specs=pl.BlockSpec((tm, d), lambda i, f: (i, 0)),
        scratch_shapes=[pltpu.VMEM((tm, d), BF16), pltpu.VMEM((tm, d), F32)],
        compiler_params=_cparams(("parallel", "arbitrary")),
        name="ffn",
    )(x, g_pre, wg, wu, wd, g_post)


def _proj_in_body(x_ref, g_ref, w_ref, o_ref, xn_ref):
    @pl.when(pl.program_id(1) == 0)
    def _():
        xn_ref[...] = _rms(x_ref[...], g_ref[...]).astype(BF16)

    o_ref[...] = _dot(xn_ref[...], w_ref[...])


def _proj_in(x, g_pre, w_in, layer, tm, tn=1024):
    m, d = x.shape
    n = w_in.shape[-1]
    return pl.pallas_call(
        _proj_in_body,
        out_shape=jax.ShapeDtypeStruct((m, n), F32),
        grid=(m // tm, n // tn),
        in_specs=[
            pl.BlockSpec((tm, d), lambda i, j: (i, 0)),
            pl.BlockSpec((None, None, 1, d), lambda i, j: (layer, 1, 0, 0)),
            pl.BlockSpec((None, d, tn), lambda i, j: (layer, 0, j)),
        ],
        out_specs=pl.BlockSpec((tm, tn), lambda i, j: (i, j)),
        scratch_shapes=[pltpu.VMEM((tm, d), BF16)],
        compiler_params=_cparams(("parallel", "arbitrary")),
        name="proj_in",
    )(x, g_pre, w_in)


def _proj_out_body(x_ref, nsa_ref, gla_ref, ret_ref, gn_ref, w_ref, gpost_ref, o_ref):
    nsa_w = nsa_ref.shape[-1]
    gla_w = gla_ref.shape[-1]
    nsa = _rms(nsa_ref[...], gn_ref[...]).astype(BF16)
    y = _dot(nsa, w_ref[0:nsa_w, :])
    y += _dot(gla_ref[...].astype(BF16), w_ref[nsa_w:nsa_w + gla_w, :])
    y += _dot(ret_ref[...].astype(BF16), w_ref[nsa_w + gla_w:, :])
    o_ref[...] = x_ref[...] + _rms(y, gpost_ref[...])


def _proj_out(x, o_nsa, gla_out, ret_out, nsa_norm, w_out, g_post, layer, tm):
    m, d = x.shape
    row = lambda a: pl.BlockSpec((tm, a.shape[-1]), lambda i: (i, 0))
    return pl.pallas_call(
        _proj_out_body,
        out_shape=jax.ShapeDtypeStruct((m, d), F32),
        grid=(m // tm,),
        in_specs=[
            row(x), row(o_nsa), row(gla_out), row(ret_out),
            pl.BlockSpec((None, 1, o_nsa.shape[-1]), lambda i: (layer, 0, 0)),
            pl.BlockSpec((None, w_out.shape[1], d), lambda i: (layer, 0, 0)),
            pl.BlockSpec((None, None, 1, d), lambda i: (layer, 1, 0, 0)),
        ],
        out_specs=row(x),
        compiler_params=_cparams(("parallel",)),
        name="proj_out",
    )(x, o_nsa, gla_out, ret_out, nsa_norm, w_out, g_post)


def _ple_body(x_ref, p_ref, wg_ref, wp_ref, o_ref):
    x = x_ref[...]
    gate = jax.nn.sigmoid(_dot(_rms(x).astype(BF16), wg_ref[...]))
    o_ref[...] = x + gate * _dot(p_ref[...].astype(BF16), wp_ref[...])


def _ple(x, p, w_gate, w_ple, layer, tm):
    m, d = x.shape
    return pl.pallas_call(
        _ple_body,
        out_shape=jax.ShapeDtypeStruct((m, d), F32),
        grid=(m // tm,),
        in_specs=[
            pl.BlockSpec((tm, d), lambda i: (i, 0)),
            pl.BlockSpec((None, tm, p.shape[-1]), lambda i: (layer, i, 0)),
            pl.BlockSpec((None, d, d), lambda i: (layer, 0, 0)),
            pl.BlockSpec((None, p.shape[-1], d), lambda i: (layer, 0, 0)),
        ],
        out_specs=pl.BlockSpec((tm, d), lambda i: (i, 0)),
        compiler_params=_cparams(("parallel",)),
        name="ple",
    )(x, p, w_gate, w_ple)


PAGES_PER_STEP = 16
CHUNKS_PER_PAGE = PAGE_SIZE // CMP_STRIDE
PAGE_COLS = CMP_STRIDE * NSA_KV_HEADS * 2 * HEAD_DIM


def _cmp_bias_body(pos_ref, w1_ref, o_ref):
    o_ref[...] = jnp.dot(pos_ref[...], w1_ref[...], preferred_element_type=F32, precision=HIGHEST)


def _cmp_bias(cmp_pos, w_cmp1, layer):
    kdim = w_cmp1.shape[2]
    return pl.pallas_call(
        _cmp_bias_body,
        out_shape=jax.ShapeDtypeStruct((2, 1, CMP_HIDDEN), F32),
        grid=(2,),
        in_specs=[pl.BlockSpec((None, None, 1, kdim), lambda c: (layer, c, 0, 0)),
                  pl.BlockSpec((None, None, kdim, CMP_HIDDEN), lambda c: (layer, c, 0, 0))],
        out_specs=pl.BlockSpec((None, 1, CMP_HIDDEN), lambda c: (c, 0, 0)),
        compiler_params=_cparams(("arbitrary",)),
        name="cmp_bias",
    )(cmp_pos, w_cmp1)


def _compress_body(pt_ref, *refs):
    del pt_ref
    page_refs = refs[:PAGES_PER_STEP]
    wblk_ref, bias_ref, w2_ref, o_ref, prev_ref = refs[PAGES_PER_STEP:]
    rows = PAGES_PER_STEP * CHUNKS_PER_PAGE
    half = 2 * CMP_HIDDEN

    @pl.when(pl.program_id(1) == 0)
    def _():
        prev_ref[...] = jnp.zeros_like(prev_ref)

    first_row = lax.broadcasted_iota(jnp.int32, (rows, half), 0) == 0
    for h in range(NSA_KV_HEADS):
        acc = jnp.zeros((rows, 2 * half), F32)
        for t in range(CMP_STRIDE):
            c0 = t * NSA_KV_HEADS * LANES + h * LANES
            lhs = jnp.concatenate([pr[:, c0:c0 + LANES] for pr in page_refs], axis=0).astype(BF16)
            acc += _dot(lhs, wblk_ref[t])
        first = acc[:, :half]
        second = acc[:, half:]
        carry = prev_ref[h][SUBLANES - 1:SUBLANES, :]
        shifted = jnp.where(first_row, carry, pltpu.roll(first, 1, 0))
        prev_ref[h] = first[rows - SUBLANES:, :]
        hidden = jax.nn.gelu(shifted + second + bias_ref[...])
        o_ref[:, h * LANES:(h + 1) * LANES] = _dot(hidden.astype(BF16), w2_ref[...])


def _compress(pages, table, layer, wblk, bias, w2blk):
    b, n_pages = table.shape
    assert n_pages % PAGES_PER_STEP == 0
    rows = PAGES_PER_STEP * CHUNKS_PER_PAGE

    def page_spec(k):
        return pl.BlockSpec((None, None, CHUNKS_PER_PAGE, PAGE_COLS),
                            lambda bi, i, pt: (pt[bi, i * PAGES_PER_STEP + k], layer, 0, 0))

    grid_spec = pltpu.PrefetchScalarGridSpec(
        num_scalar_prefetch=1,
        grid=(b, n_pages // PAGES_PER_STEP),
        in_specs=[page_spec(k) for k in range(PAGES_PER_STEP)] + [
            pl.BlockSpec(wblk.shape, lambda bi, i, pt: (0, 0, 0)),
            pl.BlockSpec(bias.shape, lambda bi, i, pt: (0, 0)),
            pl.BlockSpec(w2blk.shape, lambda bi, i, pt: (0, 0)),
        ],
        out_specs=pl.BlockSpec((None, rows, NSA_KV_HEADS * LANES), lambda bi, i, pt: (bi, i, 0)),
        scratch_shapes=[pltpu.VMEM((NSA_KV_HEADS, SUBLANES, 2 * CMP_HIDDEN), F32)],
    )
    return pl.pallas_call(
        _compress_body,
        out_shape=jax.ShapeDtypeStruct((b, n_pages * CHUNKS_PER_PAGE, NSA_KV_HEADS * LANES), F32),
        grid_spec=grid_spec,
        compiler_params=_cparams(("arbitrary", "arbitrary")),
        name="compress",
    )(table, *([pages] * PAGES_PER_STEP), wblk, bias, w2blk)


def _softmax_groups(s, distf, mask, slopes, rows):
    parts = []
    for g in range(NSA_GROUP):
        sg = s[g * rows:(g + 1) * rows] - slopes[g] * distf
        sg = jnp.where(mask, sg, NEG_INF)
        m = jnp.max(sg, axis=-1, keepdims=True)
        e = jnp.where(mask, jnp.exp(sg - m), 0.0)
        parts.append(e / jnp.maximum(jnp.sum(e, axis=-1, keepdims=True), 1e-30))
    return parts


def _stack_q(q_ref, h, scale):
    base = COL_Q + h * NSA_GROUP * LANES
    q = jnp.concatenate([q_ref[:, base + g * LANES: base + (g + 1) * LANES] for g in range(NSA_GROUP)], axis=0)
    return (q * scale).astype(BF16)


def _block_scores(imp, wov_ref, t_pos, n_slc):
    score = jnp.dot(imp, wov_ref[...], preferred_element_type=F32, precision=HIGHEST)
    blk = lax.broadcasted_iota(jnp.int32, score.shape, 1)
    cur = t_pos // SEL_BLOCK
    valid = blk <= cur
    forced = valid & ((blk == 0) | (blk == cur) | (blk == cur - 1))
    score = jnp.where(forced, FORCE, jnp.where(valid, score, -FORCE))
    return score, blk


def _pack_heads(o_ref, heads):
    lane = lax.broadcasted_iota(jnp.int32, heads[0].shape, 1)
    for p in range(NSA_HEADS // 2):
        even = pltpu.roll(heads[2 * p], HEAD_DIM, 1)
        o_ref[:, p * LANES:(p + 1) * LANES] = jnp.where(lane < HEAD_DIM, even, heads[2 * p + 1])


SLC_KEY_CHUNK = 512


def _nsa_prompt_body(q_ref, misc_ref, kc_ref, kvs_ref, kvw_ref, wov_ref, o_ref, *, seq):
    qb = pl.program_id(1)
    t0 = qb * Q_BLOCK
    n_slc = -(-seq // SEL_BLOCK)
    n_sel = min(N_SEL, n_slc)
    n_cr = kc_ref.shape[0]
    t_pos = t0 + lax.broadcasted_iota(jnp.int32, (Q_BLOCK, 1), 0)
    gates = jax.nn.sigmoid(misc_ref[...])
    heads = []
    for h in range(NSA_KV_HEADS):
        qh = _stack_q(q_ref, h, HEAD_DIM ** -0.5)
        sl = slice(h * LANES, (h + 1) * LANES)

        kc = kc_ref[:, sl].astype(BF16)
        r_idx = lax.broadcasted_iota(jnp.int32, (1, n_cr), 1)
        dist = t_pos - (r_idx * CMP_STRIDE + (CMP_STRIDE - 1))
        mask = (r_idx >= 1) & (dist >= 0)
        p_c = _softmax_groups(_dot_nt(qh, kc), dist.astype(F32), mask, ALIBI[h], Q_BLOCK)
        o_c = _dot(jnp.concatenate(p_c, axis=0).astype(BF16), kc)
        imp = p_c[0] + p_c[1] + p_c[2] + p_c[3]

        score, blk = _block_scores(imp, wov_ref, t_pos, n_slc)
        rank = jnp.zeros(score.shape, F32)
        for i in range(n_slc):
            ci = score[:, i:i + 1]
            beats = (ci > score) | ((ci == score) & (blk > i))
            rank += jnp.where(beats, 1.0, 0.0)
        sel = jnp.where((rank < n_sel) & (blk < n_slc), 1.0, 0.0).astype(BF16)

        def chunk(c, carry):
            k0 = pl.multiple_of(c * SLC_KEY_CHUNK, SLC_KEY_CHUNK)
            kv = kvs_ref[pl.ds(k0, SLC_KEY_CHUNK), sl].astype(BF16)
            s = _dot_nt(qh, kv)
            pos = k0 + lax.broadcasted_iota(jnp.int32, (1, SLC_KEY_CHUNK), 1)
            d = t_pos - pos
            df = d.astype(F32)
            expand = jnp.where(
                (pos // SEL_BLOCK) == lax.broadcasted_iota(jnp.int32, (sel.shape[1], SLC_KEY_CHUNK), 0),
                1.0, 0.0).astype(BF16)
            msk = (_dot(sel, expand) > 0.5) & (d >= 0)
            out = []
            for g in range(NSA_GROUP):
                m_old, l_old, a_old = carry[g]
                sg = jnp.where(msk, s[g * Q_BLOCK:(g + 1) * Q_BLOCK] - ALIBI[h][g] * df, NEG_INF)
                m_new = jnp.maximum(m_old, jnp.max(sg, axis=-1, keepdims=True))
                alpha = jnp.exp(m_old - m_new)
                e = jnp.where(msk, jnp.exp(sg - m_new), 0.0)
                l_new = alpha * l_old + jnp.sum(e, axis=-1, keepdims=True)
                a_new = alpha * a_old + _dot(e.astype(BF16), kv)
                out.append((m_new, l_new, a_new))
            return tuple(out)

        init = tuple((jnp.full((Q_BLOCK, 1), NEG_INF, F32), jnp.zeros((Q_BLOCK, 1), F32),
                      jnp.zeros((Q_BLOCK, LANES), F32)) for _ in range(NSA_GROUP))
        n_chunks = (t0 + Q_BLOCK + SLC_KEY_CHUNK - 1) // SLC_KEY_CHUNK
        fin = lax.fori_loop(0, n_chunks, chunk, init)
        o_s = [a / jnp.maximum(l, 1e-30) for (_, l, a) in fin]

        band = WINDOW + Q_BLOCK
        w0 = pl.multiple_of(jnp.maximum(t0 - WINDOW, 0), Q_BLOCK)
        kvw = kvw_ref[pl.ds(w0, band), sl].astype(BF16)
        posw = w0 + lax.broadcasted_iota(jnp.int32, (1, band), 1)
        dw = t_pos - posw
        p_w = _softmax_groups(_dot_nt(qh, kvw), dw.astype(F32), (dw >= 0) & (dw < WINDOW), ALIBI[h], Q_BLOCK)
        o_w = _dot(jnp.concatenate(p_w, axis=0).astype(BF16), kvw)

        for g in range(NSA_GROUP):
            col = h * NSA_GROUP + g
            rows = slice(g * Q_BLOCK, (g + 1) * Q_BLOCK)
            heads.append(gates[:, col:col + 1] * o_c[rows]
                         + gates[:, NSA_HEADS + col:NSA_HEADS + col + 1] * o_s[g]
                         + gates[:, 2 * NSA_HEADS + col:2 * NSA_HEADS + col + 1] * o_w[rows])
    _pack_heads(o_ref, heads)


def _nsa_prompt(u, kc, wov):
    b, t, _ = u.shape
    assert t % Q_BLOCK == 0 and t >= WINDOW + Q_BLOCK and t % SLC_KEY_CHUNK == 0
    return pl.pallas_call(
        functools.partial(_nsa_prompt_body, seq=t),
        out_shape=jax.ShapeDtypeStruct((b, t, NSA_HEADS * HEAD_DIM), F32),
        grid=(b, t // Q_BLOCK),
        in_specs=[
            pl.BlockSpec((None, Q_BLOCK, NSA_HEADS * LANES), lambda bi, i: (bi, i, 0)),
            pl.BlockSpec((None, Q_BLOCK, LANES), lambda bi, i: (bi, i, BLK_MISC)),
            pl.BlockSpec((None,) + kc.shape[1:], lambda bi, i: (bi, 0, 0)),
            pl.BlockSpec((None, t, 4 * LANES), lambda bi, i: (bi, 0, BLK_KVS)),
            pl.BlockSpec((None, t, 4 * LANES), lambda bi, i: (bi, 0, BLK_KVW)),
            pl.BlockSpec(wov.shape, lambda bi, i: (0, 0)),
        ],
        out_specs=pl.BlockSpec((None, Q_BLOCK, NSA_HEADS * HEAD_DIM), lambda bi, i: (bi, i, 0)),
        compiler_params=_cparams(("parallel", "arbitrary")),
        name="nsa_prompt",
    )(u, u, kc, u, u, wov)


def _cmp_select_body(q_ref, kc_ref, wov_ref, oc_ref, sel_ref, *, past_len, t_real):
    tp = q_ref.shape[0]
    n_cr = kc_ref.shape[0]
    n_slc = -(-(past_len + t_real) // SEL_BLOCK)
    t_pos = past_len + lax.broadcasted_iota(jnp.int32, (tp, 1), 0)
    for h in range(NSA_KV_HEADS):
        qh = _stack_q(q_ref, h, HEAD_DIM ** -0.5)
        kc = kc_ref[:, h * LANES:(h + 1) * LANES].astype(BF16)
        r_idx = lax.broadcasted_iota(jnp.int32, (1, n_cr), 1)
        dist = t_pos - (r_idx * CMP_STRIDE + (CMP_STRIDE - 1))
        mask = (r_idx >= 1) & (dist >= 0)
        p_c = _softmax_groups(_dot_nt(qh, kc), dist.astype(F32), mask, ALIBI[h], tp)
        oc_ref[h] = _dot(jnp.concatenate(p_c, axis=0).astype(BF16), kc)
        imp = p_c[0] + p_c[1] + p_c[2] + p_c[3]
        score, blk = _block_scores(imp, wov_ref, t_pos, n_slc)
        work = jnp.where(blk < n_slc, score, -3e38)
        blkf = blk.astype(F32)
        picked = jnp.zeros((tp, LANES), jnp.int32)
        lane = lax.broadcasted_iota(jnp.int32, (tp, LANES), 1)
        for k in range(N_SEL):
            m = jnp.max(work, axis=-1, keepdims=True)
            idx = jnp.min(jnp.where(work == m, blkf, 3e38), axis=-1, keepdims=True)
            picked = jnp.where(lane == k, idx.astype(jnp.int32), picked)
            work = jnp.where(blkf == idx, -3e38, work)
        sel_ref[h] = picked


def _cmp_select(u_s, kc, wov, past_len, t_real):
    b, tp, _ = u_s.shape
    assert -(-(past_len + t_real) // SEL_BLOCK) >= N_SEL
    return pl.pallas_call(
        functools.partial(_cmp_select_body, past_len=past_len, t_real=t_real),
        out_shape=(jax.ShapeDtypeStruct((b, NSA_KV_HEADS, NSA_GROUP * tp, LANES), F32),
                   jax.ShapeDtypeStruct((b, NSA_KV_HEADS, tp, LANES), jnp.int32)),
        grid=(b,),
        in_specs=[
            pl.BlockSpec((None, tp, NSA_HEADS * LANES), lambda bi: (bi, 0, 0)),
            pl.BlockSpec((None,) + kc.shape[1:], lambda bi: (bi, 0, 0)),
            pl.BlockSpec(wov.shape, lambda bi: (0, 0)),
        ],
        out_specs=(pl.BlockSpec((None, NSA_KV_HEADS, NSA_GROUP * tp, LANES), lambda bi: (bi, 0, 0, 0)),
                   pl.BlockSpec((None, NSA_KV_HEADS, tp, LANES), lambda bi: (bi, 0, 0, 0))),
        compiler_params=_cparams(("parallel",)),
        name="cmp_select",
    )(u_s, kc, wov)


def _slc_sample_body(sel_ref, pt_ref, *refs, past_len, t_real):
    del pt_ref
    blk_refs = refs[:N_SEL]
    q_ref, kvn_ref, o_ref = refs[N_SEL:]
    bi, h, t = pl.program_id(0), pl.program_id(1), pl.program_id(2)
    tp = kvn_ref.shape[0]
    rows = NSA_GROUP * tp
    n_past_blk = past_len // SEL_BLOCK
    q = jnp.concatenate([q_ref[:, g * LANES:(g + 1) * LANES] for g in range(NSA_GROUP)], axis=0)
    qh = (q * HEAD_DIM ** -0.5).astype(BF16)
    row = lax.broadcasted_iota(jnp.int32, (rows, 1), 0)
    t_pos = past_len + row % tp

    n_keys = N_SEL * SEL_BLOCK
    lane = lax.broadcasted_iota(jnp.int32, (1, n_keys), 1)
    slot = lane // SEL_BLOCK
    pos = lane % SEL_BLOCK
    ok = jnp.zeros((1, n_keys), jnp.int32)
    for k in range(N_SEL):
        blk = sel_ref[((bi * NSA_KV_HEADS + h) * t_real + t) * N_SEL + k]
        pos = pos + jnp.where(slot == k, blk * SEL_BLOCK, 0)
        ok = ok + jnp.where((slot == k) & (blk < n_past_blk), 1, 0)
    kv = jnp.concatenate([r[...] for r in blk_refs], axis=0).astype(BF16)
    d = t_pos - pos
    msk = (ok > 0) & (d >= 0)
    kvn = jnp.concatenate([kvn_ref[...], jnp.zeros((LANES - tp, LANES), F32)], axis=0).astype(BF16)
    idx_n = lax.broadcasted_iota(jnp.int32, (1, LANES), 1)
    dn = t_pos - (past_len + idx_n)
    mskn = (idx_n < tp) & (dn >= 0)
    s_all = _dot_nt(qh, kv)
    s_new = _dot_nt(qh, kvn)
    outs = []
    for g in range(NSA_GROUP):
        r0 = slice(g * tp, (g + 1) * tp)
        slope = jnp.where(h == 0, ALIBI[0][g], jnp.where(h == 1, ALIBI[1][g],
                                                         jnp.where(h == 2, ALIBI[2][g], ALIBI[3][g])))
        sg = jnp.where(msk[r0], s_all[r0] - slope * d[r0].astype(F32), NEG_INF)
        sn = jnp.where(mskn[r0], s_new[r0] - slope * dn[r0].astype(F32), NEG_INF)
        m = jnp.maximum(jnp.max(sg, axis=-1, keepdims=True), jnp.max(sn, axis=-1, keepdims=True))
        e = jnp.where(msk[r0], jnp.exp(sg - m), 0.0)
        en = jnp.where(mskn[r0], jnp.exp(sn - m), 0.0)
        l = jnp.sum(e, axis=-1, keepdims=True) + jnp.sum(en, axis=-1, keepdims=True)
        outs.append((_dot(e.astype(BF16), kv) + _dot(en.astype(BF16), kvn)) / jnp.maximum(l, 1e-30))
    res = jnp.concatenate(outs, axis=0)

    @pl.when(t == 0)
    def _():
        o_ref[...] = jnp.zeros_like(o_ref)

    o_ref[...] = jnp.where(row % tp == t, res, o_ref[...])


def _slc_sample(u_s, cache, table, sel, layer, past_len, t_real):
    b, tp, _ = u_s.shape
    n_past_blk = past_len // SEL_BLOCK
    per_page = PAGE_SIZE // SEL_BLOCK
    assert past_len % SEL_BLOCK == 0 and t_real <= SEL_BLOCK and PAGE_SIZE % SEL_BLOCK == 0

    def blk_spec(k):
        def imap(bi, h, t, sel_ref, pt):
            blk = jnp.minimum(sel_ref[((bi * NSA_KV_HEADS + h) * t_real + t) * N_SEL + k], n_past_blk - 1)
            return (pt[bi, blk // per_page], layer, blk % per_page, h)
        return pl.BlockSpec((None, None, SEL_BLOCK, LANES), imap)

    grid_spec = pltpu.PrefetchScalarGridSpec(
        num_scalar_prefetch=2,
        grid=(b, NSA_KV_HEADS, t_real),
        in_specs=[blk_spec(k) for k in range(N_SEL)] + [
            pl.BlockSpec((None, tp, NSA_GROUP * LANES), lambda bi, h, t, s, pt: (bi, 0, h)),
            pl.BlockSpec((None, tp, LANES), lambda bi, h, t, s, pt: (bi, 0, BLK_KVS * 4 + h)),
        ],
        out_specs=pl.BlockSpec((None, None, NSA_GROUP * tp, LANES), lambda bi, h, t, s, pt: (bi, h, 0, 0)),
    )
    return pl.pallas_call(
        functools.partial(_slc_sample_body, past_len=past_len, t_real=t_real),
        out_shape=jax.ShapeDtypeStruct((b, NSA_KV_HEADS, NSA_GROUP * tp, LANES), F32),
        grid_spec=grid_spec,
        compiler_params=_cparams(("arbitrary", "arbitrary", "arbitrary")),
        name="slc_sample",
    )(sel.reshape(-1), table, *([cache] * N_SEL), u_s, u_s)


def _win_combine_body(q_ref, misc_ref, win_ref, kvn_ref, oc_ref, os_ref, o_ref, *, past_len):
    tp = q_ref.shape[0]
    wb = win_ref.shape[0]
    t_pos = past_len + lax.broadcasted_iota(jnp.int32, (tp, 1), 0)
    gates = jax.nn.sigmoid(misc_ref[...])
    n_keys = wb + LANES
    idx = lax.broadcasted_iota(jnp.int32, (1, n_keys), 1)
    d = t_pos - (past_len - wb + idx)
    mask = (idx < wb + tp) & (d >= 0) & (d < WINDOW)
    heads = []
    for h in range(NSA_KV_HEADS):
        sl = slice(h * LANES, (h + 1) * LANES)
        qh = _stack_q(q_ref, h, HEAD_DIM ** -0.5)
        kv = jnp.concatenate([win_ref[:, sl], kvn_ref[:, sl], jnp.zeros((LANES - tp, LANES), F32)],
                             axis=0).astype(BF16)
        p_w = _softmax_groups(_dot_nt(qh, kv), d.astype(F32), mask, ALIBI[h], tp)
        o_w = _dot(jnp.concatenate(p_w, axis=0).astype(BF16), kv)
        o_c = oc_ref[h]
        o_s = os_ref[h]
        for g in range(NSA_GROUP):
            col = h * NSA_GROUP + g
            rows = slice(g * tp, (g + 1) * tp)
            heads.append(gates[:, col:col + 1] * o_c[rows]
                         + gates[:, NSA_HEADS + col:NSA_HEADS + col + 1] * o_s[rows]
                         + gates[:, 2 * NSA_HEADS + col:2 * NSA_HEADS + col + 1] * o_w[rows])
    _pack_heads(o_ref, heads)


def _win_combine(u_s, win, o_c, o_s, layer, past_len):
    b, tp, _ = u_s.shape
    wb = win.shape[2]
    return pl.pallas_call(
        functools.partial(_win_combine_body, past_len=past_len),
        out_shape=jax.ShapeDtypeStruct((b, tp, NSA_HEADS * HEAD_DIM), F32),
        grid=(b,),
        in_specs=[
            pl.BlockSpec((None, tp, NSA_HEADS * LANES), lambda bi: (bi, 0, 0)),
            pl.BlockSpec((None, tp, LANES), lambda bi: (bi, 0, BLK_MISC)),
            pl.BlockSpec((None, None, wb, 4 * LANES), lambda bi: (bi, layer, 0, 0)),
            pl.BlockSpec((None, tp, 4 * LANES), lambda bi: (bi, 0, BLK_KVW)),
            pl.BlockSpec((None,) + o_c.shape[1:], lambda bi: (bi, 0, 0, 0)),
            pl.BlockSpec((None,) + o_s.shape[1:], lambda bi: (bi, 0, 0, 0)),
        ],
        out_specs=pl.BlockSpec((None, tp, NSA_HEADS * HEAD_DIM), lambda bi: (bi, 0, 0)),
        compiler_params=_cparams(("parallel",)),
        name="win_combine",
    )(u_s, u_s, win, u_s, o_c, o_s)


def _cumsum_rows(x):
    n = x.shape[0]
    row = lax.broadcasted_iota(jnp.int32, x.shape, 0)
    shift = 1
    while shift < n:
        x = x + jnp.where(row >= shift, pltpu.roll(x, shift, 0), 0.0)
        shift *= 2
    return x


def _gla_body(gq_ref, gk_ref, gv_ref, gog_ref, misc_ref, wdec_ref, bdec_ref, gn_ref, s0_ref,
              o_ref, sout_ref, st_ref, *, c_real):
    ci = pl.program_id(1)
    cp = gq_ref.shape[0]
    sb = min(16, cp)

    @pl.when(ci == 0)
    def _():
        for h in range(GLA_HEADS):
            st_ref[h] = s0_ref[h].T

    row = lax.broadcasted_iota(jnp.int32, (cp, 1), 0)
    live = row < c_real
    x = jnp.dot(misc_ref[...], wdec_ref[...], preferred_element_type=F32, precision=HIGHEST) + bdec_ref[...]
    log_a = (jnp.minimum(x, 0.0) - jnp.log1p(jnp.exp(-jnp.abs(x)))) / GLA_GATE_TAU
    b_all = _cumsum_rows(jnp.where(live, log_a, 0.0))
    for h in range(GLA_HEADS):
        sl = slice(h * LANES, (h + 1) * LANES)
        q = gq_ref[:, sl] * GLA_DK ** -0.5
        k = jnp.where(live, gk_ref[:, sl], 0.0)
        v = jnp.where(live, gv_ref[:, sl], 0.0)
        b = b_all[:, sl]
        kb = k.astype(BF16)
        st = st_ref[h]
        attn_rows = []
        for i in range(cp // sb):
            qi = q[i * sb:(i + 1) * sb]
            bi = b[i * sb:(i + 1) * sb]
            sub_row = lax.broadcasted_iota(jnp.int32, (sb, 1), 0)
            ys = []
            for s_loc in range(sb):
                bs = b[i * sb + s_loc:i * sb + s_loc + 1]
                ys.append(qi * jnp.exp(jnp.where(sub_row >= s_loc, bi - bs, NEG_INF)))
            z = _dot_nt(jnp.concatenate(ys, axis=0).astype(BF16), kb)
            lane = lax.broadcasted_iota(jnp.int32, (sb, cp), 1)
            a_i = jnp.zeros((sb, cp), F32)
            for s_loc in range(sb):
                a_i += jnp.where(lane == i * sb + s_loc, z[s_loc * sb:(s_loc + 1) * sb], 0.0)
            if i > 0:
                ref_b = b[i * sb - 1:i * sb]
                qt = qi * jnp.exp(bi - ref_b)
                kt = k * jnp.exp(jnp.where(row < i * sb, ref_b - b, NEG_INF))
                a_i += _dot_nt(qt.astype(BF16), kt.astype(BF16))
            attn_rows.append(a_i)
        attn = jnp.concatenate(attn_rows, axis=0) if len(attn_rows) > 1 else attn_rows[0]
        o = _dot(attn.astype(BF16), v.astype(BF16))
        o += _dot_nt((q * jnp.exp(b)).astype(BF16), st.astype(BF16))
        b_last = b[c_real - 1:c_real]
        kd = k * jnp.exp(b_last - b)
        st_ref[h] = jnp.exp(b_last) * st + _dot(v.T.astype(BF16), kd.astype(BF16))
        o_ref[:, sl] = _rms(o, gn_ref[...]) * jax.nn.silu(gog_ref[:, sl])

    @pl.when(ci == pl.num_programs(1) - 1)
    def _():
        for h in range(GLA_HEADS):
            sout_ref[h] = st_ref[h].T[:GLA_DK, :]


def _gla(u, s0, w_dec, b_dec, g_norm, layer, cp, c_real):
    b, t, _ = u.shape
    assert t % cp == 0
    blk = lambda idx: pl.BlockSpec((None, cp, 4 * LANES), lambda bi, i: (bi, i, idx))
    return pl.pallas_call(
        functools.partial(_gla_body, c_real=c_real),
        out_shape=(jax.ShapeDtypeStruct((b, t, GLA_HEADS * GLA_DV), F32),
                   jax.ShapeDtypeStruct((b, GLA_HEADS, GLA_DK, GLA_DV), F32)),
        grid=(b, t // cp),
        in_specs=[
            blk(BLK_GQ), blk(BLK_GK), blk(BLK_GV), blk(BLK_GOG),
            pl.BlockSpec((None, cp, LANES), lambda bi, i: (bi, i, BLK_MISC)),
            pl.BlockSpec((None, LANES, 4 * LANES), lambda bi, i: (layer, 0, 0)),
            pl.BlockSpec((None, 1, 4 * LANES), lambda bi, i: (layer, 0, 0)),
            pl.BlockSpec((None, 1, GLA_DV), lambda bi, i: (layer, 0, 0)),
            pl.BlockSpec((None, GLA_HEADS, LANES, GLA_DV), lambda bi, i: (bi, 0, 0, 0)),
        ],
        out_specs=(pl.BlockSpec((None, cp, GLA_HEADS * GLA_DV), lambda bi, i: (bi, i, 0)),
                   pl.BlockSpec((None, GLA_HEADS, GLA_DK, GLA_DV), lambda bi, i: (bi, 0, 0, 0))),
        scratch_shapes=[pltpu.VMEM((GLA_HEADS, GLA_DV, LANES), F32)],
        compiler_params=_cparams(("parallel", "arbitrary")),
        name="gla",
    )(u, u, u, u, u, w_dec, b_dec, g_norm, s0)


def _ret_body(rq_ref, rk_ref, rv_ref, rg_ref, gn_ref, r0_ref, o_ref, rout_ref, rt_ref, *, c_real):
    ci = pl.program_id(1)
    cp = rq_ref.shape[0]

    @pl.when(ci == 0)
    def _():
        for h in range(RET_HEADS):
            rt_ref[h] = r0_ref[h].T

    row = lax.broadcasted_iota(jnp.int32, (cp, 1), 0)
    live = row < c_real
    rowf = row.astype(F32)
    rel = rowf - lax.broadcasted_iota(jnp.int32, (1, cp), 1).astype(F32)
    for h in range(RET_HEADS):
        sl = slice(h * LANES, (h + 1) * LANES)
        lg = RET_LOG_GAMMA[h]
        q = rq_ref[:, sl]
        k = jnp.where(live, rk_ref[:, sl] * RET_DK ** -0.5, 0.0)
        v = jnp.where(live, rv_ref[:, sl], 0.0)
        decay = jnp.where(rel >= 0, jnp.exp(jnp.maximum(rel, 0.0) * lg), 0.0)
        attn = _dot_nt(q.astype(BF16), k.astype(BF16)) * decay
        o = _dot(attn.astype(BF16), v.astype(BF16))
        rt = rt_ref[h]
        o += _dot_nt((q * jnp.exp((rowf + 1.0) * lg)).astype(BF16), rt.astype(BF16))
        kd = k * jnp.exp((c_real - 1.0 - rowf) * lg)
        rt_ref[h] = math.exp(c_real * lg) * rt + _dot(v.T.astype(BF16), kd.astype(BF16))
        mu = jnp.mean(o, axis=-1, keepdims=True)
        var = jnp.mean(jnp.square(o - mu), axis=-1, keepdims=True)
        o_ref[:, sl] = (o - mu) * lax.rsqrt(var + EPS) * gn_ref[:, sl] * jax.nn.silu(rg_ref[:, sl])

    @pl.when(ci == pl.num_programs(1) - 1)
    def _():
        for h in range(RET_HEADS):
            rout_ref[h] = rt_ref[h].T


def _retention(u, r0, g_norm, layer, cp, c_real):
    b, t, _ = u.shape
    blk = lambda idx: pl.BlockSpec((None, cp, 4 * LANES), lambda bi, i: (bi, i, idx))
    return pl.pallas_call(
        functools.partial(_ret_body, c_real=c_real),
        out_shape=(jax.ShapeDtypeStruct((b, t, RET_HEADS * RET_DV), F32),
                   jax.ShapeDtypeStruct((b, RET_HEADS, RET_DK, RET_DV), F32)),
        grid=(b, t // cp),
        in_specs=[
            blk(BLK_RQ), blk(BLK_RK), blk(BLK_RV), blk(BLK_RG),
            pl.BlockSpec((None, 1, RET_HEADS * RET_DV), lambda bi, i: (layer, 0, 0)),
            pl.BlockSpec((None, RET_HEADS, RET_DK, RET_DV), lambda bi, i: (bi, 0, 0, 0)),
        ],
        out_specs=(pl.BlockSpec((None, cp, RET_HEADS * RET_DV), lambda bi, i: (bi, i, 0)),
                   pl.BlockSpec((None, RET_HEADS, RET_DK, RET_DV), lambda bi, i: (bi, 0, 0, 0))),
        scratch_shapes=[pltpu.VMEM((RET_HEADS, RET_DV, RET_DK), F32)],
        compiler_params=_cparams(("parallel", "arbitrary")),
        name="retention",
    )(u, u, u, u, g_norm, r0)


def _pad_heads(w, n_heads):
    lead = w.shape[:-1]
    w = w.reshape(lead + (n_heads, HEAD_DIM))
    w = jnp.pad(w, [(0, 0)] * len(lead) + [(0, 0), (0, LANES - HEAD_DIM)])
    return w.reshape(lead + (n_heads * LANES,))


def _prep_w_in(w_in):
    cuts = np.cumsum((0,) + IN_SIZES)
    nq, kvc, kvs, kvw, ng, gq, gk, gv, glr, gog, rq, rk, rv, rg = [
        w_in[..., cuts[i]:cuts[i + 1]] for i in range(len(IN_SIZES))]
    misc = jnp.concatenate([ng, glr], axis=-1)
    misc = jnp.pad(misc, [(0, 0), (0, 0), (0, LANES - misc.shape[-1])])
    cols = [_pad_heads(nq, NSA_HEADS), kvc, kvs, kvw, _pad_heads(gq, GLA_HEADS), _pad_heads(gk, GLA_HEADS),
            gv, gog, rq, rk, rv, rg, misc]
    w = jnp.concatenate(cols, axis=-1)
    assert w.shape[-1] == (BLK_MISC + 1) * LANES
    w = jnp.pad(w, [(0, 0), (0, 0), (0, U_WIDTH - w.shape[-1])])
    return w.astype(BF16)


def _prep_cmp_weights(w_cmp1, w_cmp2):
    depth = w_cmp1.shape[0]
    half = CMP_STRIDE * HEAD_DIM
    w1 = w_cmp1.reshape(depth, 2, 2, CMP_STRIDE, HEAD_DIM, CMP_HIDDEN)
    w1 = w1.transpose(0, 3, 1, 4, 2, 5)
    z = jnp.zeros_like(w1[:, :, 0])
    k_rows = jnp.stack([w1[:, :, 0], z], axis=4)
    v_rows = jnp.stack([z, w1[:, :, 1]], axis=4)
    wblk = jnp.concatenate([k_rows, v_rows], axis=2)
    wblk = wblk.reshape(depth, CMP_STRIDE, 2 * HEAD_DIM, 4 * CMP_HIDDEN).astype(BF16)
    del half
    z2 = jnp.zeros_like(w_cmp2[:, 0])
    w2blk = jnp.concatenate([jnp.concatenate([w_cmp2[:, 0], z2], axis=-1),
                             jnp.concatenate([z2, w_cmp2[:, 1]], axis=-1)], axis=1).astype(BF16)
    return wblk, w2blk


def _overlap_matrix(n_rows, n_slc, n_cols):
    ratio = SEL_BLOCK // CMP_STRIDE
    w = np.zeros((n_rows, n_cols), np.float32)
    for j in range(n_slc):
        for k, wk in enumerate(OVERLAP_W):
            if ratio * j + k < n_rows:
                w[ratio * j + k, j] = wk
    return jnp.asarray(w)


def kernel(x_prompt, x_sample, cache_cmp_kv, cache_slc_kv, cache_win_kv, state_gla, state_ret, page_table,
           p_prompt, p_sample, norm_pre, norm_post, w_ffn_gate, w_ffn_up, w_ffn_down, w_in, w_out,
           w_cmp1, w_cmp2, cmp_pos, nsa_norm, w_gla_decay, b_gla_decay, gla_norm, ret_norm, w_ple, w_ple_gate):
    depth = w_in.shape[0]
    bp, seq, d_model = x_prompt.shape
    bs, t_real, _ = x_sample.shape
    n_pool = cache_cmp_kv.shape[0]
    n_pages = page_table.shape[1]
    past_len = n_pages * PAGE_SIZE
    tp = -(-t_real // SUBLANES) * SUBLANES
    kvw = NSA_KV_HEADS * 2 * HEAD_DIM

    wg = w_ffn_gate.astype(BF16)
    wu = w_ffn_up.astype(BF16)
    wd = w_ffn_down.astype(BF16)
    w_in_r = _prep_w_in(w_in)
    w_out_b = w_out.astype(BF16)
    w_ple_b = w_ple.astype(BF16)
    w_gate_b = w_ple_gate.astype(BF16)
    wblk, w2blk = _prep_cmp_weights(w_cmp1, w_cmp2)
    g_pre = norm_pre.reshape(depth, 3, 1, d_model)
    g_post = norm_post.reshape(depth, 3, 1, d_model)
    nsa_g = nsa_norm.reshape(depth, 1, -1)
    gla_g = gla_norm.reshape(depth, 1, -1)
    ret_g = ret_norm.reshape(depth, 1, -1)
    w_dec = jnp.pad(_pad_heads(w_gla_decay, GLA_HEADS),
                    [(0, 0), (MISC_GLR, LANES - MISC_GLR - GLA_GATE_RANK), (0, 0)])
    b_dec = _pad_heads(b_gla_decay, GLA_HEADS).reshape(depth, 1, -1)
    pos_flat = cmp_pos.reshape(depth, 2, 1, CMP_BLOCK * HEAD_DIM)

    cmp_pages = cache_cmp_kv.reshape(n_pool, depth, CHUNKS_PER_PAGE, PAGE_COLS)
    slc_rows = cache_slc_kv.reshape(n_pool, depth, PAGE_SIZE, kvw)
    win_rows = cache_win_kv.reshape(bs, depth, -1, kvw)
    wb = win_rows.shape[2]

    n_cr_p = seq // CMP_STRIDE
    wov_p = _overlap_matrix(n_cr_p, -(-seq // SEL_BLOCK), LANES)
    n_cr_s = past_len // CMP_STRIDE
    n_slc_s = -(-(past_len + t_real) // SEL_BLOCK)
    wov_s = _overlap_matrix(n_cr_s, n_slc_s, -(-n_slc_s // LANES) * LANES)
    table_p = jnp.arange(bp * (seq // PAGE_SIZE), dtype=jnp.int32).reshape(bp, seq // PAGE_SIZE)

    xp = x_prompt.reshape(bp * seq, d_model)
    xs = jnp.pad(x_sample, ((0, 0), (0, tp - t_real), (0, 0))).reshape(bs * tp, d_model)
    pp = p_prompt.reshape(depth, bp * seq, -1)
    ps = jnp.pad(p_sample, ((0, 0), (0, 0), (0, tp - t_real), (0, 0))).reshape(depth, bs * tp, -1)
    s0_gla_p = jnp.zeros((bp, GLA_HEADS, LANES, GLA_DV), F32)
    s0_ret_p = jnp.zeros((bp, RET_HEADS, RET_DK, RET_DV), F32)
    s0_gla_s = jnp.pad(state_gla, ((0, 0), (0, 0), (0, 0), (0, LANES - GLA_DK), (0, 0)))

    tm_p = 512
    tm_s = bs * tp
    st_p = [[] for _ in range(5)]
    st_s = [[] for _ in range(5)]
    for l in range(depth):
        bias = _cmp_bias(pos_flat, w_cmp1, l).reshape(1, 2 * CMP_HIDDEN)

        xp = _ffn(xp, g_pre, g_post, wg, wu, wd, l, 0, 0, tm_p)
        u2 = _proj_in(xp, g_pre, w_in_r, l, tm_p)
        u = u2.reshape(bp, seq, U_WIDTH)
        kv_c = u[..., BLK_KVC * kvw:(BLK_KVC + 1) * kvw]
        kv_s = u[..., BLK_KVS * kvw:(BLK_KVS + 1) * kvw]
        kv_w = u[..., BLK_KVW * kvw:(BLK_KVW + 1) * kvw]
        kc = _compress(kv_c.reshape(bp * (seq // PAGE_SIZE), 1, CHUNKS_PER_PAGE, PAGE_COLS), table_p, 0,
                       wblk[l], bias, w2blk[l])
        o_nsa = _nsa_prompt(u, kc, wov_p)
        gla_out, s_g = _gla(u, s0_gla_p, w_dec, b_dec, gla_g, l, GLA_CHUNK, GLA_CHUNK)
        ret_out, s_r = _retention(u, s0_ret_p, ret_g, l, RET_CHUNK, RET_CHUNK)
        xp = _proj_out(xp, o_nsa.reshape(bp * seq, -1), gla_out.reshape(bp * seq, -1),
                       ret_out.reshape(bp * seq, -1), nsa_g, w_out_b, g_post, l, tm_p)
        xp = _ffn(xp, g_pre, g_post, wg, wu, wd, l, 1, 2, tm_p)
        xp = _ple(xp, pp, w_gate_b, w_ple_b, l, tm_p)
        win_keep = min(WINDOW, seq)
        for j, a in enumerate((kv_c, kv_s, kv_w[:, seq - win_keep:], s_g, s_r)):
            st_p[j].append(a)

        xs = _ffn(xs, g_pre, g_post, wg, wu, wd, l, 0, 0, tm_s)
        us = _proj_in(xs, g_pre, w_in_r, l, tm_s).reshape(bs, tp, U_WIDTH)
        kv_c = us[:, :t_real, BLK_KVC * kvw:(BLK_KVC + 1) * kvw]
        kv_s = us[:, :t_real, BLK_KVS * kvw:(BLK_KVS + 1) * kvw]
        kv_w = us[:, :t_real, BLK_KVW * kvw:(BLK_KVW + 1) * kvw]
        kc = _compress(cmp_pages, page_table, l, wblk[l], bias, w2blk[l])
        o_c, sel = _cmp_select(us, kc, wov_s, past_len, t_real)
        o_s = _slc_sample(us, slc_rows, page_table, sel[:, :, :t_real, :N_SEL], l, past_len, t_real)
        o_nsa = _win_combine(us, win_rows, o_c, o_s, l, past_len)
        gla_out, s_g = _gla(us, s0_gla_s[:, l], w_dec, b_dec, gla_g, l, tp, t_real)
        ret_out, s_r = _retention(us, state_ret[:, l], ret_g, l, tp, t_real)
        xs = _proj_out(xs, o_nsa.reshape(bs * tp, -1), gla_out.reshape(bs * tp, -1),
                       ret_out.reshape(bs * tp, -1), nsa_g, w_out_b, g_post, l, tm_s)
        xs = _ffn(xs, g_pre, g_post, wg, wu, wd, l, 1, 2, tm_s)
        xs = _ple(xs, ps, w_gate_b, w_ple_b, l, tm_s)
        new_win = jnp.concatenate([win_rows[:, l], kv_w], axis=1)[:, t_real:]
        for j, a in enumerate((kv_c, kv_s, new_win, s_g, s_r)):
            st_s[j].append(a)

    def kv_stack(parts):
        a = jnp.stack(parts, axis=1)
        return a.reshape(a.shape[:3] + (NSA_KV_HEADS, 2, HEAD_DIM))

    y_p = xp.reshape(bp, seq, d_model)
    y_s = xs.reshape(bs, tp, d_model)[:, :t_real]
    return (y_p, y_s,
            kv_stack(st_p[0]), kv_stack(st_p[1]), kv_stack(st_p[2]),
            jnp.stack(st_p[3], axis=1), jnp.stack(st_p[4], axis=1),
            kv_stack(st_s[0]), kv_stack(st_s[1]), kv_stack(st_s[2]),
            jnp.stack(st_s[3], axis=1), jnp.stack(st_s[4], axis=1))
```

```python
import functools
import math

import numpy as np
import jax
import jax.numpy as jnp
from jax import lax
from jax.experimental import pallas as pl
from jax.experimental.pallas import tpu as pltpu

F32 = jnp.float32
BF16 = jnp.bfloat16
HIGHEST = lax.Precision.HIGHEST

HEAD_DIM = 64
NSA_HEADS = 16
NSA_KV_HEADS = 4
NSA_GROUP = 4
CMP_BLOCK = 32
CMP_STRIDE = 16
CMP_HIDDEN = 128
SEL_BLOCK = 64
N_SEL = 16
WINDOW = 512
Q_BLOCK = 128
OVERLAP_W = (0.5, 1.0, 1.0, 1.0, 0.5)
GLA_HEADS = 4
GLA_DK = 64
GLA_DV = 128
GLA_GATE_RANK = 16
GLA_GATE_TAU = 16.0
GLA_CHUNK = 64
RET_HEADS = 4
RET_DK = 128
RET_DV = 128
RET_CHUNK = 64
PAGE_SIZE = 128
NEG_INF = -1e30
FORCE = 1e9
EPS = 1e-6
IN_SIZES = (1024, 512, 512, 512, 48, 256, 256, 512, 16, 512, 512, 512, 512, 512)

LANES = 128
SUBLANES = 8
VMEM_LIMIT = 56 * 1024 * 1024

U_WIDTH = 6272
BLK_KVC, BLK_KVS, BLK_KVW = 2, 3, 4
BLK_GQK, BLK_GV, BLK_GOG = 5, 6, 7
BLK_RQ, BLK_RK, BLK_RV, BLK_RG = 8, 9, 10, 11
BLK_MISC = 48
MISC_GLR = 48
Q_HEAD_COLS = NSA_GROUP * HEAD_DIM

ALIBI = [[2.0 ** (-8.0 * (h * NSA_GROUP + g + 1) / NSA_HEADS) for g in range(NSA_GROUP)]
         for h in range(NSA_KV_HEADS)]
RET_LOG_GAMMA = [math.log1p(-(2.0 ** (-5.0 - h))) for h in range(RET_HEADS)]


def _cparams(sem):
    return pltpu.CompilerParams(dimension_semantics=sem, vmem_limit_bytes=VMEM_LIMIT)


def _rms(x, g=None):
    y = x * lax.rsqrt(jnp.mean(x * x, axis=-1, keepdims=True) + EPS)
    return y if g is None else y * g


def _dot(a, b):
    return jnp.dot(a, b, preferred_element_type=F32)


def _dot_nt(a, b):
    return lax.dot_general(a, b, (((1,), (1,)), ((), ())), preferred_element_type=F32)


def _ffn_body(x_ref, gpre_ref, wg_ref, wu_ref, wd_ref, gpost_ref, o_ref, xn_ref):
    f = pl.program_id(1)

    @pl.when(f == 0)
    def _():
        xn_ref[...] = _rms(x_ref[...], gpre_ref[...]).astype(BF16)
        o_ref[...] = jnp.zeros_like(o_ref)

    xn = xn_ref[...]
    h = jax.nn.silu(_dot(xn, wg_ref[...])) * _dot(xn, wu_ref[...])
    o_ref[...] += _dot(h.astype(BF16), wd_ref[...])

    @pl.when(f == pl.num_programs(1) - 1)
    def _():
        o_ref[...] = x_ref[...] + 0.5 * _rms(o_ref[...], gpost_ref[...])


def _ffn(x, g_pre, g_post, wg, wu, wd, layer, which, norm_idx, tm, tf=512):
    m, d = x.shape
    ff = wg.shape[-1]
    assert m % tm == 0 and ff % tf == 0
    return pl.pallas_call(
        _ffn_body,
        out_shape=jax.ShapeDtypeStruct((m, d), F32),
        grid=(m // tm, ff // tf),
        in_specs=[
            pl.BlockSpec((tm, d), lambda i, f: (i, 0), pipeline_mode=pl.Buffered(1)),
            pl.BlockSpec((None, None, 1, d), lambda i, f: (layer, norm_idx, 0, 0)),
            pl.BlockSpec((None, None, d, tf), lambda i, f: (layer, which, 0, f)),
            pl.BlockSpec((None, None, d, tf), lambda i, f: (layer, which, 0, f)),
            pl.BlockSpec((None, None, tf, d), lambda i, f: (layer, which, f, 0)),
            pl.BlockSpec((None, None, 1, d), lambda i, f: (layer, norm_idx, 0, 0)),
        ],
        out_specs=pl.BlockSpec((tm, d), lambda i, f: (i, 0)),
        scratch_shapes=[pltpu.VMEM((tm, d), BF16)],
        compiler_params=_cparams(("parallel", "arbitrary")),
        name="ffn",
    )(x, g_pre, wg, wu, wd, g_post)


def _proj_in_body(x_ref, g_ref, w_ref, o_ref, xn_ref):
    @pl.when(pl.program_id(1) == 0)
    def _():
        xn_ref[...] = _rms(x_ref[...], g_ref[...]).astype(BF16)

    o_ref[...] = _dot(xn_ref[...], w_ref[...])


def _proj_in(x, g_pre, w_in, layer, tm, tn=896):
    m, d = x.shape
    n = w_in.shape[-1]
    return pl.pallas_call(
        _proj_in_body,
        out_shape=jax.ShapeDtypeStruct((m, n), F32),
        grid=(m // tm, n // tn),
        in_specs=[
            pl.BlockSpec((tm, d), lambda i, j: (i, 0)),
            pl.BlockSpec((None, None, 1, d), lambda i, j: (layer, 1, 0, 0)),
            pl.BlockSpec((None, d, tn), lambda i, j: (layer, 0, j)),
        ],
        out_specs=pl.BlockSpec((tm, tn), lambda i, j: (i, j)),
        scratch_shapes=[pltpu.VMEM((tm, d), BF16)],
        compiler_params=_cparams(("parallel", "arbitrary")),
        name="proj_in",
    )(x, g_pre, w_in)


def _proj_out_body(x_ref, nsa_ref, gla_ref, ret_ref, gn_ref, w_ref, gpost_ref, o_ref):
    nsa_w = nsa_ref.shape[-1]
    gla_w = gla_ref.shape[-1]
    nsa = _rms(nsa_ref[...], gn_ref[...]).astype(BF16)
    y = _dot(nsa, w_ref[0:nsa_w, :])
    y += _dot(gla_ref[...].astype(BF16), w_ref[nsa_w:nsa_w + gla_w, :])
    y += _dot(ret_ref[...].astype(BF16), w_ref[nsa_w + gla_w:, :])
    o_ref[...] = x_ref[...] + _rms(y, gpost_ref[...])


def _proj_out(x, o_nsa, gla_out, ret_out, nsa_norm, w_out, g_post, layer, tm):
    m, d = x.shape
    row = lambda a: pl.BlockSpec((tm, a.shape[-1]), lambda i: (i, 0))
    return pl.pallas_call(
        _proj_out_body,
        out_shape=jax.ShapeDtypeStruct((m, d), F32),
        grid=(m // tm,),
        in_specs=[
            row(x), row(o_nsa), row(gla_out), row(ret_out),
            pl.BlockSpec((None, 1, o_nsa.shape[-1]), lambda i: (layer, 0, 0)),
            pl.BlockSpec((None, w_out.shape[1], d), lambda i: (layer, 0, 0)),
            pl.BlockSpec((None, None, 1, d), lambda i: (layer, 1, 0, 0)),
        ],
        out_specs=row(x),
        compiler_params=_cparams(("parallel",)),
        name="proj_out",
    )(x, o_nsa, gla_out, ret_out, nsa_norm, w_out, g_post)


def _ple_body(x_ref, p_ref, wg_ref, wp_ref, o_ref):
    x = x_ref[...]
    gate = jax.nn.sigmoid(_dot(_rms(x).astype(BF16), wg_ref[...]))
    o_ref[...] = x + gate * _dot(p_ref[...].astype(BF16), wp_ref[...])


def _ple(x, p, w_gate, w_ple, layer, tm):
    m, d = x.shape
    return pl.pallas_call(
        _ple_body,
        out_shape=jax.ShapeDtypeStruct((m, d), F32),
        grid=(m // tm,),
        in_specs=[
            pl.BlockSpec((tm, d), lambda i: (i, 0)),
            pl.BlockSpec((None, tm, p.shape[-1]), lambda i: (layer, i, 0)),
            pl.BlockSpec((None, d, d), lambda i: (layer, 0, 0)),
            pl.BlockSpec((None, p.shape[-1], d), lambda i: (layer, 0, 0)),
        ],
        out_specs=pl.BlockSpec((tm, d), lambda i: (i, 0)),
        compiler_params=_cparams(("parallel",)),
        name="ple",
    )(x, p, w_gate, w_ple)


PAGES_PER_STEP = 16
CHUNKS_PER_PAGE = PAGE_SIZE // CMP_STRIDE
KV_ROW = NSA_KV_HEADS * 2 * HEAD_DIM


def _cmp_bias_body(pos_ref, w1_ref, o_ref):
    o_ref[...] = jnp.dot(pos_ref[...], w1_ref[...], preferred_element_type=F32, precision=HIGHEST)


def _cmp_bias(cmp_pos, w_cmp1, layer):
    kdim = w_cmp1.shape[2]
    return pl.pallas_call(
        _cmp_bias_body,
        out_shape=jax.ShapeDtypeStruct((2, 1, CMP_HIDDEN), F32),
        grid=(2,),
        in_specs=[pl.BlockSpec((None, None, 1, kdim), lambda c: (layer, c, 0, 0)),
                  pl.BlockSpec((None, None, kdim, CMP_HIDDEN), lambda c: (layer, c, 0, 0))],
        out_specs=pl.BlockSpec((None, 1, CMP_HIDDEN), lambda c: (c, 0, 0)),
        compiler_params=_cparams(("arbitrary",)),
        name="cmp_bias",
    )(cmp_pos, w_cmp1)


def _compress_body(pt_ref, *refs):
    del pt_ref
    page_refs = refs[:PAGES_PER_STEP]
    wblk_ref, bias_ref, w2_ref, o_ref, prev_ref = refs[PAGES_PER_STEP:]
    rows = PAGES_PER_STEP * CHUNKS_PER_PAGE
    half = 2 * CMP_HIDDEN

    @pl.when(pl.program_id(1) == 0)
    def _():
        prev_ref[...] = jnp.zeros_like(prev_ref)

    first_row = lax.broadcasted_iota(jnp.int32, (rows, half), 0) == 0

    def token_rows(t, h):
        return jnp.concatenate([pr[pl.ds(t * NSA_KV_HEADS + h, CHUNKS_PER_PAGE, stride=CMP_STRIDE * NSA_KV_HEADS), :]
                                for pr in page_refs], axis=0)

    for h in range(NSA_KV_HEADS):
        acc = jnp.zeros((rows, 2 * half), F32)
        for tt in range(CMP_STRIDE // 2):
            lhs = jnp.concatenate([token_rows(2 * tt, h), token_rows(2 * tt + 1, h)], axis=1).astype(BF16)
            acc += _dot(lhs, wblk_ref[tt])
        first = acc[:, :half]
        second = acc[:, half:]
        carry = prev_ref[h][SUBLANES - 1:SUBLANES, :]
        shifted = jnp.where(first_row, carry, pltpu.roll(first, 1, 0))
        prev_ref[h] = first[rows - SUBLANES:, :]
        hidden = jax.nn.gelu(shifted + second + bias_ref[...])
        o_ref[:, h * LANES:(h + 1) * LANES] = _dot(hidden.astype(BF16), w2_ref[...])


def _compress(pages, table, layer, wblk, bias, w2blk):
    b, n_pages = table.shape
    assert n_pages % PAGES_PER_STEP == 0
    rows = PAGES_PER_STEP * CHUNKS_PER_PAGE

    def page_spec(k):
        return pl.BlockSpec((None, None, PAGE_SIZE * NSA_KV_HEADS, LANES),
                            lambda bi, i, pt: (pt[bi, i * PAGES_PER_STEP + k], layer, 0, 0))

    grid_spec = pltpu.PrefetchScalarGridSpec(
        num_scalar_prefetch=1,
        grid=(b, n_pages // PAGES_PER_STEP),
        in_specs=[page_spec(k) for k in range(PAGES_PER_STEP)] + [
            pl.BlockSpec(wblk.shape, lambda bi, i, pt: (0, 0, 0)),
            pl.BlockSpec(bias.shape, lambda bi, i, pt: (0, 0)),
            pl.BlockSpec(w2blk.shape, lambda bi, i, pt: (0, 0)),
        ],
        out_specs=pl.BlockSpec((None, rows, NSA_KV_HEADS * LANES), lambda bi, i, pt: (bi, i, 0)),
        scratch_shapes=[pltpu.VMEM((NSA_KV_HEADS, SUBLANES, 2 * CMP_HIDDEN), F32)],
    )
    return pl.pallas_call(
        _compress_body,
        out_shape=jax.ShapeDtypeStruct((b, n_pages * CHUNKS_PER_PAGE, NSA_KV_HEADS * LANES), F32),
        grid_spec=grid_spec,
        compiler_params=_cparams(("arbitrary", "arbitrary")),
        name="compress",
    )(table, *([pages] * PAGES_PER_STEP), wblk, bias, w2blk)


def _softmax_groups(s, distf, mask, slopes, rows):
    parts = []
    for g in range(NSA_GROUP):
        sg = s[g * rows:(g + 1) * rows] - slopes[g] * distf
        sg = jnp.where(mask, sg, NEG_INF)
        m = jnp.max(sg, axis=-1, keepdims=True)
        e = jnp.where(mask, jnp.exp(sg - m), 0.0)
        parts.append(e / jnp.maximum(jnp.sum(e, axis=-1, keepdims=True), 1e-30))
    return parts


def _unpack_head(pair, odd):
    lane = lax.broadcasted_iota(jnp.int32, pair.shape, 1)
    return jnp.where(lane < HEAD_DIM, pltpu.roll(pair, HEAD_DIM, 1) if odd else pair, 0.0)


def _stack_q(q_ref, base, scale):
    parts = []
    for g in range(NSA_GROUP):
        c0 = base + (g // 2) * LANES
        parts.append(_unpack_head(q_ref[:, c0:c0 + LANES], g % 2 == 1))
    return (jnp.concatenate(parts, axis=0) * scale).astype(BF16)


def _block_scores(imp, wov_ref, t_pos, n_slc):
    score = jnp.dot(imp, wov_ref[...], preferred_element_type=F32, precision=HIGHEST)
    blk = lax.broadcasted_iota(jnp.int32, score.shape, 1)
    cur = t_pos // SEL_BLOCK
    valid = blk <= cur
    forced = valid & ((blk == 0) | (blk == cur) | (blk == cur - 1))
    score = jnp.where(forced, FORCE, jnp.where(valid, score, -FORCE))
    return score, blk


def _pack_heads(o_ref, heads):
    lane = lax.broadcasted_iota(jnp.int32, heads[0].shape, 1)
    for p in range(NSA_HEADS // 2):
        even = pltpu.roll(heads[2 * p], HEAD_DIM, 1)
        o_ref[:, p * LANES:(p + 1) * LANES] = jnp.where(lane < HEAD_DIM, even, heads[2 * p + 1])


SLC_KEY_CHUNK = 512
LOG2E = 1.4426950408889634
MASK_BIG = 2.0 ** 100
ALIBI2 = [[s * LOG2E for s in row] for row in ALIBI]


def _exp2_softmax(s2, rel_row, bias, slopes2, rows):
    out = []
    for g in range(NSA_GROUP):
        lg = s2[g * rows:(g + 1) * rows] + (slopes2[g] * rel_row + bias)
        e = jnp.exp2(lg - jnp.max(lg, axis=-1, keepdims=True))
        out.append((e, jnp.sum(e, axis=-1, keepdims=True)))
    return out


def _nsa_prompt_body(q_ref, misc_ref, kc_ref, kvs_ref, kvw_ref, wovt_ref, o_ref, *, seq):
    qb = pl.program_id(1)
    t0 = qb * Q_BLOCK
    t0f = t0.astype(F32)
    n_slc = -(-seq // SEL_BLOCK)
    n_sel = min(N_SEL, n_slc)
    n_cr = kc_ref.shape[0]
    t_col = (t0 + lax.broadcasted_iota(jnp.int32, (Q_BLOCK, 1), 0)).astype(F32)
    gates = jax.nn.sigmoid(misc_ref[...])

    r_idx = lax.broadcasted_iota(jnp.int32, (1, n_cr), 1)
    end_c = (r_idx * CMP_STRIDE + (CMP_STRIDE - 1)).astype(F32)
    bias_c = jnp.minimum(t_col - end_c, 0.0) * MASK_BIG + jnp.where(r_idx >= 1, 0.0, -MASK_BIG)
    rel_c = end_c - t0f
    row_live = jnp.where(t_col >= CMP_BLOCK - 1, 1.0, 0.0)
    band = WINDOW + Q_BLOCK
    w0 = pl.multiple_of(jnp.maximum(t0 - WINDOW, 0), Q_BLOCK)
    pos_w = (w0 + lax.broadcasted_iota(jnp.int32, (1, band), 1)).astype(F32)
    d_w = t_col - pos_w
    bias_w = (jnp.minimum(d_w, 0.0) + jnp.minimum((WINDOW - 1.0) - d_w, 0.0)) * MASK_BIG
    rel_w = pos_w - t0f

    blk_t = lax.broadcasted_iota(jnp.int32, (n_slc, Q_BLOCK), 0)
    cur_t = (t0 + lax.broadcasted_iota(jnp.int32, (1, Q_BLOCK), 1)) // SEL_BLOCK
    valid_t = blk_t <= cur_t
    forced_t = valid_t & ((blk_t == 0) | (blk_t == cur_t) | (blk_t == cur_t - 1))
    tm1 = t_col - 1.0

    kv_heads = range(NSA_KV_HEADS)
    lane_sl = [slice(h * LANES, (h + 1) * LANES) for h in kv_heads]
    qhs = [_stack_q(q_ref, h * Q_HEAD_COLS, HEAD_DIM ** -0.5 * LOG2E) for h in kv_heads]

    o_cs, scores = [], []
    for h in kv_heads:
        kc = kc_ref[:, lane_sl[h]].astype(BF16)
        sm_c = _exp2_softmax(_dot_nt(qhs[h], kc), rel_c, bias_c, ALIBI2[h], Q_BLOCK)
        p_c = [e * (row_live / l) for (e, l) in sm_c]
        o_cs.append(_dot(jnp.concatenate(p_c, axis=0).astype(BF16), kc))
        imp = p_c[0] + p_c[1] + p_c[2] + p_c[3]
        score_t = lax.dot_general(wovt_ref[...], imp, (((1,), (1,)), ((), ())),
                                  preferred_element_type=F32, precision=HIGHEST)[:n_slc]
        scores.append(jnp.where(forced_t, FORCE, jnp.where(valid_t, score_t, -FORCE)))

    def ranked():
        out = []
        for score_t in scores:
            rank = jnp.zeros(score_t.shape, F32)
            for i in range(n_slc):
                ci = score_t[i:i + 1, :]
                tie = jnp.where(blk_t > i, 1.0, 0.0)
                rank += jnp.where(ci > score_t, 1.0, jnp.where(ci == score_t, tie, 0.0))
            out.append(jnp.where(rank < n_sel, MASK_BIG, 0.0))
        return tuple(out)

    sel_ts = lax.cond(t0 + Q_BLOCK > n_sel * SEL_BLOCK, ranked,
                      lambda: tuple(jnp.where(valid_t, MASK_BIG, 0.0) for _ in kv_heads))
    sels = [jnp.concatenate([s, jnp.zeros((LANES - n_slc, Q_BLOCK), F32)], axis=0).T.astype(BF16)
            for s in sel_ts]

    def chunk(c, carry):
        k0 = pl.multiple_of(c * SLC_KEY_CHUNK, SLC_KEY_CHUNK)
        pos = k0 + lax.broadcasted_iota(jnp.int32, (1, SLC_KEY_CHUNK), 1)
        posf = pos.astype(F32)
        expand = jnp.where(
            (pos // SEL_BLOCK) == lax.broadcasted_iota(jnp.int32, (LANES, SLC_KEY_CHUNK), 0),
            1.0, 0.0).astype(BF16)
        causal = jnp.minimum(tm1 - posf, -1.0) * MASK_BIG
        rel = posf - t0f
        new = []
        for h in kv_heads:
            kv = kvs_ref[pl.ds(k0, SLC_KEY_CHUNK), lane_sl[h]].astype(BF16)
            s2 = _dot_nt(qhs[h], kv)
            bias = _dot(sels[h], expand) + causal
            out = []
            for g in range(NSA_GROUP):
                m_old, l_old, a_old = carry[h][g]
                lg = s2[g * Q_BLOCK:(g + 1) * Q_BLOCK] + (ALIBI2[h][g] * rel + bias)
                m_new = jnp.maximum(m_old, jnp.max(lg, axis=-1, keepdims=True))
                alpha = jnp.exp2(m_old - m_new)
                e = jnp.exp2(lg - m_new)
                l_new = alpha * l_old + jnp.sum(e, axis=-1, keepdims=True)
                a_new = alpha * a_old + _dot(e.astype(BF16), kv)
                out.append((m_new, l_new, a_new))
            new.append(tuple(out))
        return tuple(new)

    init = tuple(tuple((jnp.full((Q_BLOCK, 1), NEG_INF, F32), jnp.zeros((Q_BLOCK, 1), F32),
                        jnp.zeros((Q_BLOCK, LANES), F32)) for _ in range(NSA_GROUP)) for _ in kv_heads)
    n_chunks = (t0 + Q_BLOCK + SLC_KEY_CHUNK - 1) // SLC_KEY_CHUNK
    fin = lax.fori_loop(0, n_chunks, chunk, init)

    heads = []
    for h in kv_heads:
        kvw = kvw_ref[pl.ds(w0, band), lane_sl[h]].astype(BF16)
        sm_w = _exp2_softmax(_dot_nt(qhs[h], kvw), rel_w, bias_w, ALIBI2[h], Q_BLOCK)
        o_w = _dot(jnp.concatenate([e for (e, _) in sm_w], axis=0).astype(BF16), kvw)
        for g in range(NSA_GROUP):
            col = h * NSA_GROUP + g
            rows = slice(g * Q_BLOCK, (g + 1) * Q_BLOCK)
            heads.append(gates[:, col:col + 1] * o_cs[h][rows]
                         + (gates[:, NSA_HEADS + col:NSA_HEADS + col + 1] / fin[h][g][1]) * fin[h][g][2]
                         + (gates[:, 2 * NSA_HEADS + col:2 * NSA_HEADS + col + 1] / sm_w[g][1]) * o_w[rows])
    _pack_heads(o_ref, heads)


def _nsa_prompt(u, kc, wov):
    b, t, _ = u.shape
    assert t % Q_BLOCK == 0 and t >= WINDOW + Q_BLOCK and t % SLC_KEY_CHUNK == 0
    n_slc = -(-t // SEL_BLOCK)
    assert n_slc % SUBLANES == 0 and n_slc <= LANES
    return pl.pallas_call(
        functools.partial(_nsa_prompt_body, seq=t),
        out_shape=jax.ShapeDtypeStruct((b, t, NSA_HEADS * HEAD_DIM), F32),
        grid=(b, t // Q_BLOCK),
        in_specs=[
            pl.BlockSpec((None, Q_BLOCK, NSA_HEADS * HEAD_DIM), lambda bi, i: (bi, i, 0)),
            pl.BlockSpec((None, Q_BLOCK, LANES), lambda bi, i: (bi, i, BLK_MISC)),
            pl.BlockSpec((None,) + kc.shape[1:], lambda bi, i: (bi, 0, 0)),
            pl.BlockSpec((None, t, 4 * LANES), lambda bi, i: (bi, 0, BLK_KVS)),
            pl.BlockSpec((None, t, 4 * LANES), lambda bi, i: (bi, 0, BLK_KVW)),
            pl.BlockSpec(wov.shape, lambda bi, i: (0, 0)),
        ],
        out_specs=pl.BlockSpec((None, Q_BLOCK, NSA_HEADS * HEAD_DIM), lambda bi, i: (bi, i, 0)),
        compiler_params=_cparams(("parallel", "arbitrary")),
        name="nsa_prompt",
    )(u, u, kc, u, u, wov)


def _cmp_select_body(q_ref, kc_ref, wov_ref, oc_ref, sel_ref, *, past_len, t_real):
    tp = q_ref.shape[0]
    n_cr = kc_ref.shape[0]
    n_slc = -(-(past_len + t_real) // SEL_BLOCK)
    t_pos = past_len + lax.broadcasted_iota(jnp.int32, (tp, 1), 0)
    for h in range(NSA_KV_HEADS):
        qh = _stack_q(q_ref, h * Q_HEAD_COLS, HEAD_DIM ** -0.5)
        kc = kc_ref[:, h * LANES:(h + 1) * LANES].astype(BF16)
        r_idx = lax.broadcasted_iota(jnp.int32, (1, n_cr), 1)
        dist = t_pos - (r_idx * CMP_STRIDE + (CMP_STRIDE - 1))
        mask = (r_idx >= 1) & (dist >= 0)
        p_c = _softmax_groups(_dot_nt(qh, kc), dist.astype(F32), mask, ALIBI[h], tp)
        oc_ref[h] = _dot(jnp.concatenate(p_c, axis=0).astype(BF16), kc)
        imp = p_c[0] + p_c[1] + p_c[2] + p_c[3]
        score, blk = _block_scores(imp, wov_ref, t_pos, n_slc)
        work = jnp.where(blk < n_slc, score, -3e38)
        blkf = blk.astype(F32)
        picked = jnp.zeros((tp, LANES), jnp.int32)
        lane = lax.broadcasted_iota(jnp.int32, (tp, LANES), 1)
        for k in range(N_SEL):
            m = jnp.max(work, axis=-1, keepdims=True)
            idx = jnp.min(jnp.where(work == m, blkf, 3e38), axis=-1, keepdims=True)
            picked = jnp.where(lane == k, idx.astype(jnp.int32), picked)
            work = jnp.where(blkf == idx, -3e38, work)
        sel_ref[h] = picked


def _cmp_select(u_s, kc, wov, past_len, t_real):
    b, tp, _ = u_s.shape
    assert -(-(past_len + t_real) // SEL_BLOCK) >= N_SEL
    return pl.pallas_call(
        functools.partial(_cmp_select_body, past_len=past_len, t_real=t_real),
        out_shape=(jax.ShapeDtypeStruct((b, NSA_KV_HEADS, NSA_GROUP * tp, LANES), F32),
                   jax.ShapeDtypeStruct((b, NSA_KV_HEADS, tp, LANES), jnp.int32)),
        grid=(b,),
        in_specs=[
            pl.BlockSpec((None, tp, NSA_HEADS * HEAD_DIM), lambda bi: (bi, 0, 0)),
            pl.BlockSpec((None,) + kc.shape[1:], lambda bi: (bi, 0, 0)),
            pl.BlockSpec(wov.shape, lambda bi: (0, 0)),
        ],
        out_specs=(pl.BlockSpec((None, NSA_KV_HEADS, NSA_GROUP * tp, LANES), lambda bi: (bi, 0, 0, 0)),
                   pl.BlockSpec((None, NSA_KV_HEADS, tp, LANES), lambda bi: (bi, 0, 0, 0))),
        compiler_params=_cparams(("parallel",)),
        name="cmp_select",
    )(u_s, kc, wov)


def _slc_sample_body(sel_ref, pt_ref, *refs, past_len, t_real):
    del pt_ref
    blk_refs = refs[:N_SEL]
    q_ref, kvn_ref, o_ref = refs[N_SEL:]
    bi, h, t = pl.program_id(0), pl.program_id(1), pl.program_id(2)
    tp = kvn_ref.shape[0]
    rows = NSA_GROUP * tp
    n_past_blk = past_len // SEL_BLOCK
    qh = _stack_q(q_ref, 0, HEAD_DIM ** -0.5)
    row = lax.broadcasted_iota(jnp.int32, (rows, 1), 0)
    t_pos = past_len + row % tp

    n_keys = N_SEL * SEL_BLOCK
    lane = lax.broadcasted_iota(jnp.int32, (1, n_keys), 1)
    slot = lane // SEL_BLOCK
    pos = lane % SEL_BLOCK
    ok = jnp.zeros((1, n_keys), jnp.int32)
    for k in range(N_SEL):
        blk = sel_ref[((bi * NSA_KV_HEADS + h) * t_real + t) * N_SEL + k]
        pos = pos + jnp.where(slot == k, blk * SEL_BLOCK, 0)
        ok = ok + jnp.where((slot == k) & (blk < n_past_blk), 1, 0)
    kv = jnp.concatenate([r[...] for r in blk_refs], axis=0).astype(BF16)
    d = t_pos - pos
    msk = (ok > 0) & (d >= 0)
    kvn = jnp.concatenate([kvn_ref[...], jnp.zeros((LANES - tp, LANES), F32)], axis=0).astype(BF16)
    idx_n = lax.broadcasted_iota(jnp.int32, (1, LANES), 1)
    dn = t_pos - (past_len + idx_n)
    mskn = (idx_n < tp) & (dn >= 0)
    s_all = _dot_nt(qh, kv)
    s_new = _dot_nt(qh, kvn)
    outs = []
    for g in range(NSA_GROUP):
        r0 = slice(g * tp, (g + 1) * tp)
        slope = jnp.where(h == 0, ALIBI[0][g], jnp.where(h == 1, ALIBI[1][g],
                                                         jnp.where(h == 2, ALIBI[2][g], ALIBI[3][g])))
        sg = jnp.where(msk[r0], s_all[r0] - slope * d[r0].astype(F32), NEG_INF)
        sn = jnp.where(mskn[r0], s_new[r0] - slope * dn[r0].astype(F32), NEG_INF)
        m = jnp.maximum(jnp.max(sg, axis=-1, keepdims=True), jnp.max(sn, axis=-1, keepdims=True))
        e = jnp.where(msk[r0], jnp.exp(sg - m), 0.0)
        en = jnp.where(mskn[r0], jnp.exp(sn - m), 0.0)
        l = jnp.sum(e, axis=-1, keepdims=True) + jnp.sum(en, axis=-1, keepdims=True)
        outs.append((_dot(e.astype(BF16), kv) + _dot(en.astype(BF16), kvn)) / jnp.maximum(l, 1e-30))
    res = jnp.concatenate(outs, axis=0)

    @pl.when(t == 0)
    def _():
        o_ref[...] = jnp.zeros_like(o_ref)

    o_ref[...] = jnp.where(row % tp == t, res, o_ref[...])


def _slc_sample(u_s, cache, table, sel, layer, past_len, t_real):
    b, tp, _ = u_s.shape
    n_past_blk = past_len // SEL_BLOCK
    per_page = PAGE_SIZE // SEL_BLOCK
    assert past_len % SEL_BLOCK == 0 and t_real <= SEL_BLOCK and PAGE_SIZE % SEL_BLOCK == 0

    def blk_spec(k):
        def imap(bi, h, t, sel_ref, pt):
            blk = jnp.minimum(sel_ref[((bi * NSA_KV_HEADS + h) * t_real + t) * N_SEL + k], n_past_blk - 1)
            return (pt[bi, blk // per_page], layer, blk % per_page, h)
        return pl.BlockSpec((None, None, SEL_BLOCK, LANES), imap)

    grid_spec = pltpu.PrefetchScalarGridSpec(
        num_scalar_prefetch=2,
        grid=(b, NSA_KV_HEADS, t_real),
        in_specs=[blk_spec(k) for k in range(N_SEL)] + [
            pl.BlockSpec((None, tp, Q_HEAD_COLS), lambda bi, h, t, s, pt: (bi, 0, h)),
            pl.BlockSpec((None, tp, LANES), lambda bi, h, t, s, pt: (bi, 0, BLK_KVS * 4 + h)),
        ],
        out_specs=pl.BlockSpec((None, None, NSA_GROUP * tp, LANES), lambda bi, h, t, s, pt: (bi, h, 0, 0)),
    )
    return pl.pallas_call(
        functools.partial(_slc_sample_body, past_len=past_len, t_real=t_real),
        out_shape=jax.ShapeDtypeStruct((b, NSA_KV_HEADS, NSA_GROUP * tp, LANES), F32),
        grid_spec=grid_spec,
        compiler_params=_cparams(("arbitrary", "arbitrary", "arbitrary")),
        name="slc_sample",
    )(sel.reshape(-1), table, *([cache] * N_SEL), u_s, u_s)


def _win_combine_body(q_ref, misc_ref, win_ref, kvn_ref, oc_ref, os_ref, o_ref, *, past_len):
    tp = q_ref.shape[0]
    wb = win_ref.shape[0]
    t_pos = past_len + lax.broadcasted_iota(jnp.int32, (tp, 1), 0)
    gates = jax.nn.sigmoid(misc_ref[...])
    n_keys = wb + LANES
    idx = lax.broadcasted_iota(jnp.int32, (1, n_keys), 1)
    d = t_pos - (past_len - wb + idx)
    mask = (idx < wb + tp) & (d >= 0) & (d < WINDOW)
    heads = []
    for h in range(NSA_KV_HEADS):
        sl = slice(h * LANES, (h + 1) * LANES)
        qh = _stack_q(q_ref, h * Q_HEAD_COLS, HEAD_DIM ** -0.5)
        kv = jnp.concatenate([win_ref[:, sl], kvn_ref[:, sl], jnp.zeros((LANES - tp, LANES), F32)],
                             axis=0).astype(BF16)
        p_w = _softmax_groups(_dot_nt(qh, kv), d.astype(F32), mask, ALIBI[h], tp)
        o_w = _dot(jnp.concatenate(p_w, axis=0).astype(BF16), kv)
        o_c = oc_ref[h]
        o_s = os_ref[h]
        for g in range(NSA_GROUP):
            col = h * NSA_GROUP + g
            rows = slice(g * tp, (g + 1) * tp)
            heads.append(gates[:, col:col + 1] * o_c[rows]
                         + gates[:, NSA_HEADS + col:NSA_HEADS + col + 1] * o_s[rows]
                         + gates[:, 2 * NSA_HEADS + col:2 * NSA_HEADS + col + 1] * o_w[rows])
    _pack_heads(o_ref, heads)


def _win_combine(u_s, win, o_c, o_s, layer, past_len):
    b, tp, _ = u_s.shape
    wb = win.shape[2]
    return pl.pallas_call(
        functools.partial(_win_combine_body, past_len=past_len),
        out_shape=jax.ShapeDtypeStruct((b, tp, NSA_HEADS * HEAD_DIM), F32),
        grid=(b,),
        in_specs=[
            pl.BlockSpec((None, tp, NSA_HEADS * HEAD_DIM), lambda bi: (bi, 0, 0)),
            pl.BlockSpec((None, tp, LANES), lambda bi: (bi, 0, BLK_MISC)),
            pl.BlockSpec((None, None, wb, 4 * LANES), lambda bi: (bi, layer, 0, 0)),
            pl.BlockSpec((None, tp, 4 * LANES), lambda bi: (bi, 0, BLK_KVW)),
            pl.BlockSpec((None,) + o_c.shape[1:], lambda bi: (bi, 0, 0, 0)),
            pl.BlockSpec((None,) + o_s.shape[1:], lambda bi: (bi, 0, 0, 0)),
        ],
        out_specs=pl.BlockSpec((None, tp, NSA_HEADS * HEAD_DIM), lambda bi: (bi, 0, 0)),
        compiler_params=_cparams(("parallel",)),
        name="win_combine",
    )(u_s, u_s, win, u_s, o_c, o_s)


def _cumsum_rows(x):
    n = x.shape[0]
    row = lax.broadcasted_iota(jnp.int32, x.shape, 0)
    shift = 1
    while shift < n:
        x = x + jnp.where(row >= shift, pltpu.roll(x, shift, 0), 0.0)
        shift *= 2
    return x


def _gla_body(gqk_ref, gv_ref, gog_ref, misc_ref, wdec_ref, bdec_ref, gn_ref, s0_ref,
              o_ref, sout_ref, st_ref, *, c_real):
    ci = pl.program_id(1)
    cp = gqk_ref.shape[0]
    k_base = GLA_HEADS * GLA_DK
    sb = min(16, cp)

    @pl.when(ci == 0)
    def _():
        for h in range(GLA_HEADS):
            st_ref[h] = s0_ref[h].T

    row = lax.broadcasted_iota(jnp.int32, (cp, 1), 0)
    live = row < c_real
    x = jnp.dot(misc_ref[...], wdec_ref[...], preferred_element_type=F32, precision=HIGHEST) + bdec_ref[...]
    log_a = (jnp.minimum(x, 0.0) - jnp.log1p(jnp.exp(-jnp.abs(x)))) / GLA_GATE_TAU
    b_all = _cumsum_rows(jnp.where(live, log_a, 0.0))
    for h in range(GLA_HEADS):
        sl = slice(h * LANES, (h + 1) * LANES)
        pair = slice((h // 2) * LANES, (h // 2 + 1) * LANES)
        q = _unpack_head(gqk_ref[:, pair], h % 2 == 1) * GLA_DK ** -0.5
        k_pair = slice(k_base + (h // 2) * LANES, k_base + (h // 2 + 1) * LANES)
        k = jnp.where(live, _unpack_head(gqk_ref[:, k_pair], h % 2 == 1), 0.0)
        v = jnp.where(live, gv_ref[:, sl], 0.0)
        b = b_all[:, sl]
        kb = k.astype(BF16)
        st = st_ref[h]
        attn_rows = []
        for i in range(cp // sb):
            qi = q[i * sb:(i + 1) * sb]
            bi = b[i * sb:(i + 1) * sb]
            sub_row = lax.broadcasted_iota(jnp.int32, (sb, 1), 0)
            ys = []
            for s_loc in range(sb):
                bs = b[i * sb + s_loc:i * sb + s_loc + 1]
                ys.append(qi * jnp.exp(jnp.where(sub_row >= s_loc, bi - bs, NEG_INF)))
            z = _dot_nt(jnp.concatenate(ys, axis=0).astype(BF16), kb)
            lane = lax.broadcasted_iota(jnp.int32, (sb, cp), 1)
            a_i = jnp.zeros((sb, cp), F32)
            for s_loc in range(sb):
                a_i += jnp.where(lane == i * sb + s_loc, z[s_loc * sb:(s_loc + 1) * sb], 0.0)
            if i > 0:
                ref_b = b[i * sb - 1:i * sb]
                qt = qi * jnp.exp(bi - ref_b)
                kt = k * jnp.exp(jnp.where(row < i * sb, ref_b - b, NEG_INF))
                a_i += _dot_nt(qt.astype(BF16), kt.astype(BF16))
            attn_rows.append(a_i)
        attn = jnp.concatenate(attn_rows, axis=0) if len(attn_rows) > 1 else attn_rows[0]
        o = _dot(attn.astype(BF16), v.astype(BF16))
        o += _dot_nt((q * jnp.exp(b)).astype(BF16), st.astype(BF16))
        b_last = b[c_real - 1:c_real]
        kd = k * jnp.exp(b_last - b)
        st_ref[h] = jnp.exp(b_last) * st + _dot(v.T.astype(BF16), kd.astype(BF16))
        o_ref[:, sl] = _rms(o, gn_ref[...]) * jax.nn.silu(gog_ref[:, sl])

    @pl.when(ci == pl.num_programs(1) - 1)
    def _():
        for h in range(GLA_HEADS):
            sout_ref[h] = st_ref[h].T[:GLA_DK, :]


def _gla(u, s0, w_dec, b_dec, g_norm, layer, cp, c_real):
    b, t, _ = u.shape
    assert t % cp == 0
    blk = lambda idx: pl.BlockSpec((None, cp, 4 * LANES), lambda bi, i: (bi, i, idx))
    return pl.pallas_call(
        functools.partial(_gla_body, c_real=c_real),
        out_shape=(jax.ShapeDtypeStruct((b, t, GLA_HEADS * GLA_DV), F32),
                   jax.ShapeDtypeStruct((b, GLA_HEADS, GLA_DK, GLA_DV), F32)),
        grid=(b, t // cp),
        in_specs=[
            blk(BLK_GQK), blk(BLK_GV), blk(BLK_GOG),
            pl.BlockSpec((None, cp, LANES), lambda bi, i: (bi, i, BLK_MISC)),
            pl.BlockSpec((None, LANES, 4 * LANES), lambda bi, i: (layer, 0, 0)),
            pl.BlockSpec((None, 1, 4 * LANES), lambda bi, i: (layer, 0, 0)),
            pl.BlockSpec((None, 1, GLA_DV), lambda bi, i: (layer, 0, 0)),
            pl.BlockSpec((None, GLA_HEADS, LANES, GLA_DV), lambda bi, i: (bi, 0, 0, 0)),
        ],
        out_specs=(pl.BlockSpec((None, cp, GLA_HEADS * GLA_DV), lambda bi, i: (bi, i, 0)),
                   pl.BlockSpec((None, GLA_HEADS, GLA_DK, GLA_DV), lambda bi, i: (bi, 0, 0, 0))),
        scratch_shapes=[pltpu.VMEM((GLA_HEADS, GLA_DV, LANES), F32)],
        compiler_params=_cparams(("parallel", "arbitrary")),
        name="gla",
    )(u, u, u, u, w_dec, b_dec, g_norm, s0)


def _ret_body(rq_ref, rk_ref, rv_ref, rg_ref, gn_ref, r0_ref, o_ref, rout_ref, rt_ref, *, c_real):
    ci = pl.program_id(1)
    cp = rq_ref.shape[0]

    @pl.when(ci == 0)
    def _():
        for h in range(RET_HEADS):
            rt_ref[h] = r0_ref[h].T

    row = lax.broadcasted_iota(jnp.int32, (cp, 1), 0)
    live = row < c_real
    rowf = row.astype(F32)
    rel = rowf - lax.broadcasted_iota(jnp.int32, (1, cp), 1).astype(F32)
    for h in range(RET_HEADS):
        sl = slice(h * LANES, (h + 1) * LANES)
        lg = RET_LOG_GAMMA[h]
        q = rq_ref[:, sl]
        k = jnp.where(live, rk_ref[:, sl] * RET_DK ** -0.5, 0.0)
        v = jnp.where(live, rv_ref[:, sl], 0.0)
        decay = jnp.where(rel >= 0, jnp.exp(jnp.maximum(rel, 0.0) * lg), 0.0)
        attn = _dot_nt(q.astype(BF16), k.astype(BF16)) * decay
        o = _dot(attn.astype(BF16), v.astype(BF16))
        rt = rt_ref[h]
        o += _dot_nt((q * jnp.exp((rowf + 1.0) * lg)).astype(BF16), rt.astype(BF16))
        kd = k * jnp.exp((c_real - 1.0 - rowf) * lg)
        rt_ref[h] = math.exp(c_real * lg) * rt + _dot(v.T.astype(BF16), kd.astype(BF16))
        mu = jnp.mean(o, axis=-1, keepdims=True)
        var = jnp.mean(jnp.square(o - mu), axis=-1, keepdims=True)
        o_ref[:, sl] = (o - mu) * lax.rsqrt(var + EPS) * gn_ref[:, sl] * jax.nn.silu(rg_ref[:, sl])

    @pl.when(ci == pl.num_programs(1) - 1)
    def _():
        for h in range(RET_HEADS):
            rout_ref[h] = rt_ref[h].T


def _retention(u, r0, g_norm, layer, cp, c_real):
    b, t, _ = u.shape
    blk = lambda idx: pl.BlockSpec((None, cp, 4 * LANES), lambda bi, i: (bi, i, idx))
    return pl.pallas_call(
        functools.partial(_ret_body, c_real=c_real),
        out_shape=(jax.ShapeDtypeStruct((b, t, RET_HEADS * RET_DV), F32),
                   jax.ShapeDtypeStruct((b, RET_HEADS, RET_DK, RET_DV), F32)),
        grid=(b, t // cp),
        in_specs=[
            blk(BLK_RQ), blk(BLK_RK), blk(BLK_RV), blk(BLK_RG),
            pl.BlockSpec((None, 1, RET_HEADS * RET_DV), lambda bi, i: (layer, 0, 0)),
            pl.BlockSpec((None, RET_HEADS, RET_DK, RET_DV), lambda bi, i: (bi, 0, 0, 0)),
        ],
        out_specs=(pl.BlockSpec((None, cp, RET_HEADS * RET_DV), lambda bi, i: (bi, i, 0)),
                   pl.BlockSpec((None, RET_HEADS, RET_DK, RET_DV), lambda bi, i: (bi, 0, 0, 0))),
        scratch_shapes=[pltpu.VMEM((RET_HEADS, RET_DV, RET_DK), F32)],
        compiler_params=_cparams(("parallel", "arbitrary")),
        name="retention",
    )(u, u, u, u, g_norm, r0)


def _pad_heads(w, n_heads):
    lead = w.shape[:-1]
    w = w.reshape(lead + (n_heads, HEAD_DIM))
    w = jnp.pad(w, [(0, 0)] * len(lead) + [(0, 0), (0, LANES - HEAD_DIM)])
    return w.reshape(lead + (n_heads * LANES,))


def _prep_w_in(w_in):
    cuts = np.cumsum((0,) + IN_SIZES)
    nq, kvc, kvs, kvw, ng, gq, gk, gv, glr, gog, rq, rk, rv, rg = [
        w_in[..., cuts[i]:cuts[i + 1]] for i in range(len(IN_SIZES))]
    misc = jnp.concatenate([ng, glr], axis=-1)
    misc = jnp.pad(misc, [(0, 0), (0, 0), (0, LANES - misc.shape[-1])])
    w = jnp.concatenate([nq, kvc, kvs, kvw, gq, gk, gv, gog, rq, rk, rv, rg, misc], axis=-1)
    assert w.shape[-1] == (BLK_MISC + 1) * LANES == U_WIDTH
    return w.astype(BF16)


def _prep_cmp_weights(w_cmp1, w_cmp2):
    depth = w_cmp1.shape[0]
    half = CMP_STRIDE * HEAD_DIM
    w1 = w_cmp1.reshape(depth, 2, 2, CMP_STRIDE, HEAD_DIM, CMP_HIDDEN)
    w1 = w1.transpose(0, 3, 1, 4, 2, 5)
    z = jnp.zeros_like(w1[:, :, 0])
    k_rows = jnp.stack([w1[:, :, 0], z], axis=4)
    v_rows = jnp.stack([z, w1[:, :, 1]], axis=4)
    wblk = jnp.concatenate([k_rows, v_rows], axis=2)
    wblk = wblk.reshape(depth, CMP_STRIDE // 2, 4 * HEAD_DIM, 4 * CMP_HIDDEN).astype(BF16)
    del half
    z2 = jnp.zeros_like(w_cmp2[:, 0])
    w2blk = jnp.concatenate([jnp.concatenate([w_cmp2[:, 0], z2], axis=-1),
                             jnp.concatenate([z2, w_cmp2[:, 1]], axis=-1)], axis=1).astype(BF16)
    return wblk, w2blk


def _overlap_matrix(n_rows, n_slc, n_cols):
    ratio = SEL_BLOCK // CMP_STRIDE
    w = np.zeros((n_rows, n_cols), np.float32)
    for j in range(n_slc):
        for k, wk in enumerate(OVERLAP_W):
            if ratio * j + k < n_rows:
                w[ratio * j + k, j] = wk
    return jnp.asarray(w)


def kernel(x_prompt, x_sample, cache_cmp_kv, cache_slc_kv, cache_win_kv, state_gla, state_ret, page_table,
           p_prompt, p_sample, norm_pre, norm_post, w_ffn_gate, w_ffn_up, w_ffn_down, w_in, w_out,
           w_cmp1, w_cmp2, cmp_pos, nsa_norm, w_gla_decay, b_gla_decay, gla_norm, ret_norm, w_ple, w_ple_gate):
    depth = w_in.shape[0]
    bp, seq, d_model = x_prompt.shape
    bs, t_real, _ = x_sample.shape
    n_pool = cache_cmp_kv.shape[0]
    n_pages = page_table.shape[1]
    past_len = n_pages * PAGE_SIZE
    tp = -(-t_real // SUBLANES) * SUBLANES
    kvw = NSA_KV_HEADS * 2 * HEAD_DIM

    wg = w_ffn_gate.astype(BF16)
    wu = w_ffn_up.astype(BF16)
    wd = w_ffn_down.astype(BF16)
    w_in_r = _prep_w_in(w_in)
    w_out_b = w_out.astype(BF16)
    w_ple_b = w_ple.astype(BF16)
    w_gate_b = w_ple_gate.astype(BF16)
    wblk, w2blk = _prep_cmp_weights(w_cmp1, w_cmp2)
    g_pre = norm_pre.reshape(depth, 3, 1, d_model)
    g_post = norm_post.reshape(depth, 3, 1, d_model)
    nsa_g = nsa_norm.reshape(depth, 1, -1)
    gla_g = gla_norm.reshape(depth, 1, -1)
    ret_g = ret_norm.reshape(depth, 1, -1)
    w_dec = jnp.pad(_pad_heads(w_gla_decay, GLA_HEADS),
                    [(0, 0), (MISC_GLR, LANES - MISC_GLR - GLA_GATE_RANK), (0, 0)])
    b_dec = _pad_heads(b_gla_decay, GLA_HEADS).reshape(depth, 1, -1)
    pos_flat = cmp_pos.reshape(depth, 2, 1, CMP_BLOCK * HEAD_DIM)

    cmp_pages = cache_cmp_kv.reshape(n_pool, depth, PAGE_SIZE * NSA_KV_HEADS, LANES)
    slc_rows = cache_slc_kv.reshape(n_pool, depth, PAGE_SIZE, kvw)
    win_rows = cache_win_kv.reshape(bs, depth, -1, kvw)
    wb = win_rows.shape[2]

    n_cr_p = seq // CMP_STRIDE
    wov_p = _overlap_matrix(n_cr_p, -(-seq // SEL_BLOCK), LANES).T
    n_cr_s = past_len // CMP_STRIDE
    n_slc_s = -(-(past_len + t_real) // SEL_BLOCK)
    wov_s = _overlap_matrix(n_cr_s, n_slc_s, -(-n_slc_s // LANES) * LANES)
    table_p = jnp.arange(bp * (seq // PAGE_SIZE), dtype=jnp.int32).reshape(bp, seq // PAGE_SIZE)

    xp = x_prompt.reshape(bp * seq, d_model)
    xs = jnp.pad(x_sample, ((0, 0), (0, tp - t_real), (0, 0))).reshape(bs * tp, d_model)
    pp = p_prompt.reshape(depth, bp * seq, -1)
    ps = jnp.pad(p_sample, ((0, 0), (0, 0), (0, tp - t_real), (0, 0))).reshape(depth, bs * tp, -1)
    s0_gla_p = jnp.zeros((bp, GLA_HEADS, LANES, GLA_DV), F32)
    s0_ret_p = jnp.zeros((bp, RET_HEADS, RET_DK, RET_DV), F32)
    s0_gla_s = jnp.pad(state_gla, ((0, 0), (0, 0), (0, 0), (0, LANES - GLA_DK), (0, 0)))

    tm_p = 1024
    tm_s = bs * tp
    st_p = [[] for _ in range(5)]
    st_s = [[] for _ in range(5)]
    for l in range(depth):
        bias = _cmp_bias(pos_flat, w_cmp1, l).reshape(1, 2 * CMP_HIDDEN)

        xp = _ffn(xp, g_pre, g_post, wg, wu, wd, l, 0, 0, tm_p)
        u2 = _proj_in(xp, g_pre, w_in_r, l, tm_p)
        u = u2.reshape(bp, seq, U_WIDTH)
        kv_c = u[..., BLK_KVC * kvw:(BLK_KVC + 1) * kvw]
        kv_s = u[..., BLK_KVS * kvw:(BLK_KVS + 1) * kvw]
        kv_w = u[..., BLK_KVW * kvw:(BLK_KVW + 1) * kvw]
        kc = _compress(kv_c.reshape(bp * (seq // PAGE_SIZE), 1, PAGE_SIZE * NSA_KV_HEADS, LANES), table_p, 0,
                       wblk[l], bias, w2blk[l])
        o_nsa = _nsa_prompt(u, kc, wov_p)
        gla_out, s_g = _gla(u, s0_gla_p, w_dec, b_dec, gla_g, l, GLA_CHUNK, GLA_CHUNK)
        ret_out, s_r = _retention(u, s0_ret_p, ret_g, l, RET_CHUNK, RET_CHUNK)
        xp = _proj_out(xp, o_nsa.reshape(bp * seq, -1), gla_out.reshape(bp * seq, -1),
                       ret_out.reshape(bp * seq, -1), nsa_g, w_out_b, g_post, l, tm_p // 2)
        xp = _ffn(xp, g_pre, g_post, wg, wu, wd, l, 1, 2, tm_p)
        xp = _ple(xp, pp, w_gate_b, w_ple_b, l, tm_p // 2)
        win_keep = min(WINDOW, seq)
        for j, a in enumerate((kv_c, kv_s, kv_w[:, seq - win_keep:], s_g, s_r)):
            st_p[j].append(a)

        xs = _ffn(xs, g_pre, g_post, wg, wu, wd, l, 0, 0, tm_s)
        us = _proj_in(xs, g_pre, w_in_r, l, tm_s).reshape(bs, tp, U_WIDTH)
        kv_c = us[:, :t_real, BLK_KVC * kvw:(BLK_KVC + 1) * kvw]
        kv_s = us[:, :t_real, BLK_KVS * kvw:(BLK_KVS + 1) * kvw]
        kv_w = us[:, :t_real, BLK_KVW * kvw:(BLK_KVW + 1) * kvw]
        kc = _compress(cmp_pages, page_table, l, wblk[l], bias, w2blk[l])
        o_c, sel = _cmp_select(us, kc, wov_s, past_len, t_real)
        o_s = _slc_sample(us, slc_rows, page_table, sel[:, :, :t_real, :N_SEL], l, past_len, t_real)
        o_nsa = _win_combine(us, win_rows, o_c, o_s, l, past_len)
        gla_out, s_g = _gla(us, s0_gla_s[:, l], w_dec, b_dec, gla_g, l, tp, t_real)
        ret_out, s_r = _retention(us, state_ret[:, l], ret_g, l, tp, t_real)
        xs = _proj_out(xs, o_nsa.reshape(bs * tp, -1), gla_out.reshape(bs * tp, -1),
                       ret_out.reshape(bs * tp, -1), nsa_g, w_out_b, g_post, l, tm_s)
        xs = _ffn(xs, g_pre, g_post, wg, wu, wd, l, 1, 2, tm_s)
        xs = _ple(xs, ps, w_gate_b, w_ple_b, l, tm_s)
        new_win = jnp.concatenate([win_rows[:, l], kv_w], axis=1)[:, t_real:]
        for j, a in enumerate((kv_c, kv_s, new_win, s_g, s_r)):
            st_s[j].append(a)

    def kv_stack(parts):
        a = jnp.stack(parts, axis=1)
        return a.reshape(a.shape[:3] + (NSA_KV_HEADS, 2, HEAD_DIM))

    y_p = xp.reshape(bp, seq, d_model)
    y_s = xs.reshape(bs, tp, d_model)[:, :t_real]
    return (y_p, y_s,
            kv_stack(st_p[0]), kv_stack(st_p[1]), kv_stack(st_p[2]),
            jnp.stack(st_p[3], axis=1), jnp.stack(st_p[4], axis=1),
            kv_stack(st_s[0]), kv_stack(st_s[1]), kv_stack(st_s[2]),
            jnp.stack(st_s[3], axis=1), jnp.stack(st_s[4], axis=1))
```

```python
import functools
import math

import numpy as np
import jax
import jax.numpy as jnp
from jax import lax
from jax.experimental import pallas as pl
from jax.experimental.pallas import tpu as pltpu

F32 = jnp.float32
BF16 = jnp.bfloat16
HIGHEST = lax.Precision.HIGHEST

HEAD_DIM = 64
NSA_HEADS = 16
NSA_KV_HEADS = 4
NSA_GROUP = 4
CMP_BLOCK = 32
CMP_STRIDE = 16
CMP_HIDDEN = 128
SEL_BLOCK = 64
N_SEL = 16
WINDOW = 512
Q_BLOCK = 128
OVERLAP_W = (0.5, 1.0, 1.0, 1.0, 0.5)
GLA_HEADS = 4
GLA_DK = 64
GLA_DV = 128
GLA_GATE_RANK = 16
GLA_GATE_TAU = 16.0
GLA_CHUNK = 64
RET_HEADS = 4
RET_DK = 128
RET_DV = 128
RET_CHUNK = 64
PAGE_SIZE = 128
NEG_INF = -1e30
FORCE = 1e9
EPS = 1e-6
IN_SIZES = (1024, 512, 512, 512, 48, 256, 256, 512, 16, 512, 512, 512, 512, 512)

LANES = 128
SUBLANES = 8
VMEM_LIMIT = 56 * 1024 * 1024

U_WIDTH = 6272
BLK_KVC, BLK_KVS, BLK_KVW = 2, 3, 4
BLK_GQK, BLK_GV, BLK_GOG = 5, 6, 7
BLK_RQ, BLK_RK, BLK_RV, BLK_RG = 8, 9, 10, 11
BLK_MISC = 48
MISC_GLR = 48
Q_HEAD_COLS = NSA_GROUP * HEAD_DIM

ALIBI = [[2.0 ** (-8.0 * (h * NSA_GROUP + g + 1) / NSA_HEADS) for g in range(NSA_GROUP)]
         for h in range(NSA_KV_HEADS)]
RET_LOG_GAMMA = [math.log1p(-(2.0 ** (-5.0 - h))) for h in range(RET_HEADS)]


def _cparams(sem):
    return pltpu.CompilerParams(dimension_semantics=sem, vmem_limit_bytes=VMEM_LIMIT)


def _rms(x, g=None):
    y = x * lax.rsqrt(jnp.mean(x * x, axis=-1, keepdims=True) + EPS)
    return y if g is None else y * g


def _dot(a, b):
    return jnp.dot(a, b, preferred_element_type=F32)


def _dot_nt(a, b):
    return lax.dot_general(a, b, (((1,), (1,)), ((), ())), preferred_element_type=F32)


def _ffn_body(x_ref, gpre_ref, wg_ref, wu_ref, wd_ref, gpost_ref, o_ref, xn_ref, acc_ref):
    f = pl.program_id(1)

    @pl.when(f == 0)
    def _():
        xn_ref[...] = _rms(x_ref[...], gpre_ref[...]).astype(BF16)
        acc_ref[...] = jnp.zeros_like(acc_ref)

    xn = xn_ref[...]
    h = jax.nn.silu(_dot(xn, wg_ref[...])) * _dot(xn, wu_ref[...])
    acc_ref[...] += _dot(h.astype(BF16), wd_ref[...])

    @pl.when(f == pl.num_programs(1) - 1)
    def _():
        o_ref[...] = x_ref[...] + 0.5 * _rms(acc_ref[...], gpost_ref[...])


def _ffn(x, g_pre, g_post, wg, wu, wd, layer, which, norm_idx, tm, tf=512):
    m, d = x.shape
    ff = wg.shape[-1]
    assert m % tm == 0 and ff % tf == 0
    return pl.pallas_call(
        _ffn_body,
        out_shape=jax.ShapeDtypeStruct((m, d), F32),
        grid=(m // tm, ff // tf),
        in_specs=[
            pl.BlockSpec((tm, d), lambda i, f: (i, 0)),
            pl.BlockSpec((None, None, 1, d), lambda i, f: (layer, norm_idx, 0, 0)),
            pl.BlockSpec((None, None, d, tf), lambda i, f: (layer, which, 0, f)),
            pl.BlockSpec((None, None, d, tf), lambda i, f: (layer, which, 0, f)),
            pl.BlockSpec((None, None, tf, d), lambda i, f: (layer, which, f, 0)),
            pl.BlockSpec((None, None, 1, d), lambda i, f: (layer, norm_idx, 0, 0)),
        ],
        out_specs=pl.BlockSpec((tm, d), lambda i, f: (i, 0)),
        scratch_shapes=[pltpu.VMEM((tm, d), BF16), pltpu.VMEM((tm, d), F32)],
        compiler_params=_cparams(("parallel", "arbitrary")),
        name="ffn",
    )(x, g_pre, wg, wu, wd, g_post)


def _proj_in_body(x_ref, g_ref, w_ref, o_ref, xn_ref):
    @pl.when(pl.program_id(1) == 0)
    def _():
        xn_ref[...] = _rms(x_ref[...], g_ref[...]).astype(BF16)

    o_ref[...] = _dot(xn_ref[...], w_ref[...])


def _proj_in(x, g_pre, w_in, layer, tm, tn=896):
    m, d = x.shape
    n = w_in.shape[-1]
    return pl.pallas_call(
        _proj_in_body,
        out_shape=jax.ShapeDtypeStruct((m, n), F32),
        grid=(m // tm, n // tn),
        in_specs=[
            pl.BlockSpec((tm, d), lambda i, j: (i, 0)),
            pl.BlockSpec((None, None, 1, d), lambda i, j: (layer, 1, 0, 0)),
            pl.BlockSpec((None, d, tn), lambda i, j: (layer, 0, j)),
        ],
        out_specs=pl.BlockSpec((tm, tn), lambda i, j: (i, j)),
        scratch_shapes=[pltpu.VMEM((tm, d), BF16)],
        compiler_params=_cparams(("parallel", "arbitrary")),
        name="proj_in",
    )(x, g_pre, w_in)


def _proj_out_body(x_ref, nsa_ref, gla_ref, ret_ref, gn_ref, w_ref, gpost_ref, o_ref):
    nsa_w = nsa_ref.shape[-1]
    gla_w = gla_ref.shape[-1]
    nsa = _rms(nsa_ref[...], gn_ref[...]).astype(BF16)
    y = _dot(nsa, w_ref[0:nsa_w, :])
    y += _dot(gla_ref[...].astype(BF16), w_ref[nsa_w:nsa_w + gla_w, :])
    y += _dot(ret_ref[...].astype(BF16), w_ref[nsa_w + gla_w:, :])
    o_ref[...] = x_ref[...] + _rms(y, gpost_ref[...])


def _proj_out(x, o_nsa, gla_out, ret_out, nsa_norm, w_out, g_post, layer, tm):
    m, d = x.shape
    row = lambda a: pl.BlockSpec((tm, a.shape[-1]), lambda i: (i, 0))
    return pl.pallas_call(
        _proj_out_body,
        out_shape=jax.ShapeDtypeStruct((m, d), F32),
        grid=(m // tm,),
        in_specs=[
            row(x), row(o_nsa), row(gla_out), row(ret_out),
            pl.BlockSpec((None, 1, o_nsa.shape[-1]), lambda i: (layer, 0, 0)),
            pl.BlockSpec((None, w_out.shape[1], d), lambda i: (layer, 0, 0)),
            pl.BlockSpec((None, None, 1, d), lambda i: (layer, 1, 0, 0)),
        ],
        out_specs=row(x),
        compiler_params=_cparams(("parallel",)),
        name="proj_out",
    )(x, o_nsa, gla_out, ret_out, nsa_norm, w_out, g_post)


def _ple_body(x_ref, p_ref, wg_ref, wp_ref, o_ref):
    x = x_ref[...]
    gate = jax.nn.sigmoid(_dot(_rms(x).astype(BF16), wg_ref[...]))
    o_ref[...] = x + gate * _dot(p_ref[...].astype(BF16), wp_ref[...])


def _ple(x, p, w_gate, w_ple, layer, tm):
    m, d = x.shape
    return pl.pallas_call(
        _ple_body,
        out_shape=jax.ShapeDtypeStruct((m, d), F32),
        grid=(m // tm,),
        in_specs=[
            pl.BlockSpec((tm, d), lambda i: (i, 0)),
            pl.BlockSpec((None, tm, p.shape[-1]), lambda i: (layer, i, 0)),
            pl.BlockSpec((None, d, d), lambda i: (layer, 0, 0)),
            pl.BlockSpec((None, p.shape[-1], d), lambda i: (layer, 0, 0)),
        ],
        out_specs=pl.BlockSpec((tm, d), lambda i: (i, 0)),
        compiler_params=_cparams(("parallel",)),
        name="ple",
    )(x, p, w_gate, w_ple)


PAGES_PER_STEP = 16
CHUNKS_PER_PAGE = PAGE_SIZE // CMP_STRIDE
KV_ROW = NSA_KV_HEADS * 2 * HEAD_DIM


def _cmp_bias_body(pos_ref, w1_ref, o_ref):
    o_ref[...] = jnp.dot(pos_ref[...], w1_ref[...], preferred_element_type=F32, precision=HIGHEST)


def _cmp_bias(cmp_pos, w_cmp1, layer):
    kdim = w_cmp1.shape[2]
    return pl.pallas_call(
        _cmp_bias_body,
        out_shape=jax.ShapeDtypeStruct((2, 1, CMP_HIDDEN), F32),
        grid=(2,),
        in_specs=[pl.BlockSpec((None, None, 1, kdim), lambda c: (layer, c, 0, 0)),
                  pl.BlockSpec((None, None, kdim, CMP_HIDDEN), lambda c: (layer, c, 0, 0))],
        out_specs=pl.BlockSpec((None, 1, CMP_HIDDEN), lambda c: (c, 0, 0)),
        compiler_params=_cparams(("arbitrary",)),
        name="cmp_bias",
    )(cmp_pos, w_cmp1)


def _compress_body(pt_ref, *refs, feature_major):
    del pt_ref
    page_refs = refs[:PAGES_PER_STEP]
    wblk_ref, bias_ref, w2_ref, o_ref, prev_ref = refs[PAGES_PER_STEP:PAGES_PER_STEP + 5]
    rows = PAGES_PER_STEP * CHUNKS_PER_PAGE
    half = 2 * CMP_HIDDEN

    @pl.when(pl.program_id(1) == 0)
    def _():
        prev_ref[...] = jnp.zeros_like(prev_ref)

    first_row = lax.broadcasted_iota(jnp.int32, (rows, half), 0) == 0

    if feature_major:
        tok_ref = refs[PAGES_PER_STEP + 5]
        for k, pr in enumerate(page_refs):
            for h in range(NSA_KV_HEADS):
                tok_ref[k * NSA_KV_HEADS + h] = pr[h].reshape(LANES, PAGE_SIZE).T

        def token_rows(t, h):
            return jnp.concatenate([tok_ref[k * NSA_KV_HEADS + h, pl.ds(t, CHUNKS_PER_PAGE, stride=CMP_STRIDE), :]
                                    for k in range(PAGES_PER_STEP)], axis=0)
    else:
        def token_rows(t, h):
            return jnp.concatenate([pr[h, pl.ds(t, CHUNKS_PER_PAGE, stride=CMP_STRIDE), :]
                                    for pr in page_refs], axis=0)

    for h in range(NSA_KV_HEADS):
        acc = jnp.zeros((rows, 2 * half), F32)
        for tt in range(CMP_STRIDE // 2):
            lhs = jnp.concatenate([token_rows(2 * tt, h), token_rows(2 * tt + 1, h)], axis=1).astype(BF16)
            acc += _dot(lhs, wblk_ref[tt])
        first = acc[:, :half]
        second = acc[:, half:]
        carry = prev_ref[h][SUBLANES - 1:SUBLANES, :]
        shifted = jnp.where(first_row, carry, pltpu.roll(first, 1, 0))
        prev_ref[h] = first[rows - SUBLANES:, :]
        hidden = jax.nn.gelu(shifted + second + bias_ref[...])
        o_ref[:, h * LANES:(h + 1) * LANES] = _dot(hidden.astype(BF16), w2_ref[...])


def _compress(pages, table, layer, wblk, bias, w2blk):
    b, n_pages = table.shape
    assert n_pages % PAGES_PER_STEP == 0
    rows = PAGES_PER_STEP * CHUNKS_PER_PAGE
    feature_major = pages.ndim == 6
    page_block = (None, None) + pages.shape[2:]
    zeros = (0,) * (pages.ndim - 2)

    def page_spec(k):
        return pl.BlockSpec(page_block, lambda bi, i, pt: (pt[bi, i * PAGES_PER_STEP + k], layer) + zeros)

    scratch = [pltpu.VMEM((NSA_KV_HEADS, SUBLANES, 2 * CMP_HIDDEN), F32)]
    if feature_major:
        scratch.append(pltpu.VMEM((PAGES_PER_STEP * NSA_KV_HEADS, PAGE_SIZE, LANES), F32))

    grid_spec = pltpu.PrefetchScalarGridSpec(
        num_scalar_prefetch=1,
        grid=(b, n_pages // PAGES_PER_STEP),
        in_specs=[page_spec(k) for k in range(PAGES_PER_STEP)] + [
            pl.BlockSpec(wblk.shape, lambda bi, i, pt: (0, 0, 0)),
            pl.BlockSpec(bias.shape, lambda bi, i, pt: (0, 0)),
            pl.BlockSpec(w2blk.shape, lambda bi, i, pt: (0, 0)),
        ],
        out_specs=pl.BlockSpec((None, rows, NSA_KV_HEADS * LANES), lambda bi, i, pt: (bi, i, 0)),
        scratch_shapes=scratch,
    )
    return pl.pallas_call(
        functools.partial(_compress_body, feature_major=feature_major),
        out_shape=jax.ShapeDtypeStruct((b, n_pages * CHUNKS_PER_PAGE, NSA_KV_HEADS * LANES), F32),
        grid_spec=grid_spec,
        compiler_params=_cparams(("arbitrary", "arbitrary")),
        name="compress",
    )(table, *([pages] * PAGES_PER_STEP), wblk, bias, w2blk)


def _softmax_groups(s, distf, mask, slopes, rows):
    parts = []
    for g in range(NSA_GROUP):
        sg = s[g * rows:(g + 1) * rows] - slopes[g] * distf
        sg = jnp.where(mask, sg, NEG_INF)
        m = jnp.max(sg, axis=-1, keepdims=True)
        e = jnp.where(mask, jnp.exp(sg - m), 0.0)
        parts.append(e / jnp.maximum(jnp.sum(e, axis=-1, keepdims=True), 1e-30))
    return parts


def _unpack_head(pair, odd):
    lane = lax.broadcasted_iota(jnp.int32, pair.shape, 1)
    return jnp.where(lane < HEAD_DIM, pltpu.roll(pair, HEAD_DIM, 1) if odd else pair, 0.0)


def _stack_q(q_ref, base, scale):
    parts = []
    for g in range(NSA_GROUP):
        c0 = base + (g // 2) * LANES
        parts.append(_unpack_head(q_ref[:, c0:c0 + LANES], g % 2 == 1))
    return (jnp.concatenate(parts, axis=0) * scale).astype(BF16)


def _block_scores(imp, wov_ref, t_pos, n_slc):
    score = jnp.dot(imp, wov_ref[...], preferred_element_type=F32, precision=HIGHEST)
    blk = lax.broadcasted_iota(jnp.int32, score.shape, 1)
    cur = t_pos // SEL_BLOCK
    valid = blk <= cur
    forced = valid & ((blk == 0) | (blk == cur) | (blk == cur - 1))
    score = jnp.where(forced, FORCE, jnp.where(valid, score, -FORCE))
    return score, blk


def _pack_heads(o_ref, heads):
    lane = lax.broadcasted_iota(jnp.int32, heads[0].shape, 1)
    for p in range(NSA_HEADS // 2):
        even = pltpu.roll(heads[2 * p], HEAD_DIM, 1)
        o_ref[:, p * LANES:(p + 1) * LANES] = jnp.where(lane < HEAD_DIM, even, heads[2 * p + 1])


SLC_KEY_CHUNK = 512
LOG2E = 1.4426950408889634
MASK_BIG = 2.0 ** 100
ALIBI2 = [[s * LOG2E for s in row] for row in ALIBI]


def _exp2_softmax(s2, rel_row, bias, slopes2, rows):
    out = []
    for g in range(NSA_GROUP):
        lg = s2[g * rows:(g + 1) * rows] + (slopes2[g] * rel_row + bias)
        e = jnp.exp2(lg - jnp.max(lg, axis=-1, keepdims=True))
        out.append((e, jnp.sum(e, axis=-1, keepdims=True)))
    return out


def _nsa_prompt_body(q_ref, misc_ref, kc_ref, kvs_ref, kvw_ref, wovt_ref, o_ref, *, seq):
    qb = pl.program_id(1)
    t0 = qb * Q_BLOCK
    t0f = t0.astype(F32)
    n_slc = -(-seq // SEL_BLOCK)
    n_sel = min(N_SEL, n_slc)
    n_cr = kc_ref.shape[0]
    t_col = (t0 + lax.broadcasted_iota(jnp.int32, (Q_BLOCK, 1), 0)).astype(F32)
    gates = jax.nn.sigmoid(misc_ref[...])

    r_idx = lax.broadcasted_iota(jnp.int32, (1, n_cr), 1)
    end_c = (r_idx * CMP_STRIDE + (CMP_STRIDE - 1)).astype(F32)
    bias_c = jnp.minimum(t_col - end_c, 0.0) * MASK_BIG + jnp.where(r_idx >= 1, 0.0, -MASK_BIG)
    rel_c = end_c - t0f
    row_live = jnp.where(t_col >= CMP_BLOCK - 1, 1.0, 0.0)
    band = WINDOW + Q_BLOCK
    w0 = pl.multiple_of(jnp.maximum(t0 - WINDOW, 0), Q_BLOCK)
    pos_w = (w0 + lax.broadcasted_iota(jnp.int32, (1, band), 1)).astype(F32)
    d_w = t_col - pos_w
    bias_w = (jnp.minimum(d_w, 0.0) + jnp.minimum((WINDOW - 1.0) - d_w, 0.0)) * MASK_BIG
    rel_w = pos_w - t0f

    blk_t = lax.broadcasted_iota(jnp.int32, (n_slc, Q_BLOCK), 0)
    cur_t = (t0 + lax.broadcasted_iota(jnp.int32, (1, Q_BLOCK), 1)) // SEL_BLOCK
    valid_t = blk_t <= cur_t
    forced_t = valid_t & ((blk_t == 0) | (blk_t == cur_t) | (blk_t == cur_t - 1))
    tm1 = t_col - 1.0

    kv_heads = range(NSA_KV_HEADS)
    lane_sl = [slice(h * LANES, (h + 1) * LANES) for h in kv_heads]
    qhs = [_stack_q(q_ref, h * Q_HEAD_COLS, HEAD_DIM ** -0.5 * LOG2E) for h in kv_heads]

    o_cs, scores = [], []
    for h in kv_heads:
        kc = kc_ref[:, lane_sl[h]].astype(BF16)
        sm_c = _exp2_softmax(_dot_nt(qhs[h], kc), rel_c, bias_c, ALIBI2[h], Q_BLOCK)
        p_c = [e * (row_live / l) for (e, l) in sm_c]
        o_cs.append(_dot(jnp.concatenate(p_c, axis=0).astype(BF16), kc))
        imp = p_c[0] + p_c[1] + p_c[2] + p_c[3]
        score_t = lax.dot_general(wovt_ref[...], imp, (((1,), (1,)), ((), ())),
                                  preferred_element_type=F32, precision=HIGHEST)[:n_slc]
        scores.append(jnp.where(forced_t, FORCE, jnp.where(valid_t, score_t, -FORCE)))

    def ranked():
        out = []
        for score_t in scores:
            rank = jnp.zeros(score_t.shape, F32)
            for i in range(n_slc):
                ci = score_t[i:i + 1, :]
                tie = jnp.where(blk_t > i, 1.0, 0.0)
                rank += jnp.where(ci > score_t, 1.0, jnp.where(ci == score_t, tie, 0.0))
            out.append(jnp.where(rank < n_sel, MASK_BIG, 0.0))
        return tuple(out)

    sel_ts = lax.cond(t0 + Q_BLOCK > n_sel * SEL_BLOCK, ranked,
                      lambda: tuple(jnp.where(valid_t, MASK_BIG, 0.0) for _ in kv_heads))
    sels = [jnp.concatenate([s, jnp.zeros((LANES - n_slc, Q_BLOCK), F32)], axis=0).T.astype(BF16)
            for s in sel_ts]

    def chunk(c, carry):
        k0 = pl.multiple_of(c * SLC_KEY_CHUNK, SLC_KEY_CHUNK)
        pos = k0 + lax.broadcasted_iota(jnp.int32, (1, SLC_KEY_CHUNK), 1)
        posf = pos.astype(F32)
        expand = jnp.where(
            (pos // SEL_BLOCK) == lax.broadcasted_iota(jnp.int32, (LANES, SLC_KEY_CHUNK), 0),
            1.0, 0.0).astype(BF16)
        causal = jnp.minimum(tm1 - posf, -1.0) * MASK_BIG
        rel = posf - t0f
        new = []
        for h in kv_heads:
            kv = kvs_ref[pl.ds(k0, SLC_KEY_CHUNK), lane_sl[h]].astype(BF16)
            s2 = _dot_nt(qhs[h], kv)
            bias = _dot(sels[h], expand) + causal
            out = []
            for g in range(NSA_GROUP):
                m_old, l_old, a_old = carry[h][g]
                lg = s2[g * Q_BLOCK:(g + 1) * Q_BLOCK] + (ALIBI2[h][g] * rel + bias)
                m_new = jnp.maximum(m_old, jnp.max(lg, axis=-1, keepdims=True))
                alpha = jnp.exp2(m_old - m_new)
                e = jnp.exp2(lg - m_new)
                l_new = alpha * l_old + jnp.sum(e, axis=-1, keepdims=True)
                a_new = alpha * a_old + _dot(e.astype(BF16), kv)
                out.append((m_new, l_new, a_new))
            new.append(tuple(out))
        return tuple(new)

    init = tuple(tuple((jnp.full((Q_BLOCK, 1), NEG_INF, F32), jnp.zeros((Q_BLOCK, 1), F32),
                        jnp.zeros((Q_BLOCK, LANES), F32)) for _ in range(NSA_GROUP)) for _ in kv_heads)
    n_chunks = (t0 + Q_BLOCK + SLC_KEY_CHUNK - 1) // SLC_KEY_CHUNK
    fin = lax.fori_loop(0, n_chunks, chunk, init)

    heads = []
    for h in kv_heads:
        kvw = kvw_ref[pl.ds(w0, band), lane_sl[h]].astype(BF16)
        sm_w = _exp2_softmax(_dot_nt(qhs[h], kvw), rel_w, bias_w, ALIBI2[h], Q_BLOCK)
        o_w = _dot(jnp.concatenate([e for (e, _) in sm_w], axis=0).astype(BF16), kvw)
        for g in range(NSA_GROUP):
            col = h * NSA_GROUP + g
            rows = slice(g * Q_BLOCK, (g + 1) * Q_BLOCK)
            heads.append(gates[:, col:col + 1] * o_cs[h][rows]
                         + (gates[:, NSA_HEADS + col:NSA_HEADS + col + 1] / fin[h][g][1]) * fin[h][g][2]
                         + (gates[:, 2 * NSA_HEADS + col:2 * NSA_HEADS + col + 1] / sm_w[g][1]) * o_w[rows])
    _pack_heads(o_ref, heads)


def _nsa_prompt(u, kc, wov):
    b, t, _ = u.shape
    assert t % Q_BLOCK == 0 and t >= WINDOW + Q_BLOCK and t % SLC_KEY_CHUNK == 0
    n_slc = -(-t // SEL_BLOCK)
    assert n_slc % SUBLANES == 0 and n_slc <= LANES
    return pl.pallas_call(
        functools.partial(_nsa_prompt_body, seq=t),
        out_shape=jax.ShapeDtypeStruct((b, t, NSA_HEADS * HEAD_DIM), F32),
        grid=(b, t // Q_BLOCK),
        in_specs=[
            pl.BlockSpec((None, Q_BLOCK, NSA_HEADS * HEAD_DIM), lambda bi, i: (bi, i, 0)),
            pl.BlockSpec((None, Q_BLOCK, LANES), lambda bi, i: (bi, i, BLK_MISC)),
            pl.BlockSpec((None,) + kc.shape[1:], lambda bi, i: (bi, 0, 0)),
            pl.BlockSpec((None, t, 4 * LANES), lambda bi, i: (bi, 0, BLK_KVS)),
            pl.BlockSpec((None, t, 4 * LANES), lambda bi, i: (bi, 0, BLK_KVW)),
            pl.BlockSpec(wov.shape, lambda bi, i: (0, 0)),
        ],
        out_specs=pl.BlockSpec((None, Q_BLOCK, NSA_HEADS * HEAD_DIM), lambda bi, i: (bi, i, 0)),
        compiler_params=_cparams(("parallel", "arbitrary")),
        name="nsa_prompt",
    )(u, u, kc, u, u, wov)


def _cmp_select_body(q_ref, kc_ref, wov_ref, oc_ref, sel_ref, *, past_len, t_real):
    tp = q_ref.shape[0]
    n_cr = kc_ref.shape[0]
    n_slc = -(-(past_len + t_real) // SEL_BLOCK)
    t_pos = past_len + lax.broadcasted_iota(jnp.int32, (tp, 1), 0)
    for h in range(NSA_KV_HEADS):
        qh = _stack_q(q_ref, h * Q_HEAD_COLS, HEAD_DIM ** -0.5)
        kc = kc_ref[:, h * LANES:(h + 1) * LANES].astype(BF16)
        r_idx = lax.broadcasted_iota(jnp.int32, (1, n_cr), 1)
        dist = t_pos - (r_idx * CMP_STRIDE + (CMP_STRIDE - 1))
        mask = (r_idx >= 1) & (dist >= 0)
        p_c = _softmax_groups(_dot_nt(qh, kc), dist.astype(F32), mask, ALIBI[h], tp)
        oc_ref[h] = _dot(jnp.concatenate(p_c, axis=0).astype(BF16), kc)
        imp = p_c[0] + p_c[1] + p_c[2] + p_c[3]
        score, blk = _block_scores(imp, wov_ref, t_pos, n_slc)
        work = jnp.where(blk < n_slc, score, -3e38)
        blkf = blk.astype(F32)
        picked = jnp.zeros((tp, LANES), jnp.int32)
        lane = lax.broadcasted_iota(jnp.int32, (tp, LANES), 1)
        for k in range(N_SEL):
            m = jnp.max(work, axis=-1, keepdims=True)
            idx = jnp.min(jnp.where(work == m, blkf, 3e38), axis=-1, keepdims=True)
            picked = jnp.where(lane == k, idx.astype(jnp.int32), picked)
            work = jnp.where(blkf == idx, -3e38, work)
        sel_ref[h] = picked


def _cmp_select(u_s, kc, wov, past_len, t_real):
    b, tp, _ = u_s.shape
    assert -(-(past_len + t_real) // SEL_BLOCK) >= N_SEL
    return pl.pallas_call(
        functools.partial(_cmp_select_body, past_len=past_len, t_real=t_real),
        out_shape=(jax.ShapeDtypeStruct((b, NSA_KV_HEADS, NSA_GROUP * tp, LANES), F32),
                   jax.ShapeDtypeStruct((b, NSA_KV_HEADS, tp, LANES), jnp.int32)),
        grid=(b,),
        in_specs=[
            pl.BlockSpec((None, tp, NSA_HEADS * HEAD_DIM), lambda bi: (bi, 0, 0)),
            pl.BlockSpec((None,) + kc.shape[1:], lambda bi: (bi, 0, 0)),
            pl.BlockSpec(wov.shape, lambda bi: (0, 0)),
        ],
        out_specs=(pl.BlockSpec((None, NSA_KV_HEADS, NSA_GROUP * tp, LANES), lambda bi: (bi, 0, 0, 0)),
                   pl.BlockSpec((None, NSA_KV_HEADS, tp, LANES), lambda bi: (bi, 0, 0, 0))),
        compiler_params=_cparams(("parallel",)),
        name="cmp_select",
    )(u_s, kc, wov)


def _slc_sample_body(sel_ref, pt_ref, *refs, past_len, t_real):
    del pt_ref
    blk_refs = refs[:N_SEL]
    q_ref, kvn_ref, o_ref = refs[N_SEL:]
    bi, h, t = pl.program_id(0), pl.program_id(1), pl.program_id(2)
    tp = kvn_ref.shape[0]
    rows = NSA_GROUP * tp
    n_past_blk = past_len // SEL_BLOCK
    qh = _stack_q(q_ref, 0, HEAD_DIM ** -0.5)
    row = lax.broadcasted_iota(jnp.int32, (rows, 1), 0)
    t_pos = past_len + row % tp

    n_keys = N_SEL * PAGE_SIZE
    per_page = PAGE_SIZE // SEL_BLOCK
    lane = lax.broadcasted_iota(jnp.int32, (1, n_keys), 1)
    slot = lane // PAGE_SIZE
    in_page = lane % PAGE_SIZE
    pos = in_page
    ok = jnp.zeros((1, n_keys), jnp.int32)
    for k in range(N_SEL):
        blk = sel_ref[((bi * NSA_KV_HEADS + h) * t_real + t) * N_SEL + k]
        pos = pos + jnp.where(slot == k, (blk // per_page) * PAGE_SIZE, 0)
        ok = ok + jnp.where((slot == k) & (blk < n_past_blk) & (in_page // SEL_BLOCK == blk % per_page), 1, 0)
    kv_t = jnp.concatenate([r[...].reshape(2 * HEAD_DIM, PAGE_SIZE) for r in blk_refs], axis=1).astype(BF16)
    d = t_pos - pos
    msk = (ok > 0) & (d >= 0)
    kvn = jnp.concatenate([kvn_ref[...], jnp.zeros((LANES - tp, LANES), F32)], axis=0).astype(BF16)
    idx_n = lax.broadcasted_iota(jnp.int32, (1, LANES), 1)
    dn = t_pos - (past_len + idx_n)
    mskn = (idx_n < tp) & (dn >= 0)
    s_all = _dot(qh, kv_t)
    s_new = _dot_nt(qh, kvn)
    outs = []
    for g in range(NSA_GROUP):
        r0 = slice(g * tp, (g + 1) * tp)
        slope = jnp.where(h == 0, ALIBI[0][g], jnp.where(h == 1, ALIBI[1][g],
                                                         jnp.where(h == 2, ALIBI[2][g], ALIBI[3][g])))
        sg = jnp.where(msk[r0], s_all[r0] - slope * d[r0].astype(F32), NEG_INF)
        sn = jnp.where(mskn[r0], s_new[r0] - slope * dn[r0].astype(F32), NEG_INF)
        m = jnp.maximum(jnp.max(sg, axis=-1, keepdims=True), jnp.max(sn, axis=-1, keepdims=True))
        e = jnp.where(msk[r0], jnp.exp(sg - m), 0.0)
        en = jnp.where(mskn[r0], jnp.exp(sn - m), 0.0)
        l = jnp.sum(e, axis=-1, keepdims=True) + jnp.sum(en, axis=-1, keepdims=True)
        outs.append((_dot_nt(e.astype(BF16), kv_t) + _dot(en.astype(BF16), kvn)) / jnp.maximum(l, 1e-30))
    res = jnp.concatenate(outs, axis=0)

    @pl.when(t == 0)
    def _():
        o_ref[...] = jnp.zeros_like(o_ref)

    o_ref[...] = jnp.where(row % tp == t, res, o_ref[...])


def _slc_sample(u_s, cache, table, sel, layer, past_len, t_real):
    b, tp, _ = u_s.shape
    n_past_blk = past_len // SEL_BLOCK
    per_page = PAGE_SIZE // SEL_BLOCK
    assert past_len % SEL_BLOCK == 0 and t_real <= SEL_BLOCK and PAGE_SIZE % SEL_BLOCK == 0

    def blk_spec(k):
        def imap(bi, h, t, sel_ref, pt):
            blk = jnp.minimum(sel_ref[((bi * NSA_KV_HEADS + h) * t_real + t) * N_SEL + k], n_past_blk - 1)
            return (pt[bi, blk // per_page], layer, h, 0, 0, 0)
        return pl.BlockSpec((None, None, None, 2, HEAD_DIM, PAGE_SIZE), imap)

    grid_spec = pltpu.PrefetchScalarGridSpec(
        num_scalar_prefetch=2,
        grid=(b, NSA_KV_HEADS, t_real),
        in_specs=[blk_spec(k) for k in range(N_SEL)] + [
            pl.BlockSpec((None, tp, Q_HEAD_COLS), lambda bi, h, t, s, pt: (bi, 0, h)),
            pl.BlockSpec((None, tp, LANES), lambda bi, h, t, s, pt: (bi, 0, BLK_KVS * 4 + h)),
        ],
        out_specs=pl.BlockSpec((None, None, NSA_GROUP * tp, LANES), lambda bi, h, t, s, pt: (bi, h, 0, 0)),
    )
    return pl.pallas_call(
        functools.partial(_slc_sample_body, past_len=past_len, t_real=t_real),
        out_shape=jax.ShapeDtypeStruct((b, NSA_KV_HEADS, NSA_GROUP * tp, LANES), F32),
        grid_spec=grid_spec,
        compiler_params=_cparams(("arbitrary", "arbitrary", "arbitrary")),
        name="slc_sample",
    )(sel.reshape(-1), table, *([cache] * N_SEL), u_s, u_s)


def _win_combine_body(q_ref, misc_ref, win_ref, kvn_ref, oc_ref, os_ref, o_ref, *, past_len):
    tp = q_ref.shape[0]
    wb = win_ref.shape[0]
    t_pos = past_len + lax.broadcasted_iota(jnp.int32, (tp, 1), 0)
    gates = jax.nn.sigmoid(misc_ref[...])
    n_keys = wb + LANES
    idx = lax.broadcasted_iota(jnp.int32, (1, n_keys), 1)
    d = t_pos - (past_len - wb + idx)
    mask = (idx < wb + tp) & (d >= 0) & (d < WINDOW)
    heads = []
    for h in range(NSA_KV_HEADS):
        sl = slice(h * LANES, (h + 1) * LANES)
        qh = _stack_q(q_ref, h * Q_HEAD_COLS, HEAD_DIM ** -0.5)
        kv = jnp.concatenate([win_ref[:, sl], kvn_ref[:, sl], jnp.zeros((LANES - tp, LANES), F32)],
                             axis=0).astype(BF16)
        p_w = _softmax_groups(_dot_nt(qh, kv), d.astype(F32), mask, ALIBI[h], tp)
        o_w = _dot(jnp.concatenate(p_w, axis=0).astype(BF16), kv)
        o_c = oc_ref[h]
        o_s = os_ref[h]
        for g in range(NSA_GROUP):
            col = h * NSA_GROUP + g
            rows = slice(g * tp, (g + 1) * tp)
            heads.append(gates[:, col:col + 1] * o_c[rows]
                         + gates[:, NSA_HEADS + col:NSA_HEADS + col + 1] * o_s[rows]
                         + gates[:, 2 * NSA_HEADS + col:2 * NSA_HEADS + col + 1] * o_w[rows])
    _pack_heads(o_ref, heads)


def _win_combine(u_s, win, o_c, o_s, layer, past_len):
    b, tp, _ = u_s.shape
    wb = win.shape[2]
    return pl.pallas_call(
        functools.partial(_win_combine_body, past_len=past_len),
        out_shape=jax.ShapeDtypeStruct((b, tp, NSA_HEADS * HEAD_DIM), F32),
        grid=(b,),
        in_specs=[
            pl.BlockSpec((None, tp, NSA_HEADS * HEAD_DIM), lambda bi: (bi, 0, 0)),
            pl.BlockSpec((None, tp, LANES), lambda bi: (bi, 0, BLK_MISC)),
            pl.BlockSpec((None, None, wb, 4 * LANES), lambda bi: (bi, layer, 0, 0)),
            pl.BlockSpec((None, tp, 4 * LANES), lambda bi: (bi, 0, BLK_KVW)),
            pl.BlockSpec((None,) + o_c.shape[1:], lambda bi: (bi, 0, 0, 0)),
            pl.BlockSpec((None,) + o_s.shape[1:], lambda bi: (bi, 0, 0, 0)),
        ],
        out_specs=pl.BlockSpec((None, tp, NSA_HEADS * HEAD_DIM), lambda bi: (bi, 0, 0)),
        compiler_params=_cparams(("parallel",)),
        name="win_combine",
    )(u_s, u_s, win, u_s, o_c, o_s)


def _cumsum_rows(x):
    n = x.shape[0]
    row = lax.broadcasted_iota(jnp.int32, x.shape, 0)
    shift = 1
    while shift < n:
        x = x + jnp.where(row >= shift, pltpu.roll(x, shift, 0), 0.0)
        shift *= 2
    return x


def _gla_body(gqk_ref, gv_ref, gog_ref, misc_ref, wdec_ref, bdec_ref, gn_ref, s0_ref,
              o_ref, sout_ref, st_ref, *, c_real):
    ci = pl.program_id(1)
    cp = gqk_ref.shape[0]
    k_base = GLA_HEADS * GLA_DK
    sb = min(16, cp)

    @pl.when(ci == 0)
    def _():
        for h in range(GLA_HEADS):
            st_ref[h] = s0_ref[h].T

    row = lax.broadcasted_iota(jnp.int32, (cp, 1), 0)
    live = row < c_real
    x = jnp.dot(misc_ref[...], wdec_ref[...], preferred_element_type=F32, precision=HIGHEST) + bdec_ref[...]
    log_a = (jnp.minimum(x, 0.0) - jnp.log1p(jnp.exp(-jnp.abs(x)))) / GLA_GATE_TAU
    b_all = _cumsum_rows(jnp.where(live, log_a, 0.0))
    for h in range(GLA_HEADS):
        sl = slice(h * LANES, (h + 1) * LANES)
        pair = slice((h // 2) * LANES, (h // 2 + 1) * LANES)
        q = _unpack_head(gqk_ref[:, pair], h % 2 == 1) * GLA_DK ** -0.5
        k_pair = slice(k_base + (h // 2) * LANES, k_base + (h // 2 + 1) * LANES)
        k = jnp.where(live, _unpack_head(gqk_ref[:, k_pair], h % 2 == 1), 0.0)
        v = jnp.where(live, gv_ref[:, sl], 0.0)
        b = b_all[:, sl]
        kb = k.astype(BF16)
        st = st_ref[h]
        attn_rows = []
        for i in range(cp // sb):
            qi = q[i * sb:(i + 1) * sb]
            bi = b[i * sb:(i + 1) * sb]
            sub_row = lax.broadcasted_iota(jnp.int32, (sb, 1), 0)
            ys = []
            for s_loc in range(sb):
                bs = b[i * sb + s_loc:i * sb + s_loc + 1]
                ys.append(qi * jnp.exp(jnp.where(sub_row >= s_loc, bi - bs, NEG_INF)))
            z = _dot_nt(jnp.concatenate(ys, axis=0).astype(BF16), kb)
            lane = lax.broadcasted_iota(jnp.int32, (sb, cp), 1)
            a_i = jnp.zeros((sb, cp), F32)
            for s_loc in range(sb):
                a_i += jnp.where(lane == i * sb + s_loc, z[s_loc * sb:(s_loc + 1) * sb], 0.0)
            if i > 0:
                ref_b = b[i * sb - 1:i * sb]
                qt = qi * jnp.exp(bi - ref_b)
                kt = k * jnp.exp(jnp.where(row < i * sb, ref_b - b, NEG_INF))
                a_i += _dot_nt(qt.astype(BF16), kt.astype(BF16))
            attn_rows.append(a_i)
        attn = jnp.concatenate(attn_rows, axis=0) if len(attn_rows) > 1 else attn_rows[0]
        o = _dot(attn.astype(BF16), v.astype(BF16))
        o += _dot_nt((q * jnp.exp(b)).astype(BF16), st.astype(BF16))
        b_last = b[c_real - 1:c_real]
        kd = k * jnp.exp(b_last - b)
        st_ref[h] = jnp.exp(b_last) * st + _dot(v.T.astype(BF16), kd.astype(BF16))
        o_ref[:, sl] = _rms(o, gn_ref[...]) * jax.nn.silu(gog_ref[:, sl])

    @pl.when(ci == pl.num_programs(1) - 1)
    def _():
        for h in range(GLA_HEADS):
            sout_ref[h] = st_ref[h].T[:GLA_DK, :]


def _recurrent_body(gqk_ref, gv_ref, gog_ref, misc_ref, wdec_ref, bdec_ref, gn_ref, s0_ref,
                    rq_ref, rk_ref, rv_ref, rg_ref, rn_ref, r0_ref,
                    og_ref, sout_ref, or_ref, rout_ref, st_ref, rt_ref, *, c_real):
    _gla_body(gqk_ref, gv_ref, gog_ref, misc_ref, wdec_ref, bdec_ref, gn_ref, s0_ref,
              og_ref, sout_ref, st_ref, c_real=c_real)
    _ret_body(rq_ref, rk_ref, rv_ref, rg_ref, rn_ref, r0_ref, or_ref, rout_ref, rt_ref, c_real=c_real)


def _recurrent(u, s0, r0, w_dec, b_dec, gla_norm, ret_norm, layer, cp, c_real):
    b, t, _ = u.shape
    assert t % cp == 0
    blk = lambda idx: pl.BlockSpec((None, cp, 4 * LANES), lambda bi, i: (bi, i, idx))
    per_b = lambda shape: pl.BlockSpec((None,) + shape, lambda bi, i: (bi, 0, 0, 0))
    per_layer = lambda shape: pl.BlockSpec((None,) + shape, lambda bi, i: (layer, 0, 0))
    return pl.pallas_call(
        functools.partial(_recurrent_body, c_real=c_real),
        out_shape=(jax.ShapeDtypeStruct((b, t, GLA_HEADS * GLA_DV), F32),
                   jax.ShapeDtypeStruct((b, GLA_HEADS, GLA_DK, GLA_DV), F32),
                   jax.ShapeDtypeStruct((b, t, RET_HEADS * RET_DV), F32),
                   jax.ShapeDtypeStruct((b, RET_HEADS, RET_DK, RET_DV), F32)),
        grid=(b, t // cp),
        in_specs=[
            blk(BLK_GQK), blk(BLK_GV), blk(BLK_GOG),
            pl.BlockSpec((None, cp, LANES), lambda bi, i: (bi, i, BLK_MISC)),
            per_layer((LANES, 4 * LANES)), per_layer((1, 4 * LANES)), per_layer((1, GLA_DV)),
            per_b((GLA_HEADS, LANES, GLA_DV)),
            blk(BLK_RQ), blk(BLK_RK), blk(BLK_RV), blk(BLK_RG),
            per_layer((1, RET_HEADS * RET_DV)),
            per_b((RET_HEADS, RET_DK, RET_DV)),
        ],
        out_specs=(pl.BlockSpec((None, cp, GLA_HEADS * GLA_DV), lambda bi, i: (bi, i, 0)),
                   per_b((GLA_HEADS, GLA_DK, GLA_DV)),
                   pl.BlockSpec((None, cp, RET_HEADS * RET_DV), lambda bi, i: (bi, i, 0)),
                   per_b((RET_HEADS, RET_DK, RET_DV))),
        scratch_shapes=[pltpu.VMEM((GLA_HEADS, GLA_DV, LANES), F32),
                        pltpu.VMEM((RET_HEADS, RET_DV, RET_DK), F32)],
        compiler_params=_cparams(("parallel", "arbitrary")),
        name="recurrent",
    )(u, u, u, u, w_dec, b_dec, gla_norm, s0, u, u, u, u, ret_norm, r0)


def _ret_body(rq_ref, rk_ref, rv_ref, rg_ref, gn_ref, r0_ref, o_ref, rout_ref, rt_ref, *, c_real):
    ci = pl.program_id(1)
    cp = rq_ref.shape[0]

    @pl.when(ci == 0)
    def _():
        for h in range(RET_HEADS):
            rt_ref[h] = r0_ref[h].T

    row = lax.broadcasted_iota(jnp.int32, (cp, 1), 0)
    live = row < c_real
    rowf = row.astype(F32)
    rel = rowf - lax.broadcasted_iota(jnp.int32, (1, cp), 1).astype(F32)
    for h in range(RET_HEADS):
        sl = slice(h * LANES, (h + 1) * LANES)
        lg = RET_LOG_GAMMA[h]
        q = rq_ref[:, sl]
        k = jnp.where(live, rk_ref[:, sl] * RET_DK ** -0.5, 0.0)
        v = jnp.where(live, rv_ref[:, sl], 0.0)
        decay = jnp.where(rel >= 0, jnp.exp(jnp.maximum(rel, 0.0) * lg), 0.0)
        attn = _dot_nt(q.astype(BF16), k.astype(BF16)) * decay
        o = _dot(attn.astype(BF16), v.astype(BF16))
        rt = rt_ref[h]
        o += _dot_nt((q * jnp.exp((rowf + 1.0) * lg)).astype(BF16), rt.astype(BF16))
        kd = k * jnp.exp((c_real - 1.0 - rowf) * lg)
        rt_ref[h] = math.exp(c_real * lg) * rt + _dot(v.T.astype(BF16), kd.astype(BF16))
        mu = jnp.mean(o, axis=-1, keepdims=True)
        var = jnp.mean(jnp.square(o - mu), axis=-1, keepdims=True)
        o_ref[:, sl] = (o - mu) * lax.rsqrt(var + EPS) * gn_ref[:, sl] * jax.nn.silu(rg_ref[:, sl])

    @pl.when(ci == pl.num_programs(1) - 1)
    def _():
        for h in range(RET_HEADS):
            rout_ref[h] = rt_ref[h].T


def _pad_heads(w, n_heads):
    lead = w.shape[:-1]
    w = w.reshape(lead + (n_heads, HEAD_DIM))
    w = jnp.pad(w, [(0, 0)] * len(lead) + [(0, 0), (0, LANES - HEAD_DIM)])
    return w.reshape(lead + (n_heads * LANES,))


def _prep_w_in(w_in):
    cuts = np.cumsum((0,) + IN_SIZES)
    nq, kvc, kvs, kvw, ng, gq, gk, gv, glr, gog, rq, rk, rv, rg = [
        w_in[..., cuts[i]:cuts[i + 1]] for i in range(len(IN_SIZES))]
    misc = jnp.concatenate([ng, glr], axis=-1)
    misc = jnp.pad(misc, [(0, 0), (0, 0), (0, LANES - misc.shape[-1])])
    w = jnp.concatenate([nq, kvc, kvs, kvw, gq, gk, gv, gog, rq, rk, rv, rg, misc], axis=-1)
    assert w.shape[-1] == (BLK_MISC + 1) * LANES == U_WIDTH
    return w.astype(BF16)


def _prep_cmp_weights(w_cmp1, w_cmp2):
    depth = w_cmp1.shape[0]
    half = CMP_STRIDE * HEAD_DIM
    w1 = w_cmp1.reshape(depth, 2, 2, CMP_STRIDE, HEAD_DIM, CMP_HIDDEN)
    w1 = w1.transpose(0, 3, 1, 4, 2, 5)
    z = jnp.zeros_like(w1[:, :, 0])
    k_rows = jnp.stack([w1[:, :, 0], z], axis=4)
    v_rows = jnp.stack([z, w1[:, :, 1]], axis=4)
    wblk = jnp.concatenate([k_rows, v_rows], axis=2)
    wblk = wblk.reshape(depth, CMP_STRIDE // 2, 4 * HEAD_DIM, 4 * CMP_HIDDEN).astype(BF16)
    del half
    z2 = jnp.zeros_like(w_cmp2[:, 0])
    w2blk = jnp.concatenate([jnp.concatenate([w_cmp2[:, 0], z2], axis=-1),
                             jnp.concatenate([z2, w_cmp2[:, 1]], axis=-1)], axis=1).astype(BF16)
    return wblk, w2blk


def _overlap_matrix(n_rows, n_slc, n_cols):
    ratio = SEL_BLOCK // CMP_STRIDE
    w = np.zeros((n_rows, n_cols), np.float32)
    for j in range(n_slc):
        for k, wk in enumerate(OVERLAP_W):
            if ratio * j + k < n_rows:
                w[ratio * j + k, j] = wk
    return jnp.asarray(w)


def kernel(x_prompt, x_sample, cache_cmp_kv, cache_slc_kv, cache_win_kv, state_gla, state_ret, page_table,
           p_prompt, p_sample, norm_pre, norm_post, w_ffn_gate, w_ffn_up, w_ffn_down, w_in, w_out,
           w_cmp1, w_cmp2, cmp_pos, nsa_norm, w_gla_decay, b_gla_decay, gla_norm, ret_norm, w_ple, w_ple_gate):
    depth = w_in.shape[0]
    bp, seq, d_model = x_prompt.shape
    bs, t_real, _ = x_sample.shape
    n_pool = cache_cmp_kv.shape[0]
    n_pages = page_table.shape[1]
    past_len = n_pages * PAGE_SIZE
    tp = -(-t_real // SUBLANES) * SUBLANES
    kvw = NSA_KV_HEADS * 2 * HEAD_DIM

    wg = w_ffn_gate.astype(BF16)
    wu = w_ffn_up.astype(BF16)
    wd = w_ffn_down.astype(BF16)
    w_in_r = _prep_w_in(w_in)
    w_out_b = w_out.astype(BF16)
    w_ple_b = w_ple.astype(BF16)
    w_gate_b = w_ple_gate.astype(BF16)
    wblk, w2blk = _prep_cmp_weights(w_cmp1, w_cmp2)
    g_pre = norm_pre.reshape(depth, 3, 1, d_model)
    g_post = norm_post.reshape(depth, 3, 1, d_model)
    nsa_g = nsa_norm.reshape(depth, 1, -1)
    gla_g = gla_norm.reshape(depth, 1, -1)
    ret_g = ret_norm.reshape(depth, 1, -1)
    w_dec = jnp.pad(_pad_heads(w_gla_decay, GLA_HEADS),
                    [(0, 0), (MISC_GLR, LANES - MISC_GLR - GLA_GATE_RANK), (0, 0)])
    b_dec = _pad_heads(b_gla_decay, GLA_HEADS).reshape(depth, 1, -1)
    pos_flat = cmp_pos.reshape(depth, 2, 1, CMP_BLOCK * HEAD_DIM)

    cmp_pages = cache_cmp_kv.transpose(0, 1, 3, 4, 5, 2)
    slc_rows = cache_slc_kv.transpose(0, 1, 3, 4, 5, 2)
    win_rows = cache_win_kv.reshape(bs, depth, -1, kvw)
    wb = win_rows.shape[2]

    n_cr_p = seq // CMP_STRIDE
    wov_p = _overlap_matrix(n_cr_p, -(-seq // SEL_BLOCK), LANES).T
    n_cr_s = past_len // CMP_STRIDE
    n_slc_s = -(-(past_len + t_real) // SEL_BLOCK)
    wov_s = _overlap_matrix(n_cr_s, n_slc_s, -(-n_slc_s // LANES) * LANES)
    table_p = jnp.arange(bp * (seq // PAGE_SIZE), dtype=jnp.int32).reshape(bp, seq // PAGE_SIZE)

    xp = x_prompt.reshape(bp * seq, d_model)
    xs = jnp.pad(x_sample, ((0, 0), (0, tp - t_real), (0, 0))).reshape(bs * tp, d_model)
    pp = p_prompt.reshape(depth, bp * seq, -1)
    ps = jnp.pad(p_sample, ((0, 0), (0, 0), (0, tp - t_real), (0, 0))).reshape(depth, bs * tp, -1)
    s0_gla_p = jnp.zeros((bp, GLA_HEADS, LANES, GLA_DV), F32)
    s0_ret_p = jnp.zeros((bp, RET_HEADS, RET_DK, RET_DV), F32)
    s0_gla_s = jnp.pad(state_gla, ((0, 0), (0, 0), (0, 0), (0, LANES - GLA_DK), (0, 0)))

    tm_p = 1024
    tm_s = bs * tp
    st_p = [[] for _ in range(5)]
    st_s = [[] for _ in range(5)]
    for l in range(depth):
        bias = _cmp_bias(pos_flat, w_cmp1, l).reshape(1, 2 * CMP_HIDDEN)

        xp = _ffn(xp, g_pre, g_post, wg, wu, wd, l, 0, 0, tm_p // 2)
        u2 = _proj_in(xp, g_pre, w_in_r, l, tm_p)
        u = u2.reshape(bp, seq, U_WIDTH)
        kv_c = u[..., BLK_KVC * kvw:(BLK_KVC + 1) * kvw]
        kv_s = u[..., BLK_KVS * kvw:(BLK_KVS + 1) * kvw]
        kv_w = u[..., BLK_KVW * kvw:(BLK_KVW + 1) * kvw]
        kc_pages = kv_c.reshape(bp * (seq // PAGE_SIZE), 1, PAGE_SIZE, NSA_KV_HEADS, LANES).transpose(0, 1, 3, 2, 4)
        kc = _compress(kc_pages, table_p, 0,
                       wblk[l], bias, w2blk[l])
        o_nsa = _nsa_prompt(u, kc, wov_p)
        gla_out, s_g, ret_out, s_r = _recurrent(u, s0_gla_p, s0_ret_p, w_dec, b_dec, gla_g, ret_g, l,
                                                GLA_CHUNK, GLA_CHUNK)
        xp = _proj_out(xp, o_nsa.reshape(bp * seq, -1), gla_out.reshape(bp * seq, -1),
                       ret_out.reshape(bp * seq, -1), nsa_g, w_out_b, g_post, l, tm_p // 2)
        xp = _ffn(xp, g_pre, g_post, wg, wu, wd, l, 1, 2, tm_p // 2)
        xp = _ple(xp, pp, w_gate_b, w_ple_b, l, tm_p // 2)
        win_keep = min(WINDOW, seq)
        for j, a in enumerate((kv_c, kv_s, kv_w[:, seq - win_keep:], s_g, s_r)):
            st_p[j].append(a)

        xs = _ffn(xs, g_pre, g_post, wg, wu, wd, l, 0, 0, tm_s)
        us = _proj_in(xs, g_pre, w_in_r, l, tm_s).reshape(bs, tp, U_WIDTH)
        kv_c = us[:, :t_real, BLK_KVC * kvw:(BLK_KVC + 1) * kvw]
        kv_s = us[:, :t_real, BLK_KVS * kvw:(BLK_KVS + 1) * kvw]
        kv_w = us[:, :t_real, BLK_KVW * kvw:(BLK_KVW + 1) * kvw]
        kc = _compress(cmp_pages, page_table, l, wblk[l], bias, w2blk[l])
        o_c, sel = _cmp_select(us, kc, wov_s, past_len, t_real)
        o_s = _slc_sample(us, slc_rows, page_table, sel[:, :, :t_real, :N_SEL], l, past_len, t_real)
        o_nsa = _win_combine(us, win_rows, o_c, o_s, l, past_len)
        gla_out, s_g, ret_out, s_r = _recurrent(us, s0_gla_s[:, l], state_ret[:, l], w_dec, b_dec, gla_g, ret_g, l,
                                                tp, t_real)
        xs = _proj_out(xs, o_nsa.reshape(bs * tp, -1), gla_out.reshape(bs * tp, -1),
                       ret_out.reshape(bs * tp, -1), nsa_g, w_out_b, g_post, l, tm_s)
        xs = _ffn(xs, g_pre, g_post, wg, wu, wd, l, 1, 2, tm_s)
        xs = _ple(xs, ps, w_gate_b, w_ple_b, l, tm_s)
        new_win = jnp.concatenate([win_rows[:, l], kv_w], axis=1)[:, t_real:]
        for j, a in enumerate((kv_c, kv_s, new_win, s_g, s_r)):
            st_s[j].append(a)

    def kv_stack(parts):
        a = jnp.stack(parts, axis=1)
        return a.reshape(a.shape[:3] + (NSA_KV_HEADS, 2, HEAD_DIM))

    y_p = xp.reshape(bp, seq, d_model)
    y_s = xs.reshape(bs, tp, d_model)[:, :t_real]
    return (y_p, y_s,
            kv_stack(st_p[0]), kv_stack(st_p[1]), kv_stack(st_p[2]),
            jnp.stack(st_p[3], axis=1), jnp.stack(st_p[4], axis=1),
            kv_stack(st_s[0]), kv_stack(st_s[1]), kv_stack(st_s[2]),
            jnp.stack(st_s[3], axis=1), jnp.stack(st_s[4], axis=1))
```

```python
import functools
import math

import numpy as np
import jax
import jax.numpy as jnp
from jax import lax
from jax.experimental import pallas as pl
from jax.experimental.pallas import tpu as pltpu

F32 = jnp.float32
BF16 = jnp.bfloat16
HIGHEST = lax.Precision.HIGHEST

HEAD_DIM = 64
NSA_HEADS = 16
NSA_KV_HEADS = 4
NSA_GROUP = 4
CMP_BLOCK = 32
CMP_STRIDE = 16
CMP_HIDDEN = 128
SEL_BLOCK = 64
N_SEL = 16
WINDOW = 512
Q_BLOCK = 128
OVERLAP_W = (0.5, 1.0, 1.0, 1.0, 0.5)
GLA_HEADS = 4
GLA_DK = 64
GLA_DV = 128
GLA_GATE_RANK = 16
GLA_GATE_TAU = 16.0
GLA_CHUNK = 64
RET_HEADS = 4
RET_DK = 128
RET_DV = 128
RET_CHUNK = 64
PAGE_SIZE = 128
NEG_INF = -1e30
FORCE = 1e9
EPS = 1e-6
IN_SIZES = (1024, 512, 512, 512, 48, 256, 256, 512, 16, 512, 512, 512, 512, 512)

LANES = 128
SUBLANES = 8
VMEM_LIMIT = 56 * 1024 * 1024

U_WIDTH = 6272
BLK_KVC, BLK_KVS, BLK_KVW = 2, 3, 4
BLK_GQK, BLK_GV, BLK_GOG = 5, 6, 7
BLK_RQ, BLK_RK, BLK_RV, BLK_RG = 8, 9, 10, 11
BLK_MISC = 48
MISC_GLR = 48
Q_HEAD_COLS = NSA_GROUP * HEAD_DIM

ALIBI = [[2.0 ** (-8.0 * (h * NSA_GROUP + g + 1) / NSA_HEADS) for g in range(NSA_GROUP)]
         for h in range(NSA_KV_HEADS)]
RET_LOG_GAMMA = [math.log1p(-(2.0 ** (-5.0 - h))) for h in range(RET_HEADS)]


def _cparams(sem):
    return pltpu.CompilerParams(dimension_semantics=sem, vmem_limit_bytes=VMEM_LIMIT)


def _rms(x, g=None):
    y = x * lax.rsqrt(jnp.mean(x * x, axis=-1, keepdims=True) + EPS)
    return y if g is None else y * g


def _dot(a, b):
    return jnp.dot(a, b, preferred_element_type=F32)


def _dot_nt(a, b):
    return lax.dot_general(a, b, (((1,), (1,)), ((), ())), preferred_element_type=F32)


def _ffn_body(x_ref, gpre_ref, wg_ref, wu_ref, wd_ref, gpost_ref, o_ref, xn_ref, acc_ref):
    f = pl.program_id(1)

    @pl.when(f == 0)
    def _():
        xn_ref[...] = _rms(x_ref[...], gpre_ref[...]).astype(BF16)
        acc_ref[...] = jnp.zeros_like(acc_ref)

    xn = xn_ref[...]
    h = jax.nn.silu(_dot(xn, wg_ref[...])) * _dot(xn, wu_ref[...])
    acc_ref[...] += _dot(h.astype(BF16), wd_ref[...])

    @pl.when(f == pl.num_programs(1) - 1)
    def _():
        o_ref[...] = x_ref[...] + 0.5 * _rms(acc_ref[...], gpost_ref[...])


def _ffn(x, g_pre, g_post, wg, wu, wd, layer, which, norm_idx, tm, tf=512):
    m, d = x.shape
    ff = wg.shape[-1]
    assert m % tm == 0 and ff % tf == 0
    return pl.pallas_call(
        _ffn_body,
        out_shape=jax.ShapeDtypeStruct((m, d), F32),
        grid=(m // tm, ff // tf),
        in_specs=[
            pl.BlockSpec((tm, d), lambda i, f: (i, 0)),
            pl.BlockSpec((None, None, 1, d), lambda i, f: (layer, norm_idx, 0, 0)),
            pl.BlockSpec((None, None, d, tf), lambda i, f: (layer, which, 0, f)),
            pl.BlockSpec((None, None, d, tf), lambda i, f: (layer, which, 0, f)),
            pl.BlockSpec((None, None, tf, d), lambda i, f: (layer, which, f, 0)),
            pl.BlockSpec((None, None, 1, d), lambda i, f: (layer, norm_idx, 0, 0)),
        ],
        out_specs=pl.BlockSpec((tm, d), lambda i, f: (i, 0)),
        scratch_shapes=[pltpu.VMEM((tm, d), BF16), pltpu.VMEM((tm, d), F32)],
        compiler_params=_cparams(("parallel", "arbitrary")),
        name="ffn",
    )(x, g_pre, wg, wu, wd, g_post)


def _proj_in_body(x_ref, g_ref, w_ref, o_ref, xn_ref):
    @pl.when(pl.program_id(1) == 0)
    def _():
        xn_ref[...] = _rms(x_ref[...], g_ref[...]).astype(BF16)

    o_ref[...] = _dot(xn_ref[...], w_ref[...])


def _proj_in(x, g_pre, w_in, layer, tm, tn=896):
    m, d = x.shape
    n = w_in.shape[-1]
    return pl.pallas_call(
        _proj_in_body,
        out_shape=jax.ShapeDtypeStruct((m, n), F32),
        grid=(m // tm, n // tn),
        in_specs=[
            pl.BlockSpec((tm, d), lambda i, j: (i, 0)),
            pl.BlockSpec((None, None, 1, d), lambda i, j: (layer, 1, 0, 0)),
            pl.BlockSpec((None, d, tn), lambda i, j: (layer, 0, j)),
        ],
        out_specs=pl.BlockSpec((tm, tn), lambda i, j: (i, j)),
        scratch_shapes=[pltpu.VMEM((tm, d), BF16)],
        compiler_params=_cparams(("parallel", "arbitrary")),
        name="proj_in",
    )(x, g_pre, w_in)


def _proj_out_body(x_ref, nsa_ref, gla_ref, ret_ref, gn_ref, w_ref, gpost_ref, o_ref):
    nsa_w = nsa_ref.shape[-1]
    gla_w = gla_ref.shape[-1]
    nsa = _rms(nsa_ref[...], gn_ref[...]).astype(BF16)
    y = _dot(nsa, w_ref[0:nsa_w, :])
    y += _dot(gla_ref[...].astype(BF16), w_ref[nsa_w:nsa_w + gla_w, :])
    y += _dot(ret_ref[...].astype(BF16), w_ref[nsa_w + gla_w:, :])
    o_ref[...] = x_ref[...] + _rms(y, gpost_ref[...])


def _proj_out(x, o_nsa, gla_out, ret_out, nsa_norm, w_out, g_post, layer, tm):
    m, d = x.shape
    row = lambda a: pl.BlockSpec((tm, a.shape[-1]), lambda i: (i, 0))
    return pl.pallas_call(
        _proj_out_body,
        out_shape=jax.ShapeDtypeStruct((m, d), F32),
        grid=(m // tm,),
        in_specs=[
            row(x), row(o_nsa), row(gla_out), row(ret_out),
            pl.BlockSpec((None, 1, o_nsa.shape[-1]), lambda i: (layer, 0, 0)),
            pl.BlockSpec((None, w_out.shape[1], d), lambda i: (layer, 0, 0)),
            pl.BlockSpec((None, None, 1, d), lambda i: (layer, 1, 0, 0)),
        ],
        out_specs=row(x),
        compiler_params=_cparams(("parallel",)),
        name="proj_out",
    )(x, o_nsa, gla_out, ret_out, nsa_norm, w_out, g_post)


def _ple_body(x_ref, p_ref, wg_ref, wp_ref, o_ref):
    x = x_ref[...]
    gate = jax.nn.sigmoid(_dot(_rms(x).astype(BF16), wg_ref[...]))
    o_ref[...] = x + gate * _dot(p_ref[...].astype(BF16), wp_ref[...])


def _ple(x, p, w_gate, w_ple, layer, tm):
    m, d = x.shape
    return pl.pallas_call(
        _ple_body,
        out_shape=jax.ShapeDtypeStruct((m, d), F32),
        grid=(m // tm,),
        in_specs=[
            pl.BlockSpec((tm, d), lambda i: (i, 0)),
            pl.BlockSpec((None, tm, p.shape[-1]), lambda i: (layer, i, 0)),
            pl.BlockSpec((None, d, d), lambda i: (layer, 0, 0)),
            pl.BlockSpec((None, p.shape[-1], d), lambda i: (layer, 0, 0)),
        ],
        out_specs=pl.BlockSpec((tm, d), lambda i: (i, 0)),
        compiler_params=_cparams(("parallel",)),
        name="ple",
    )(x, p, w_gate, w_ple)


COMPRESS_PAGE_STEPS = (32, 16)
CHUNKS_PER_PAGE = PAGE_SIZE // CMP_STRIDE
KV_ROW = NSA_KV_HEADS * 2 * HEAD_DIM


def _cmp_bias_body(pos_ref, w1_ref, o_ref):
    o_ref[...] = jnp.dot(pos_ref[...], w1_ref[...], preferred_element_type=F32, precision=HIGHEST)


def _cmp_bias(cmp_pos, w_cmp1, layer):
    kdim = w_cmp1.shape[2]
    return pl.pallas_call(
        _cmp_bias_body,
        out_shape=jax.ShapeDtypeStruct((2, 1, CMP_HIDDEN), F32),
        grid=(2,),
        in_specs=[pl.BlockSpec((None, None, 1, kdim), lambda c: (layer, c, 0, 0)),
                  pl.BlockSpec((None, None, kdim, CMP_HIDDEN), lambda c: (layer, c, 0, 0))],
        out_specs=pl.BlockSpec((None, 1, CMP_HIDDEN), lambda c: (c, 0, 0)),
        compiler_params=_cparams(("arbitrary",)),
        name="cmp_bias",
    )(cmp_pos, w_cmp1)


def _compress_body(pt_ref, *refs, feature_major, pps):
    del pt_ref
    page_refs = refs[:pps]
    wblk_ref, bias_ref, w2_ref, o_ref, prev_ref = refs[pps:pps + 5]
    rows = pps * CHUNKS_PER_PAGE
    half = 2 * CMP_HIDDEN

    @pl.when(pl.program_id(1) == 0)
    def _():
        prev_ref[...] = jnp.zeros_like(prev_ref)

    first_row = lax.broadcasted_iota(jnp.int32, (rows, half), 0) == 0

    if feature_major:
        tok_ref = refs[pps + 5]
        for k, pr in enumerate(page_refs):
            for h in range(NSA_KV_HEADS):
                tok_ref[k * NSA_KV_HEADS + h] = pr[h].reshape(LANES, PAGE_SIZE).T

        def token_rows(t, h):
            return jnp.concatenate([tok_ref[k * NSA_KV_HEADS + h, pl.ds(t, CHUNKS_PER_PAGE, stride=CMP_STRIDE), :]
                                    for k in range(pps)], axis=0)
    else:
        def token_rows(t, h):
            return jnp.concatenate([pr[h, pl.ds(t, CHUNKS_PER_PAGE, stride=CMP_STRIDE), :]
                                    for pr in page_refs], axis=0)

    for h in range(NSA_KV_HEADS):
        acc = jnp.zeros((rows, 2 * half), F32)
        for tt in range(CMP_STRIDE // 2):
            lhs = jnp.concatenate([token_rows(2 * tt, h), token_rows(2 * tt + 1, h)], axis=1).astype(BF16)
            acc += _dot(lhs, wblk_ref[tt])
        first = acc[:, :half]
        second = acc[:, half:]
        carry = prev_ref[h][SUBLANES - 1:SUBLANES, :]
        shifted = jnp.where(first_row, carry, pltpu.roll(first, 1, 0))
        prev_ref[h] = first[rows - SUBLANES:, :]
        hidden = jax.nn.gelu(shifted + second + bias_ref[...])
        o_ref[:, h * LANES:(h + 1) * LANES] = _dot(hidden.astype(BF16), w2_ref[...])


def _compress(pages, table, layer, wblk, bias, w2blk):
    b, n_pages = table.shape
    pps = next(p for p in COMPRESS_PAGE_STEPS if n_pages % p == 0)
    rows = pps * CHUNKS_PER_PAGE
    feature_major = pages.ndim == 6
    page_block = (None, None) + pages.shape[2:]
    zeros = (0,) * (pages.ndim - 2)

    def page_spec(k):
        return pl.BlockSpec(page_block, lambda bi, i, pt: (pt[bi, i * pps + k], layer) + zeros)

    scratch = [pltpu.VMEM((NSA_KV_HEADS, SUBLANES, 2 * CMP_HIDDEN), F32)]
    if feature_major:
        scratch.append(pltpu.VMEM((pps * NSA_KV_HEADS, PAGE_SIZE, LANES), F32))

    grid_spec = pltpu.PrefetchScalarGridSpec(
        num_scalar_prefetch=1,
        grid=(b, n_pages // pps),
        in_specs=[page_spec(k) for k in range(pps)] + [
            pl.BlockSpec(wblk.shape, lambda bi, i, pt: (0, 0, 0)),
            pl.BlockSpec(bias.shape, lambda bi, i, pt: (0, 0)),
            pl.BlockSpec(w2blk.shape, lambda bi, i, pt: (0, 0)),
        ],
        out_specs=pl.BlockSpec((None, rows, NSA_KV_HEADS * LANES), lambda bi, i, pt: (bi, i, 0)),
        scratch_shapes=scratch,
    )
    return pl.pallas_call(
        functools.partial(_compress_body, feature_major=feature_major, pps=pps),
        out_shape=jax.ShapeDtypeStruct((b, n_pages * CHUNKS_PER_PAGE, NSA_KV_HEADS * LANES), F32),
        grid_spec=grid_spec,
        compiler_params=_cparams(("arbitrary", "arbitrary")),
        name="compress",
    )(table, *([pages] * pps), wblk, bias, w2blk)


def _softmax_groups(s, distf, mask, slopes, rows):
    parts = []
    for g in range(NSA_GROUP):
        sg = s[g * rows:(g + 1) * rows] - slopes[g] * distf
        sg = jnp.where(mask, sg, NEG_INF)
        m = jnp.max(sg, axis=-1, keepdims=True)
        e = jnp.where(mask, jnp.exp(sg - m), 0.0)
        parts.append(e / jnp.maximum(jnp.sum(e, axis=-1, keepdims=True), 1e-30))
    return parts


def _unpack_head(pair, odd):
    lane = lax.broadcasted_iota(jnp.int32, pair.shape, 1)
    return jnp.where(lane < HEAD_DIM, pltpu.roll(pair, HEAD_DIM, 1) if odd else pair, 0.0)


def _stack_q(q_ref, base, scale):
    parts = []
    for g in range(NSA_GROUP):
        c0 = base + (g // 2) * LANES
        parts.append(_unpack_head(q_ref[:, c0:c0 + LANES], g % 2 == 1))
    return (jnp.concatenate(parts, axis=0) * scale).astype(BF16)


def _block_scores(imp, wov_ref, t_pos, n_slc):
    score = jnp.dot(imp, wov_ref[...], preferred_element_type=F32, precision=HIGHEST)
    blk = lax.broadcasted_iota(jnp.int32, score.shape, 1)
    cur = t_pos // SEL_BLOCK
    valid = blk <= cur
    forced = valid & ((blk == 0) | (blk == cur) | (blk == cur - 1))
    score = jnp.where(forced, FORCE, jnp.where(valid, score, -FORCE))
    return score, blk


def _pack_heads(o_ref, heads):
    lane = lax.broadcasted_iota(jnp.int32, heads[0].shape, 1)
    for p in range(NSA_HEADS // 2):
        even = pltpu.roll(heads[2 * p], HEAD_DIM, 1)
        o_ref[:, p * LANES:(p + 1) * LANES] = jnp.where(lane < HEAD_DIM, even, heads[2 * p + 1])


SLC_KEY_CHUNK = 512
LOG2E = 1.4426950408889634
MASK_BIG = 2.0 ** 100
ALIBI2 = [[s * LOG2E for s in row] for row in ALIBI]


def _exp2_softmax(s2, rel_row, bias, slopes2, rows):
    out = []
    for g in range(NSA_GROUP):
        lg = s2[g * rows:(g + 1) * rows] + (slopes2[g] * rel_row + bias)
        e = jnp.exp2(lg - jnp.max(lg, axis=-1, keepdims=True))
        out.append((e, jnp.sum(e, axis=-1, keepdims=True)))
    return out


def _nsa_prompt_body(q_ref, misc_ref, kc_ref, kvs_ref, kvw_ref, wovt_ref, o_ref, *, seq):
    qb = pl.program_id(1)
    t0 = qb * Q_BLOCK
    t0f = t0.astype(F32)
    n_slc = -(-seq // SEL_BLOCK)
    n_sel = min(N_SEL, n_slc)
    n_cr = kc_ref.shape[0]
    t_col = (t0 + lax.broadcasted_iota(jnp.int32, (Q_BLOCK, 1), 0)).astype(F32)
    gates = jax.nn.sigmoid(misc_ref[...])

    r_idx = lax.broadcasted_iota(jnp.int32, (1, n_cr), 1)
    end_c = (r_idx * CMP_STRIDE + (CMP_STRIDE - 1)).astype(F32)
    bias_c = jnp.minimum(t_col - end_c, 0.0) * MASK_BIG + jnp.where(r_idx >= 1, 0.0, -MASK_BIG)
    rel_c = end_c - t0f
    row_live = jnp.where(t_col >= CMP_BLOCK - 1, 1.0, 0.0)
    band = WINDOW + Q_BLOCK
    w0 = pl.multiple_of(jnp.maximum(t0 - WINDOW, 0), Q_BLOCK)
    pos_w = (w0 + lax.broadcasted_iota(jnp.int32, (1, band), 1)).astype(F32)
    d_w = t_col - pos_w
    bias_w = (jnp.minimum(d_w, 0.0) + jnp.minimum((WINDOW - 1.0) - d_w, 0.0)) * MASK_BIG
    rel_w = pos_w - t0f

    blk_t = lax.broadcasted_iota(jnp.int32, (n_slc, Q_BLOCK), 0)
    cur_t = (t0 + lax.broadcasted_iota(jnp.int32, (1, Q_BLOCK), 1)) // SEL_BLOCK
    valid_t = blk_t <= cur_t
    forced_t = valid_t & ((blk_t == 0) | (blk_t == cur_t) | (blk_t == cur_t - 1))
    tm1 = t_col - 1.0

    kv_heads = range(NSA_KV_HEADS)
    lane_sl = [slice(h * LANES, (h + 1) * LANES) for h in kv_heads]
    qhs = [_stack_q(q_ref, h * Q_HEAD_COLS, HEAD_DIM ** -0.5 * LOG2E) for h in kv_heads]

    o_cs, scores = [], []
    for h in kv_heads:
        kc = kc_ref[:, lane_sl[h]].astype(BF16)
        sm_c = _exp2_softmax(_dot_nt(qhs[h], kc), rel_c, bias_c, ALIBI2[h], Q_BLOCK)
        p_c = [e * (row_live / l) for (e, l) in sm_c]
        o_cs.append(_dot(jnp.concatenate(p_c, axis=0).astype(BF16), kc))
        imp = p_c[0] + p_c[1] + p_c[2] + p_c[3]
        score_t = lax.dot_general(wovt_ref[...], imp, (((1,), (1,)), ((), ())),
                                  preferred_element_type=F32, precision=HIGHEST)[:n_slc]
        scores.append(jnp.where(forced_t, FORCE, jnp.where(valid_t, score_t, -FORCE)))

    def ranked():
        out = []
        for score_t in scores:
            rank = jnp.zeros(score_t.shape, F32)
            for i in range(n_slc):
                ci = score_t[i:i + 1, :]
                tie = jnp.where(blk_t > i, 1.0, 0.0)
                rank += jnp.where(ci > score_t, 1.0, jnp.where(ci == score_t, tie, 0.0))
            out.append(jnp.where(rank < n_sel, MASK_BIG, 0.0))
        return tuple(out)

    sel_ts = lax.cond(t0 + Q_BLOCK > n_sel * SEL_BLOCK, ranked,
                      lambda: tuple(jnp.where(valid_t, MASK_BIG, 0.0) for _ in kv_heads))
    sels = [jnp.concatenate([s, jnp.zeros((LANES - n_slc, Q_BLOCK), F32)], axis=0).T.astype(BF16)
            for s in sel_ts]

    def chunk(c, carry):
        k0 = pl.multiple_of(c * SLC_KEY_CHUNK, SLC_KEY_CHUNK)
        pos = k0 + lax.broadcasted_iota(jnp.int32, (1, SLC_KEY_CHUNK), 1)
        posf = pos.astype(F32)
        expand = jnp.where(
            (pos // SEL_BLOCK) == lax.broadcasted_iota(jnp.int32, (LANES, SLC_KEY_CHUNK), 0),
            1.0, 0.0).astype(BF16)
        causal = jnp.minimum(tm1 - posf, -1.0) * MASK_BIG
        rel = posf - t0f
        new = []
        for h in kv_heads:
            kv = kvs_ref[pl.ds(k0, SLC_KEY_CHUNK), lane_sl[h]].astype(BF16)
            s2 = _dot_nt(qhs[h], kv)
            bias = _dot(sels[h], expand) + causal
            out = []
            for g in range(NSA_GROUP):
                m_old, l_old, a_old = carry[h][g]
                lg = s2[g * Q_BLOCK:(g + 1) * Q_BLOCK] + (ALIBI2[h][g] * rel + bias)
                m_new = jnp.maximum(m_old, jnp.max(lg, axis=-1, keepdims=True))
                alpha = jnp.exp2(m_old - m_new)
                e = jnp.exp2(lg - m_new)
                l_new = alpha * l_old + jnp.sum(e, axis=-1, keepdims=True)
                a_new = alpha * a_old + _dot(e.astype(BF16), kv)
                out.append((m_new, l_new, a_new))
            new.append(tuple(out))
        return tuple(new)

    init = tuple(tuple((jnp.full((Q_BLOCK, 1), NEG_INF, F32), jnp.zeros((Q_BLOCK, 1), F32),
                        jnp.zeros((Q_BLOCK, LANES), F32)) for _ in range(NSA_GROUP)) for _ in kv_heads)
    n_chunks = (t0 + Q_BLOCK + SLC_KEY_CHUNK - 1) // SLC_KEY_CHUNK
    fin = lax.fori_loop(0, n_chunks, chunk, init)

    heads = []
    for h in kv_heads:
        kvw = kvw_ref[pl.ds(w0, band), lane_sl[h]].astype(BF16)
        sm_w = _exp2_softmax(_dot_nt(qhs[h], kvw), rel_w, bias_w, ALIBI2[h], Q_BLOCK)
        o_w = _dot(jnp.concatenate([e for (e, _) in sm_w], axis=0).astype(BF16), kvw)
        for g in range(NSA_GROUP):
            col = h * NSA_GROUP + g
            rows = slice(g * Q_BLOCK, (g + 1) * Q_BLOCK)
            heads.append(gates[:, col:col + 1] * o_cs[h][rows]
                         + (gates[:, NSA_HEADS + col:NSA_HEADS + col + 1] / fin[h][g][1]) * fin[h][g][2]
                         + (gates[:, 2 * NSA_HEADS + col:2 * NSA_HEADS + col + 1] / sm_w[g][1]) * o_w[rows])
    _pack_heads(o_ref, heads)


def _nsa_prompt(u, kc, wov):
    b, t, _ = u.shape
    assert t % Q_BLOCK == 0 and t >= WINDOW + Q_BLOCK and t % SLC_KEY_CHUNK == 0
    n_slc = -(-t // SEL_BLOCK)
    assert n_slc % SUBLANES == 0 and n_slc <= LANES
    return pl.pallas_call(
        functools.partial(_nsa_prompt_body, seq=t),
        out_shape=jax.ShapeDtypeStruct((b, t, NSA_HEADS * HEAD_DIM), F32),
        grid=(b, t // Q_BLOCK),
        in_specs=[
            pl.BlockSpec((None, Q_BLOCK, NSA_HEADS * HEAD_DIM), lambda bi, i: (bi, i, 0)),
            pl.BlockSpec((None, Q_BLOCK, LANES), lambda bi, i: (bi, i, BLK_MISC)),
            pl.BlockSpec((None,) + kc.shape[1:], lambda bi, i: (bi, 0, 0)),
            pl.BlockSpec((None, t, 4 * LANES), lambda bi, i: (bi, 0, BLK_KVS)),
            pl.BlockSpec((None, t, 4 * LANES), lambda bi, i: (bi, 0, BLK_KVW)),
            pl.BlockSpec(wov.shape, lambda bi, i: (0, 0)),
        ],
        out_specs=pl.BlockSpec((None, Q_BLOCK, NSA_HEADS * HEAD_DIM), lambda bi, i: (bi, i, 0)),
        compiler_params=_cparams(("parallel", "arbitrary")),
        name="nsa_prompt",
    )(u, u, kc, u, u, wov)


def _cmp_select_body(q_ref, kc_ref, wov_ref, oc_ref, sel_ref, *, past_len, t_real):
    tp = q_ref.shape[0]
    n_cr = kc_ref.shape[0]
    n_slc = -(-(past_len + t_real) // SEL_BLOCK)
    t_pos = past_len + lax.broadcasted_iota(jnp.int32, (tp, 1), 0)
    for h in range(NSA_KV_HEADS):
        qh = _stack_q(q_ref, h * Q_HEAD_COLS, HEAD_DIM ** -0.5)
        kc = kc_ref[:, h * LANES:(h + 1) * LANES].astype(BF16)
        r_idx = lax.broadcasted_iota(jnp.int32, (1, n_cr), 1)
        dist = t_pos - (r_idx * CMP_STRIDE + (CMP_STRIDE - 1))
        mask = (r_idx >= 1) & (dist >= 0)
        p_c = _softmax_groups(_dot_nt(qh, kc), dist.astype(F32), mask, ALIBI[h], tp)
        oc_ref[h] = _dot(jnp.concatenate(p_c, axis=0).astype(BF16), kc)
        imp = p_c[0] + p_c[1] + p_c[2] + p_c[3]
        score, blk = _block_scores(imp, wov_ref, t_pos, n_slc)
        work = jnp.where(blk < n_slc, score, -3e38)
        blkf = blk.astype(F32)
        picked = jnp.zeros((tp, LANES), jnp.int32)
        lane = lax.broadcasted_iota(jnp.int32, (tp, LANES), 1)
        for k in range(N_SEL):
            m = jnp.max(work, axis=-1, keepdims=True)
            idx = jnp.min(jnp.where(work == m, blkf, 3e38), axis=-1, keepdims=True)
            picked = jnp.where(lane == k, idx.astype(jnp.int32), picked)
            work = jnp.where(blkf == idx, -3e38, work)
        sel_ref[h] = picked


def _cmp_select(u_s, kc, wov, past_len, t_real):
    b, tp, _ = u_s.shape
    assert -(-(past_len + t_real) // SEL_BLOCK) >= N_SEL
    return pl.pallas_call(
        functools.partial(_cmp_select_body, past_len=past_len, t_real=t_real),
        out_shape=(jax.ShapeDtypeStruct((b, NSA_KV_HEADS, NSA_GROUP * tp, LANES), F32),
                   jax.ShapeDtypeStruct((b, NSA_KV_HEADS, tp, LANES), jnp.int32)),
        grid=(b,),
        in_specs=[
            pl.BlockSpec((None, tp, NSA_HEADS * HEAD_DIM), lambda bi: (bi, 0, 0)),
            pl.BlockSpec((None,) + kc.shape[1:], lambda bi: (bi, 0, 0)),
            pl.BlockSpec(wov.shape, lambda bi: (0, 0)),
        ],
        out_specs=(pl.BlockSpec((None, NSA_KV_HEADS, NSA_GROUP * tp, LANES), lambda bi: (bi, 0, 0, 0)),
                   pl.BlockSpec((None, NSA_KV_HEADS, tp, LANES), lambda bi: (bi, 0, 0, 0))),
        compiler_params=_cparams(("parallel",)),
        name="cmp_select",
    )(u_s, kc, wov)


def _slc_sample_body(pages_ref, *refs, past_len):
    del pages_ref
    blk_refs = refs[:N_SEL]
    pos_ref, ok_ref, q_ref, kvn_ref, o_ref = refs[N_SEL:]
    h, t = pl.program_id(1), pl.program_id(2)
    tp = kvn_ref.shape[0]
    rows = NSA_GROUP * tp
    qh = _stack_q(q_ref, 0, HEAD_DIM ** -0.5)
    row = lax.broadcasted_iota(jnp.int32, (rows, 1), 0)
    t_pos = past_len + row % tp

    kv_t = jnp.concatenate([r[...].reshape(2 * HEAD_DIM, PAGE_SIZE) for r in blk_refs], axis=1).astype(BF16)
    d = t_pos - pos_ref[...]
    msk = (ok_ref[...] > 0) & (d >= 0)
    kvn = jnp.concatenate([kvn_ref[...], jnp.zeros((LANES - tp, LANES), F32)], axis=0).astype(BF16)
    idx_n = lax.broadcasted_iota(jnp.int32, (1, LANES), 1)
    dn = t_pos - (past_len + idx_n)
    mskn = (idx_n < tp) & (dn >= 0)
    s_all = _dot(qh, kv_t)
    s_new = _dot_nt(qh, kvn)
    outs = []
    for g in range(NSA_GROUP):
        r0 = slice(g * tp, (g + 1) * tp)
        slope = jnp.where(h == 0, ALIBI[0][g], jnp.where(h == 1, ALIBI[1][g],
                                                         jnp.where(h == 2, ALIBI[2][g], ALIBI[3][g])))
        sg = jnp.where(msk[r0], s_all[r0] - slope * d[r0].astype(F32), NEG_INF)
        sn = jnp.where(mskn[r0], s_new[r0] - slope * dn[r0].astype(F32), NEG_INF)
        m = jnp.maximum(jnp.max(sg, axis=-1, keepdims=True), jnp.max(sn, axis=-1, keepdims=True))
        e = jnp.where(msk[r0], jnp.exp(sg - m), 0.0)
        en = jnp.where(mskn[r0], jnp.exp(sn - m), 0.0)
        l = jnp.sum(e, axis=-1, keepdims=True) + jnp.sum(en, axis=-1, keepdims=True)
        outs.append((_dot_nt(e.astype(BF16), kv_t) + _dot(en.astype(BF16), kvn)) / jnp.maximum(l, 1e-30))
    res = jnp.concatenate(outs, axis=0)

    @pl.when(t == 0)
    def _():
        o_ref[...] = jnp.zeros_like(o_ref)

    o_ref[...] = jnp.where(row % tp == t, res, o_ref[...])


def _slc_sample(u_s, cache, table, sel, layer, past_len, t_real):
    b, tp, _ = u_s.shape
    n_past_blk = past_len // SEL_BLOCK
    per_page = PAGE_SIZE // SEL_BLOCK
    assert past_len % SEL_BLOCK == 0 and t_real <= SEL_BLOCK and PAGE_SIZE % SEL_BLOCK == 0

    page_idx = jnp.minimum(sel, n_past_blk - 1) // per_page
    pages = jnp.take_along_axis(table[:, None, None, :], page_idx.reshape(b, 1, 1, -1), axis=-1).reshape(-1)
    lane = np.arange(N_SEL * PAGE_SIZE)
    blk_l = jnp.take(sel, lane // PAGE_SIZE, axis=-1)
    in_page = jnp.asarray(lane % PAGE_SIZE, jnp.int32)
    pos = ((blk_l // per_page) * PAGE_SIZE + in_page)[:, :, :, None, :]
    ok = ((blk_l < n_past_blk) & (in_page // SEL_BLOCK == blk_l % per_page)).astype(jnp.int32)[:, :, :, None, :]

    def blk_spec(k):
        def imap(bi, h, t, pg):
            return (pg[((bi * NSA_KV_HEADS + h) * t_real + t) * N_SEL + k], layer, h, 0, 0, 0)
        return pl.BlockSpec((None, None, None, 2, HEAD_DIM, PAGE_SIZE), imap)

    lane_spec = pl.BlockSpec((None, None, None, 1, N_SEL * PAGE_SIZE), lambda bi, h, t, pg: (bi, h, t, 0, 0))
    grid_spec = pltpu.PrefetchScalarGridSpec(
        num_scalar_prefetch=1,
        grid=(b, NSA_KV_HEADS, t_real),
        in_specs=[blk_spec(k) for k in range(N_SEL)] + [
            lane_spec, lane_spec,
            pl.BlockSpec((None, tp, Q_HEAD_COLS), lambda bi, h, t, pg: (bi, 0, h)),
            pl.BlockSpec((None, tp, LANES), lambda bi, h, t, pg: (bi, 0, BLK_KVS * 4 + h)),
        ],
        out_specs=pl.BlockSpec((None, None, NSA_GROUP * tp, LANES), lambda bi, h, t, pg: (bi, h, 0, 0)),
    )
    return pl.pallas_call(
        functools.partial(_slc_sample_body, past_len=past_len),
        out_shape=jax.ShapeDtypeStruct((b, NSA_KV_HEADS, NSA_GROUP * tp, LANES), F32),
        grid_spec=grid_spec,
        compiler_params=_cparams(("arbitrary", "arbitrary", "arbitrary")),
        name="slc_sample",
    )(pages, *([cache] * N_SEL), pos, ok, u_s, u_s)


def _win_combine_body(q_ref, misc_ref, win_ref, kvn_ref, oc_ref, os_ref, o_ref, *, past_len):
    tp = q_ref.shape[0]
    wb = win_ref.shape[0]
    t_pos = past_len + lax.broadcasted_iota(jnp.int32, (tp, 1), 0)
    gates = jax.nn.sigmoid(misc_ref[...])
    n_keys = wb + LANES
    idx = lax.broadcasted_iota(jnp.int32, (1, n_keys), 1)
    d = t_pos - (past_len - wb + idx)
    mask = (idx < wb + tp) & (d >= 0) & (d < WINDOW)
    heads = []
    for h in range(NSA_KV_HEADS):
        sl = slice(h * LANES, (h + 1) * LANES)
        qh = _stack_q(q_ref, h * Q_HEAD_COLS, HEAD_DIM ** -0.5)
        kv = jnp.concatenate([win_ref[:, sl], kvn_ref[:, sl], jnp.zeros((LANES - tp, LANES), F32)],
                             axis=0).astype(BF16)
        p_w = _softmax_groups(_dot_nt(qh, kv), d.astype(F32), mask, ALIBI[h], tp)
        o_w = _dot(jnp.concatenate(p_w, axis=0).astype(BF16), kv)
        o_c = oc_ref[h]
        o_s = os_ref[h]
        for g in range(NSA_GROUP):
            col = h * NSA_GROUP + g
            rows = slice(g * tp, (g + 1) * tp)
            heads.append(gates[:, col:col + 1] * o_c[rows]
                         + gates[:, NSA_HEADS + col:NSA_HEADS + col + 1] * o_s[rows]
                         + gates[:, 2 * NSA_HEADS + col:2 * NSA_HEADS + col + 1] * o_w[rows])
    _pack_heads(o_ref, heads)


def _win_combine(u_s, win, o_c, o_s, layer, past_len):
    b, tp, _ = u_s.shape
    wb = win.shape[2]
    return pl.pallas_call(
        functools.partial(_win_combine_body, past_len=past_len),
        out_shape=jax.ShapeDtypeStruct((b, tp, NSA_HEADS * HEAD_DIM), F32),
        grid=(b,),
        in_specs=[
            pl.BlockSpec((None, tp, NSA_HEADS * HEAD_DIM), lambda bi: (bi, 0, 0)),
            pl.BlockSpec((None, tp, LANES), lambda bi: (bi, 0, BLK_MISC)),
            pl.BlockSpec((None, None, wb, 4 * LANES), lambda bi: (bi, layer, 0, 0)),
            pl.BlockSpec((None, tp, 4 * LANES), lambda bi: (bi, 0, BLK_KVW)),
            pl.BlockSpec((None,) + o_c.shape[1:], lambda bi: (bi, 0, 0, 0)),
            pl.BlockSpec((None,) + o_s.shape[1:], lambda bi: (bi, 0, 0, 0)),
        ],
        out_specs=pl.BlockSpec((None, tp, NSA_HEADS * HEAD_DIM), lambda bi: (bi, 0, 0)),
        compiler_params=_cparams(("parallel",)),
        name="win_combine",
    )(u_s, u_s, win, u_s, o_c, o_s)


def _cumsum_rows(x):
    n = x.shape[0]
    row = lax.broadcasted_iota(jnp.int32, x.shape, 0)
    shift = 1
    while shift < n:
        x = x + jnp.where(row >= shift, pltpu.roll(x, shift, 0), 0.0)
        shift *= 2
    return x


def _gla_body(gqk_ref, gv_ref, gog_ref, misc_ref, wdec_ref, bdec_ref, gn_ref, s0_ref,
              o_ref, sout_ref, st_ref, *, c_real, n_sub):
    ci = pl.program_id(1)

    @pl.when(ci == 0)
    def _():
        for h in range(GLA_HEADS):
            st_ref[h] = s0_ref[h].T

    cp = gqk_ref.shape[0] // n_sub
    for j in range(n_sub):
        _gla_chunk(gqk_ref, gv_ref, gog_ref, misc_ref, wdec_ref, bdec_ref, gn_ref, o_ref, st_ref,
                   slice(j * cp, (j + 1) * cp), cp, c_real)

    @pl.when(ci == pl.num_programs(1) - 1)
    def _():
        for h in range(GLA_HEADS):
            sout_ref[h] = st_ref[h].T[:GLA_DK, :]


def _gla_chunk(gqk_ref, gv_ref, gog_ref, misc_ref, wdec_ref, bdec_ref, gn_ref, o_ref, st_ref, rs, cp, c_real):
    k_base = GLA_HEADS * GLA_DK
    sb = min(16, cp)
    row = lax.broadcasted_iota(jnp.int32, (cp, 1), 0)
    live = row < c_real
    x = jnp.dot(misc_ref[rs, :], wdec_ref[...], preferred_element_type=F32, precision=HIGHEST) + bdec_ref[...]
    log_a = (jnp.minimum(x, 0.0) - jnp.log1p(jnp.exp(-jnp.abs(x)))) / GLA_GATE_TAU
    b_all = _cumsum_rows(jnp.where(live, log_a, 0.0))
    for h in range(GLA_HEADS):
        sl = slice(h * LANES, (h + 1) * LANES)
        pair = slice((h // 2) * LANES, (h // 2 + 1) * LANES)
        q = _unpack_head(gqk_ref[rs, pair], h % 2 == 1) * GLA_DK ** -0.5
        k_pair = slice(k_base + (h // 2) * LANES, k_base + (h // 2 + 1) * LANES)
        k = jnp.where(live, _unpack_head(gqk_ref[rs, k_pair], h % 2 == 1), 0.0)
        v = jnp.where(live, gv_ref[rs, sl], 0.0)
        b = b_all[:, sl]
        kb = k.astype(BF16)
        st = st_ref[h]
        attn_rows = []
        for i in range(cp // sb):
            qi = q[i * sb:(i + 1) * sb]
            bi = b[i * sb:(i + 1) * sb]
            sub_row = lax.broadcasted_iota(jnp.int32, (sb, 1), 0)
            ys = []
            for s_loc in range(sb):
                bs = b[i * sb + s_loc:i * sb + s_loc + 1]
                ys.append(qi * jnp.exp(jnp.where(sub_row >= s_loc, bi - bs, NEG_INF)))
            z = _dot_nt(jnp.concatenate(ys, axis=0).astype(BF16), kb)
            lane = lax.broadcasted_iota(jnp.int32, (sb, cp), 1)
            a_i = jnp.zeros((sb, cp), F32)
            for s_loc in range(sb):
                a_i += jnp.where(lane == i * sb + s_loc, z[s_loc * sb:(s_loc + 1) * sb], 0.0)
            if i > 0:
                ref_b = b[i * sb - 1:i * sb]
                qt = qi * jnp.exp(bi - ref_b)
                kt = k * jnp.exp(jnp.where(row < i * sb, ref_b - b, NEG_INF))
                a_i += _dot_nt(qt.astype(BF16), kt.astype(BF16))
            attn_rows.append(a_i)
        attn = jnp.concatenate(attn_rows, axis=0) if len(attn_rows) > 1 else attn_rows[0]
        o = _dot(attn.astype(BF16), v.astype(BF16))
        o += _dot_nt((q * jnp.exp(b)).astype(BF16), st.astype(BF16))
        b_last = b[c_real - 1:c_real]
        kd = k * jnp.exp(b_last - b)
        st_ref[h] = jnp.exp(b_last) * st + _dot(v.T.astype(BF16), kd.astype(BF16))
        o_ref[rs, sl] = _rms(o, gn_ref[...]) * jax.nn.silu(gog_ref[rs, sl])


def _recurrent_body(gqk_ref, gv_ref, gog_ref, misc_ref, wdec_ref, bdec_ref, gn_ref, s0_ref,
                    rq_ref, rk_ref, rv_ref, rg_ref, rn_ref, r0_ref,
                    og_ref, sout_ref, or_ref, rout_ref, st_ref, rt_ref, *, c_real, n_sub):
    _gla_body(gqk_ref, gv_ref, gog_ref, misc_ref, wdec_ref, bdec_ref, gn_ref, s0_ref,
              og_ref, sout_ref, st_ref, c_real=c_real, n_sub=n_sub)
    ret_real = c_real if n_sub == 1 else rq_ref.shape[0]
    _ret_body(rq_ref, rk_ref, rv_ref, rg_ref, rn_ref, r0_ref, or_ref, rout_ref, rt_ref, c_real=ret_real)


def _recurrent(u, s0, r0, w_dec, b_dec, gla_norm, ret_norm, layer, gla_chunk, c_real, n_sub=1):
    b, t, _ = u.shape
    cp = gla_chunk * n_sub
    assert t % cp == 0 and (n_sub == 1 or c_real == gla_chunk)
    blk = lambda idx: pl.BlockSpec((None, cp, 4 * LANES), lambda bi, i: (bi, i, idx))
    per_b = lambda shape: pl.BlockSpec((None,) + shape, lambda bi, i: (bi, 0, 0, 0))
    per_layer = lambda shape: pl.BlockSpec((None,) + shape, lambda bi, i: (layer, 0, 0))
    return pl.pallas_call(
        functools.partial(_recurrent_body, c_real=c_real, n_sub=n_sub),
        out_shape=(jax.ShapeDtypeStruct((b, t, GLA_HEADS * GLA_DV), F32),
                   jax.ShapeDtypeStruct((b, GLA_HEADS, GLA_DK, GLA_DV), F32),
                   jax.ShapeDtypeStruct((b, t, RET_HEADS * RET_DV), F32),
                   jax.ShapeDtypeStruct((b, RET_HEADS, RET_DK, RET_DV), F32)),
        grid=(b, t // cp),
        in_specs=[
            blk(BLK_GQK), blk(BLK_GV), blk(BLK_GOG),
            pl.BlockSpec((None, cp, LANES), lambda bi, i: (bi, i, BLK_MISC)),
            per_layer((LANES, 4 * LANES)), per_layer((1, 4 * LANES)), per_layer((1, GLA_DV)),
            per_b((GLA_HEADS, LANES, GLA_DV)),
            blk(BLK_RQ), blk(BLK_RK), blk(BLK_RV), blk(BLK_RG),
            per_layer((1, RET_HEADS * RET_DV)),
            per_b((RET_HEADS, RET_DK, RET_DV)),
        ],
        out_specs=(pl.BlockSpec((None, cp, GLA_HEADS * GLA_DV), lambda bi, i: (bi, i, 0)),
                   per_b((GLA_HEADS, GLA_DK, GLA_DV)),
                   pl.BlockSpec((None, cp, RET_HEADS * RET_DV), lambda bi, i: (bi, i, 0)),
                   per_b((RET_HEADS, RET_DK, RET_DV))),
        scratch_shapes=[pltpu.VMEM((GLA_HEADS, GLA_DV, LANES), F32),
                        pltpu.VMEM((RET_HEADS, RET_DV, RET_DK), F32)],
        compiler_params=_cparams(("parallel", "arbitrary")),
        name="recurrent",
    )(u, u, u, u, w_dec, b_dec, gla_norm, s0, u, u, u, u, ret_norm, r0)


def _ret_body(rq_ref, rk_ref, rv_ref, rg_ref, gn_ref, r0_ref, o_ref, rout_ref, rt_ref, *, c_real):
    ci = pl.program_id(1)
    cp = rq_ref.shape[0]

    @pl.when(ci == 0)
    def _():
        for h in range(RET_HEADS):
            rt_ref[h] = r0_ref[h].T

    row = lax.broadcasted_iota(jnp.int32, (cp, 1), 0)
    live = row < c_real
    rowf = row.astype(F32)
    rel = rowf - lax.broadcasted_iota(jnp.int32, (1, cp), 1).astype(F32)
    for h in range(RET_HEADS):
        sl = slice(h * LANES, (h + 1) * LANES)
        lg = RET_LOG_GAMMA[h]
        q = rq_ref[:, sl]
        k = jnp.where(live, rk_ref[:, sl] * RET_DK ** -0.5, 0.0)
        v = jnp.where(live, rv_ref[:, sl], 0.0)
        decay = jnp.where(rel >= 0, jnp.exp(jnp.maximum(rel, 0.0) * lg), 0.0)
        attn = _dot_nt(q.astype(BF16), k.astype(BF16)) * decay
        o = _dot(attn.astype(BF16), v.astype(BF16))
        rt = rt_ref[h]
        o += _dot_nt((q * jnp.exp((rowf + 1.0) * lg)).astype(BF16), rt.astype(BF16))
        kd = k * jnp.exp((c_real - 1.0 - rowf) * lg)
        rt_ref[h] = math.exp(c_real * lg) * rt + _dot(v.T.astype(BF16), kd.astype(BF16))
        mu = jnp.mean(o, axis=-1, keepdims=True)
        var = jnp.mean(jnp.square(o - mu), axis=-1, keepdims=True)
        o_ref[:, sl] = (o - mu) * lax.rsqrt(var + EPS) * gn_ref[:, sl] * jax.nn.silu(rg_ref[:, sl])

    @pl.when(ci == pl.num_programs(1) - 1)
    def _():
        for h in range(RET_HEADS):
            rout_ref[h] = rt_ref[h].T


def _pad_heads(w, n_heads):
    lead = w.shape[:-1]
    w = w.reshape(lead + (n_heads, HEAD_DIM))
    w = jnp.pad(w, [(0, 0)] * len(lead) + [(0, 0), (0, LANES - HEAD_DIM)])
    return w.reshape(lead + (n_heads * LANES,))


def _prep_w_in(w_in):
    cuts = np.cumsum((0,) + IN_SIZES)
    nq, kvc, kvs, kvw, ng, gq, gk, gv, glr, gog, rq, rk, rv, rg = [
        w_in[..., cuts[i]:cuts[i + 1]] for i in range(len(IN_SIZES))]
    misc = jnp.concatenate([ng, glr], axis=-1)
    misc = jnp.pad(misc, [(0, 0), (0, 0), (0, LANES - misc.shape[-1])])
    w = jnp.concatenate([nq, kvc, kvs, kvw, gq, gk, gv, gog, rq, rk, rv, rg, misc], axis=-1)
    assert w.shape[-1] == (BLK_MISC + 1) * LANES == U_WIDTH
    return w.astype(BF16)


def _prep_cmp_weights(w_cmp1, w_cmp2):
    depth = w_cmp1.shape[0]
    half = CMP_STRIDE * HEAD_DIM
    w1 = w_cmp1.reshape(depth, 2, 2, CMP_STRIDE, HEAD_DIM, CMP_HIDDEN)
    w1 = w1.transpose(0, 3, 1, 4, 2, 5)
    z = jnp.zeros_like(w1[:, :, 0])
    k_rows = jnp.stack([w1[:, :, 0], z], axis=4)
    v_rows = jnp.stack([z, w1[:, :, 1]], axis=4)
    wblk = jnp.concatenate([k_rows, v_rows], axis=2)
    wblk = wblk.reshape(depth, CMP_STRIDE // 2, 4 * HEAD_DIM, 4 * CMP_HIDDEN).astype(BF16)
    del half
    z2 = jnp.zeros_like(w_cmp2[:, 0])
    w2blk = jnp.concatenate([jnp.concatenate([w_cmp2[:, 0], z2], axis=-1),
                             jnp.concatenate([z2, w_cmp2[:, 1]], axis=-1)], axis=1).astype(BF16)
    return wblk, w2blk


def _overlap_matrix(n_rows, n_slc, n_cols):
    ratio = SEL_BLOCK // CMP_STRIDE
    w = np.zeros((n_rows, n_cols), np.float32)
    for j in range(n_slc):
        for k, wk in enumerate(OVERLAP_W):
            if ratio * j + k < n_rows:
                w[ratio * j + k, j] = wk
    return jnp.asarray(w)


def kernel(x_prompt, x_sample, cache_cmp_kv, cache_slc_kv, cache_win_kv, state_gla, state_ret, page_table,
           p_prompt, p_sample, norm_pre, norm_post, w_ffn_gate, w_ffn_up, w_ffn_down, w_in, w_out,
           w_cmp1, w_cmp2, cmp_pos, nsa_norm, w_gla_decay, b_gla_decay, gla_norm, ret_norm, w_ple, w_ple_gate):
    depth = w_in.shape[0]
    bp, seq, d_model = x_prompt.shape
    bs, t_real, _ = x_sample.shape
    n_pool = cache_cmp_kv.shape[0]
    n_pages = page_table.shape[1]
    past_len = n_pages * PAGE_SIZE
    tp = -(-t_real // SUBLANES) * SUBLANES
    kvw = NSA_KV_HEADS * 2 * HEAD_DIM

    wg = w_ffn_gate.astype(BF16)
    wu = w_ffn_up.astype(BF16)
    wd = w_ffn_down.astype(BF16)
    w_in_r = _prep_w_in(w_in)
    w_out_b = w_out.astype(BF16)
    w_ple_b = w_ple.astype(BF16)
    w_gate_b = w_ple_gate.astype(BF16)
    wblk, w2blk = _prep_cmp_weights(w_cmp1, w_cmp2)
    g_pre = norm_pre.reshape(depth, 3, 1, d_model)
    g_post = norm_post.reshape(depth, 3, 1, d_model)
    nsa_g = nsa_norm.reshape(depth, 1, -1)
    gla_g = gla_norm.reshape(depth, 1, -1)
    ret_g = ret_norm.reshape(depth, 1, -1)
    w_dec = jnp.pad(_pad_heads(w_gla_decay, GLA_HEADS),
                    [(0, 0), (MISC_GLR, LANES - MISC_GLR - GLA_GATE_RANK), (0, 0)])
    b_dec = _pad_heads(b_gla_decay, GLA_HEADS).reshape(depth, 1, -1)
    pos_flat = cmp_pos.reshape(depth, 2, 1, CMP_BLOCK * HEAD_DIM)

    cmp_pages = cache_cmp_kv.transpose(0, 1, 3, 4, 5, 2)
    slc_rows = cache_slc_kv.transpose(0, 1, 3, 4, 5, 2)
    win_rows = cache_win_kv.reshape(bs, depth, -1, kvw)
    wb = win_rows.shape[2]

    n_cr_p = seq // CMP_STRIDE
    wov_p = _overlap_matrix(n_cr_p, -(-seq // SEL_BLOCK), LANES).T
    n_cr_s = past_len // CMP_STRIDE
    n_slc_s = -(-(past_len + t_real) // SEL_BLOCK)
    wov_s = _overlap_matrix(n_cr_s, n_slc_s, -(-n_slc_s // LANES) * LANES)
    table_p = jnp.arange(bp * (seq // PAGE_SIZE), dtype=jnp.int32).reshape(bp, seq // PAGE_SIZE)

    xp = x_prompt.reshape(bp * seq, d_model)
    xs = jnp.pad(x_sample, ((0, 0), (0, tp - t_real), (0, 0))).reshape(bs * tp, d_model)
    pp = p_prompt.reshape(depth, bp * seq, -1)
    ps = jnp.pad(p_sample, ((0, 0), (0, 0), (0, tp - t_real), (0, 0))).reshape(depth, bs * tp, -1)
    s0_gla_p = jnp.zeros((bp, GLA_HEADS, LANES, GLA_DV), F32)
    s0_ret_p = jnp.zeros((bp, RET_HEADS, RET_DK, RET_DV), F32)
    s0_gla_s = jnp.pad(state_gla, ((0, 0), (0, 0), (0, 0), (0, LANES - GLA_DK), (0, 0)))

    tm_p = 1024
    tm_s = bs * tp
    st_p = [[] for _ in range(5)]
    st_s = [[] for _ in range(5)]
    for l in range(depth):
        bias = _cmp_bias(pos_flat, w_cmp1, l).reshape(1, 2 * CMP_HIDDEN)

        xp = _ffn(xp, g_pre, g_post, wg, wu, wd, l, 0, 0, tm_p // 2)
        u2 = _proj_in(xp, g_pre, w_in_r, l, tm_p)
        u = u2.reshape(bp, seq, U_WIDTH)
        kv_c = u[..., BLK_KVC * kvw:(BLK_KVC + 1) * kvw]
        kv_s = u[..., BLK_KVS * kvw:(BLK_KVS + 1) * kvw]
        kv_w = u[..., BLK_KVW * kvw:(BLK_KVW + 1) * kvw]
        kc_pages = kv_c.reshape(bp * (seq // PAGE_SIZE), 1, PAGE_SIZE, NSA_KV_HEADS, LANES).transpose(0, 1, 3, 2, 4)
        kc = _compress(kc_pages, table_p, 0,
                       wblk[l], bias, w2blk[l])
        o_nsa = _nsa_prompt(u, kc, wov_p)
        gla_out, s_g, ret_out, s_r = _recurrent(u, s0_gla_p, s0_ret_p, w_dec, b_dec, gla_g, ret_g, l,
                                                GLA_CHUNK, GLA_CHUNK, n_sub=2)
        xp = _proj_out(xp, o_nsa.reshape(bp * seq, -1), gla_out.reshape(bp * seq, -1),
                       ret_out.reshape(bp * seq, -1), nsa_g, w_out_b, g_post, l, tm_p // 2)
        xp = _ffn(xp, g_pre, g_post, wg, wu, wd, l, 1, 2, tm_p // 2)
        xp = _ple(xp, pp, w_gate_b, w_ple_b, l, tm_p // 2)
        win_keep = min(WINDOW, seq)
        for j, a in enumerate((kv_c, kv_s, kv_w[:, seq - win_keep:], s_g, s_r)):
            st_p[j].append(a)

        xs = _ffn(xs, g_pre, g_post, wg, wu, wd, l, 0, 0, tm_s)
        us = _proj_in(xs, g_pre, w_in_r, l, tm_s).reshape(bs, tp, U_WIDTH)
        kv_c = us[:, :t_real, BLK_KVC * kvw:(BLK_KVC + 1) * kvw]
        kv_s = us[:, :t_real, BLK_KVS * kvw:(BLK_KVS + 1) * kvw]
        kv_w = us[:, :t_real, BLK_KVW * kvw:(BLK_KVW + 1) * kvw]
        kc = _compress(cmp_pages, page_table, l, wblk[l], bias, w2blk[l])
        o_c, sel = _cmp_select(us, kc, wov_s, past_len, t_real)
        o_s = _slc_sample(us, slc_rows, page_table, sel[:, :, :t_real, :N_SEL], l, past_len, t_real)
        o_nsa = _win_combine(us, win_rows, o_c, o_s, l, past_len)
        gla_out, s_g, ret_out, s_r = _recurrent(us, s0_gla_s[:, l], state_ret[:, l], w_dec, b_dec, gla_g, ret_g, l,
                                                tp, t_real)
        xs = _proj_out(xs, o_nsa.reshape(bs * tp, -1), gla_out.reshape(bs * tp, -1),
                       ret_out.reshape(bs * tp, -1), nsa_g, w_out_b, g_post, l, tm_s)
        xs = _ffn(xs, g_pre, g_post, wg, wu, wd, l, 1, 2, tm_s)
        xs = _ple(xs, ps, w_gate_b, w_ple_b, l, tm_s)
        new_win = jnp.concatenate([win_rows[:, l], kv_w], axis=1)[:, t_real:]
        for j, a in enumerate((kv_c, kv_s, new_win, s_g, s_r)):
            st_s[j].append(a)

    def kv_stack(parts):
        a = jnp.stack(parts, axis=1)
        return a.reshape(a.shape[:3] + (NSA_KV_HEADS, 2, HEAD_DIM))

    y_p = xp.reshape(bp, seq, d_model)
    y_s = xs.reshape(bs, tp, d_model)[:, :t_real]
    return (y_p, y_s,
            kv_stack(st_p[0]), kv_stack(st_p[1]), kv_stack(st_p[2]),
            jnp.stack(st_p[3], axis=1), jnp.stack(st_p[4], axis=1),
            kv_stack(st_s[0]), kv_stack(st_s[1]), kv_stack(st_s[2]),
            jnp.stack(st_s[3], axis=1), jnp.stack(st_s[4], axis=1))
```

```python
import functools
import math

import numpy as np
import jax
import jax.numpy as jnp
from jax import lax
from jax.experimental import pallas as pl
from jax.experimental.pallas import tpu as pltpu

F32 = jnp.float32
BF16 = jnp.bfloat16
HIGHEST = lax.Precision.HIGHEST

HEAD_DIM = 64
NSA_HEADS = 16
NSA_KV_HEADS = 4
NSA_GROUP = 4
CMP_BLOCK = 32
CMP_STRIDE = 16
CMP_HIDDEN = 128
SEL_BLOCK = 64
N_SEL = 16
WINDOW = 512
Q_BLOCK = 128
OVERLAP_W = (0.5, 1.0, 1.0, 1.0, 0.5)
GLA_HEADS = 4
GLA_DK = 64
GLA_DV = 128
GLA_GATE_RANK = 16
GLA_GATE_TAU = 16.0
GLA_CHUNK = 64
RET_HEADS = 4
RET_DK = 128
RET_DV = 128
RET_CHUNK = 64
PAGE_SIZE = 128
NEG_INF = -1e30
FORCE = 1e9
EPS = 1e-6
IN_SIZES = (1024, 512, 512, 512, 48, 256, 256, 512, 16, 512, 512, 512, 512, 512)

LANES = 128
SUBLANES = 8
VMEM_LIMIT = 56 * 1024 * 1024

MXU_COLS = 256
U_WIDTH = 6400
BLK_KVC, BLK_KVS, BLK_KVW = 2, 3, 4
BLK_GQK, BLK_GV, BLK_GOG = 5, 6, 7
BLK_RQ, BLK_RK, BLK_RV, BLK_RG = 8, 9, 10, 11
BLK_MISC = 48
MISC_GLR = 48
Q_HEAD_COLS = NSA_GROUP * HEAD_DIM

ALIBI = [[2.0 ** (-8.0 * (h * NSA_GROUP + g + 1) / NSA_HEADS) for g in range(NSA_GROUP)]
         for h in range(NSA_KV_HEADS)]
RET_LOG_GAMMA = [math.log1p(-(2.0 ** (-5.0 - h))) for h in range(RET_HEADS)]


def _cparams(sem):
    return pltpu.CompilerParams(dimension_semantics=sem, vmem_limit_bytes=VMEM_LIMIT)


def _rms(x, g=None):
    y = x * lax.rsqrt(jnp.mean(x * x, axis=-1, keepdims=True) + EPS)
    return y if g is None else y * g


def _dot(a, b):
    return jnp.dot(a, b, preferred_element_type=F32)


def _dot_nt(a, b):
    return lax.dot_general(a, b, (((1,), (1,)), ((), ())), preferred_element_type=F32)


def _ffn_body(x_ref, gpre_ref, wg_ref, wu_ref, wd_ref, gpost_ref, o_ref, xn_ref, acc_ref):
    f = pl.program_id(1)

    @pl.when(f == 0)
    def _():
        xn_ref[...] = _rms(x_ref[...], gpre_ref[...]).astype(BF16)
        acc_ref[...] = jnp.zeros_like(acc_ref)

    xn = xn_ref[...]
    h = jax.nn.silu(_dot(xn, wg_ref[...])) * _dot(xn, wu_ref[...])
    acc_ref[...] += _dot(h.astype(BF16), wd_ref[...])

    @pl.when(f == pl.num_programs(1) - 1)
    def _():
        o_ref[...] = x_ref[...] + 0.5 * _rms(acc_ref[...], gpost_ref[...])


def _ffn(x, g_pre, g_post, wg, wu, wd, layer, which, norm_idx, tm, tf=512):
    m, d = x.shape
    ff = wg.shape[-1]
    assert m % tm == 0 and ff % tf == 0
    return pl.pallas_call(
        _ffn_body,
        out_shape=jax.ShapeDtypeStruct((m, d), F32),
        grid=(m // tm, ff // tf),
        in_specs=[
            pl.BlockSpec((tm, d), lambda i, f: (i, 0)),
            pl.BlockSpec((None, None, 1, d), lambda i, f: (layer, norm_idx, 0, 0)),
            pl.BlockSpec((None, None, d, tf), lambda i, f: (layer, which, 0, f)),
            pl.BlockSpec((None, None, d, tf), lambda i, f: (layer, which, 0, f)),
            pl.BlockSpec((None, None, tf, d), lambda i, f: (layer, which, f, 0)),
            pl.BlockSpec((None, None, 1, d), lambda i, f: (layer, norm_idx, 0, 0)),
        ],
        out_specs=pl.BlockSpec((tm, d), lambda i, f: (i, 0)),
        scratch_shapes=[pltpu.VMEM((tm, d), BF16), pltpu.VMEM((tm, d), F32)],
        compiler_params=_cparams(("parallel", "arbitrary")),
        name="ffn",
    )(x, g_pre, wg, wu, wd, g_post)


def _proj_in_body(x_ref, g_ref, w_ref, o_ref, xn_ref):
    @pl.when(pl.program_id(1) == 0)
    def _():
        xn_ref[...] = _rms(x_ref[...], g_ref[...]).astype(BF16)

    o_ref[...] = _dot(xn_ref[...], w_ref[...])


def _proj_in(x, g_pre, w_in, layer, tm, tn=5 * MXU_COLS):
    m, d = x.shape
    n = w_in.shape[-1]
    return pl.pallas_call(
        _proj_in_body,
        out_shape=jax.ShapeDtypeStruct((m, n), F32),
        grid=(m // tm, n // tn),
        in_specs=[
            pl.BlockSpec((tm, d), lambda i, j: (i, 0)),
            pl.BlockSpec((None, None, 1, d), lambda i, j: (layer, 1, 0, 0)),
            pl.BlockSpec((None, d, tn), lambda i, j: (layer, 0, j)),
        ],
        out_specs=pl.BlockSpec((tm, tn), lambda i, j: (i, j)),
        scratch_shapes=[pltpu.VMEM((tm, d), BF16)],
        compiler_params=_cparams(("parallel", "arbitrary")),
        name="proj_in",
    )(x, g_pre, w_in)


def _proj_out_body(x_ref, nsa_ref, gla_ref, ret_ref, gn_ref, w_ref, gpost_ref, o_ref):
    nsa_w = nsa_ref.shape[-1]
    gla_w = gla_ref.shape[-1]
    nsa = _rms(nsa_ref[...], gn_ref[...]).astype(BF16)
    y = _dot(nsa, w_ref[0:nsa_w, :])
    y += _dot(gla_ref[...].astype(BF16), w_ref[nsa_w:nsa_w + gla_w, :])
    y += _dot(ret_ref[...].astype(BF16), w_ref[nsa_w + gla_w:, :])
    o_ref[...] = x_ref[...] + _rms(y, gpost_ref[...])


def _proj_out(x, o_nsa, gla_out, ret_out, nsa_norm, w_out, g_post, layer, tm):
    m, d = x.shape
    row = lambda a: pl.BlockSpec((tm, a.shape[-1]), lambda i: (i, 0))
    return pl.pallas_call(
        _proj_out_body,
        out_shape=jax.ShapeDtypeStruct((m, d), F32),
        grid=(m // tm,),
        in_specs=[
            row(x), row(o_nsa), row(gla_out), row(ret_out),
            pl.BlockSpec((None, 1, o_nsa.shape[-1]), lambda i: (layer, 0, 0)),
            pl.BlockSpec((None, w_out.shape[1], d), lambda i: (layer, 0, 0)),
            pl.BlockSpec((None, None, 1, d), lambda i: (layer, 1, 0, 0)),
        ],
        out_specs=row(x),
        compiler_params=_cparams(("parallel",)),
        name="proj_out",
    )(x, o_nsa, gla_out, ret_out, nsa_norm, w_out, g_post)


def _ple_body(x_ref, p_ref, wg_ref, wp_ref, o_ref):
    x = x_ref[...]
    gate = jax.nn.sigmoid(_dot(_rms(x).astype(BF16), wg_ref[...]))
    o_ref[...] = x + gate * _dot(p_ref[...].astype(BF16), wp_ref[...])


def _ple(x, p, w_gate, w_ple, layer, tm):
    m, d = x.shape
    return pl.pallas_call(
        _ple_body,
        out_shape=jax.ShapeDtypeStruct((m, d), F32),
        grid=(m // tm,),
        in_specs=[
            pl.BlockSpec((tm, d), lambda i: (i, 0)),
            pl.BlockSpec((None, tm, p.shape[-1]), lambda i: (layer, i, 0)),
            pl.BlockSpec((None, d, d), lambda i: (layer, 0, 0)),
            pl.BlockSpec((None, p.shape[-1], d), lambda i: (layer, 0, 0)),
        ],
        out_specs=pl.BlockSpec((tm, d), lambda i: (i, 0)),
        compiler_params=_cparams(("parallel",)),
        name="ple",
    )(x, p, w_gate, w_ple)


COMPRESS_PAGE_STEPS = (32, 16)
CHUNKS_PER_PAGE = PAGE_SIZE // CMP_STRIDE
KV_ROW = NSA_KV_HEADS * 2 * HEAD_DIM


def _cmp_bias_body(pos_ref, w1_ref, o_ref):
    o_ref[...] = jnp.dot(pos_ref[...], w1_ref[...], preferred_element_type=F32, precision=HIGHEST)


def _cmp_bias(cmp_pos, w_cmp1, layer):
    kdim = w_cmp1.shape[2]
    return pl.pallas_call(
        _cmp_bias_body,
        out_shape=jax.ShapeDtypeStruct((2, 1, CMP_HIDDEN), F32),
        grid=(2,),
        in_specs=[pl.BlockSpec((None, None, 1, kdim), lambda c: (layer, c, 0, 0)),
                  pl.BlockSpec((None, None, kdim, CMP_HIDDEN), lambda c: (layer, c, 0, 0))],
        out_specs=pl.BlockSpec((None, 1, CMP_HIDDEN), lambda c: (c, 0, 0)),
        compiler_params=_cparams(("arbitrary",)),
        name="cmp_bias",
    )(cmp_pos, w_cmp1)


def _compress_body(pt_ref, *refs, feature_major, pps):
    del pt_ref
    page_refs = refs[:pps]
    wblk_ref, bias_ref, w2_ref, o_ref, prev_ref = refs[pps:pps + 5]
    rows = pps * CHUNKS_PER_PAGE
    half = 2 * CMP_HIDDEN

    @pl.when(pl.program_id(1) == 0)
    def _():
        prev_ref[...] = jnp.zeros_like(prev_ref)

    first_row = lax.broadcasted_iota(jnp.int32, (rows, half), 0) == 0

    if feature_major:
        tok_ref = refs[pps + 5]
        for k, pr in enumerate(page_refs):
            for h in range(NSA_KV_HEADS):
                tok_ref[k * NSA_KV_HEADS + h] = pr[h].reshape(LANES, PAGE_SIZE).T

        def token_rows(t, h):
            return jnp.concatenate([tok_ref[k * NSA_KV_HEADS + h, pl.ds(t, CHUNKS_PER_PAGE, stride=CMP_STRIDE), :]
                                    for k in range(pps)], axis=0)
    else:
        def token_rows(t, h):
            return jnp.concatenate([pr[h, pl.ds(t, CHUNKS_PER_PAGE, stride=CMP_STRIDE), :]
                                    for pr in page_refs], axis=0)

    for h in range(NSA_KV_HEADS):
        acc = jnp.zeros((rows, 2 * half), F32)
        for tt in range(CMP_STRIDE // 2):
            lhs = jnp.concatenate([token_rows(2 * tt, h), token_rows(2 * tt + 1, h)], axis=1).astype(BF16)
            acc += _dot(lhs, wblk_ref[tt])
        first = acc[:, :half]
        second = acc[:, half:]
        carry = prev_ref[h][SUBLANES - 1:SUBLANES, :]
        shifted = jnp.where(first_row, carry, pltpu.roll(first, 1, 0))
        prev_ref[h] = first[rows - SUBLANES:, :]
        hidden = jax.nn.gelu(shifted + second + bias_ref[...])
        o_ref[:, h * LANES:(h + 1) * LANES] = _dot(hidden.astype(BF16), w2_ref[...])


def _compress(pages, table, layer, wblk, bias, w2blk):
    b, n_pages = table.shape
    pps = next(p for p in COMPRESS_PAGE_STEPS if n_pages % p == 0)
    rows = pps * CHUNKS_PER_PAGE
    feature_major = pages.ndim == 6
    page_block = (None, None) + pages.shape[2:]
    zeros = (0,) * (pages.ndim - 2)

    def page_spec(k):
        return pl.BlockSpec(page_block, lambda bi, i, pt: (pt[bi, i * pps + k], layer) + zeros)

    scratch = [pltpu.VMEM((NSA_KV_HEADS, SUBLANES, 2 * CMP_HIDDEN), F32)]
    if feature_major:
        scratch.append(pltpu.VMEM((pps * NSA_KV_HEADS, PAGE_SIZE, LANES), F32))

    grid_spec = pltpu.PrefetchScalarGridSpec(
        num_scalar_prefetch=1,
        grid=(b, n_pages // pps),
        in_specs=[page_spec(k) for k in range(pps)] + [
            pl.BlockSpec(wblk.shape, lambda bi, i, pt: (0, 0, 0)),
            pl.BlockSpec(bias.shape, lambda bi, i, pt: (0, 0)),
            pl.BlockSpec(w2blk.shape, lambda bi, i, pt: (0, 0)),
        ],
        out_specs=pl.BlockSpec((None, rows, NSA_KV_HEADS * LANES), lambda bi, i, pt: (bi, i, 0)),
        scratch_shapes=scratch,
    )
    return pl.pallas_call(
        functools.partial(_compress_body, feature_major=feature_major, pps=pps),
        out_shape=jax.ShapeDtypeStruct((b, n_pages * CHUNKS_PER_PAGE, NSA_KV_HEADS * LANES), F32),
        grid_spec=grid_spec,
        compiler_params=_cparams(("arbitrary", "arbitrary")),
        name="compress",
    )(table, *([pages] * pps), wblk, bias, w2blk)


def _softmax_groups(s, distf, mask, slopes, rows):
    parts = []
    for g in range(NSA_GROUP):
        sg = s[g * rows:(g + 1) * rows] - slopes[g] * distf
        sg = jnp.where(mask, sg, NEG_INF)
        m = jnp.max(sg, axis=-1, keepdims=True)
        e = jnp.where(mask, jnp.exp(sg - m), 0.0)
        parts.append(e / jnp.maximum(jnp.sum(e, axis=-1, keepdims=True), 1e-30))
    return parts


def _unpack_head(pair, odd):
    lane = lax.broadcasted_iota(jnp.int32, pair.shape, 1)
    return jnp.where(lane < HEAD_DIM, pltpu.roll(pair, HEAD_DIM, 1) if odd else pair, 0.0)


def _stack_q(q_ref, base, scale):
    parts = []
    for g in range(NSA_GROUP):
        c0 = base + (g // 2) * LANES
        parts.append(_unpack_head(q_ref[:, c0:c0 + LANES], g % 2 == 1))
    return (jnp.concatenate(parts, axis=0) * scale).astype(BF16)


def _block_scores(imp, wov_ref, t_pos, n_slc):
    score = jnp.dot(imp, wov_ref[...], preferred_element_type=F32, precision=HIGHEST)
    blk = lax.broadcasted_iota(jnp.int32, score.shape, 1)
    cur = t_pos // SEL_BLOCK
    valid = blk <= cur
    forced = valid & ((blk == 0) | (blk == cur) | (blk == cur - 1))
    score = jnp.where(forced, FORCE, jnp.where(valid, score, -FORCE))
    return score, blk


def _pack_heads(o_ref, heads):
    lane = lax.broadcasted_iota(jnp.int32, heads[0].shape, 1)
    for p in range(NSA_HEADS // 2):
        even = pltpu.roll(heads[2 * p], HEAD_DIM, 1)
        o_ref[:, p * LANES:(p + 1) * LANES] = jnp.where(lane < HEAD_DIM, even, heads[2 * p + 1])


SLC_KEY_CHUNK = 512
LOG2E = 1.4426950408889634
MASK_BIG = 2.0 ** 100
ALIBI2 = [[s * LOG2E for s in row] for row in ALIBI]


def _exp2_softmax(s2, rel_row, bias, slopes2, rows):
    out = []
    for g in range(NSA_GROUP):
        lg = s2[g * rows:(g + 1) * rows] + (slopes2[g] * rel_row + bias)
        e = jnp.exp2(lg - jnp.max(lg, axis=-1, keepdims=True))
        out.append((e, jnp.sum(e, axis=-1, keepdims=True)))
    return out


def _nsa_prompt_body(q_ref, misc_ref, kc_ref, kvs_ref, kvw_ref, wovt_ref, o_ref, *, seq):
    qb = pl.program_id(1)
    t0 = qb * Q_BLOCK
    t0f = t0.astype(F32)
    n_slc = -(-seq // SEL_BLOCK)
    n_sel = min(N_SEL, n_slc)
    n_cr = kc_ref.shape[0]
    t_col = (t0 + lax.broadcasted_iota(jnp.int32, (Q_BLOCK, 1), 0)).astype(F32)
    gates = jax.nn.sigmoid(misc_ref[...])

    r_idx = lax.broadcasted_iota(jnp.int32, (1, n_cr), 1)
    end_c = (r_idx * CMP_STRIDE + (CMP_STRIDE - 1)).astype(F32)
    bias_c = jnp.minimum(t_col - end_c, 0.0) * MASK_BIG + jnp.where(r_idx >= 1, 0.0, -MASK_BIG)
    rel_c = end_c - t0f
    row_live = jnp.where(t_col >= CMP_BLOCK - 1, 1.0, 0.0)
    band = WINDOW + Q_BLOCK
    w0 = pl.multiple_of(jnp.maximum(t0 - WINDOW, 0), Q_BLOCK)
    pos_w = (w0 + lax.broadcasted_iota(jnp.int32, (1, band), 1)).astype(F32)
    d_w = t_col - pos_w
    bias_w = (jnp.minimum(d_w, 0.0) + jnp.minimum((WINDOW - 1.0) - d_w, 0.0)) * MASK_BIG
    rel_w = pos_w - t0f

    blk_t = lax.broadcasted_iota(jnp.int32, (n_slc, Q_BLOCK), 0)
    cur_t = (t0 + lax.broadcasted_iota(jnp.int32, (1, Q_BLOCK), 1)) // SEL_BLOCK
    valid_t = blk_t <= cur_t
    forced_t = valid_t & ((blk_t == 0) | (blk_t == cur_t) | (blk_t == cur_t - 1))
    tm1 = t_col - 1.0

    kv_heads = range(NSA_KV_HEADS)
    lane_sl = [slice(h * LANES, (h + 1) * LANES) for h in kv_heads]
    qhs = [_stack_q(q_ref, h * Q_HEAD_COLS, HEAD_DIM ** -0.5 * LOG2E) for h in kv_heads]

    o_cs, scores = [], []
    for h in kv_heads:
        kc = kc_ref[:, lane_sl[h]].astype(BF16)
        sm_c = _exp2_softmax(_dot_nt(qhs[h], kc), rel_c, bias_c, ALIBI2[h], Q_BLOCK)
        p_c = [e * (row_live / l) for (e, l) in sm_c]
        o_cs.append(_dot(jnp.concatenate(p_c, axis=0).astype(BF16), kc))
        imp = p_c[0] + p_c[1] + p_c[2] + p_c[3]
        score_t = lax.dot_general(wovt_ref[...], imp, (((1,), (1,)), ((), ())),
                                  preferred_element_type=F32, precision=HIGHEST)[:n_slc]
        scores.append(jnp.where(forced_t, FORCE, jnp.where(valid_t, score_t, -FORCE)))

    def ranked():
        out = []
        for score_t in scores:
            rank = jnp.zeros(score_t.shape, F32)
            for i in range(n_slc):
                ci = score_t[i:i + 1, :]
                tie = jnp.where(blk_t > i, 1.0, 0.0)
                rank += jnp.where(ci > score_t, 1.0, jnp.where(ci == score_t, tie, 0.0))
            out.append(jnp.where(rank < n_sel, MASK_BIG, 0.0))
        return tuple(out)

    sel_ts = lax.cond(t0 + Q_BLOCK > n_sel * SEL_BLOCK, ranked,
                      lambda: tuple(jnp.where(valid_t, MASK_BIG, 0.0) for _ in kv_heads))
    sels = [jnp.concatenate([s, jnp.zeros((LANES - n_slc, Q_BLOCK), F32)], axis=0).T.astype(BF16)
            for s in sel_ts]

    def chunk(c, carry):
        k0 = pl.multiple_of(c * SLC_KEY_CHUNK, SLC_KEY_CHUNK)
        pos = k0 + lax.broadcasted_iota(jnp.int32, (1, SLC_KEY_CHUNK), 1)
        posf = pos.astype(F32)
        expand = jnp.where(
            (pos // SEL_BLOCK) == lax.broadcasted_iota(jnp.int32, (LANES, SLC_KEY_CHUNK), 0),
            1.0, 0.0).astype(BF16)
        causal = jnp.minimum(tm1 - posf, -1.0) * MASK_BIG
        rel = posf - t0f
        new = []
        for h in kv_heads:
            kv = kvs_ref[pl.ds(k0, SLC_KEY_CHUNK), lane_sl[h]].astype(BF16)
            s2 = _dot_nt(qhs[h], kv)
            bias = _dot(sels[h], expand) + causal
            out = []
            for g in range(NSA_GROUP):
                m_old, l_old, a_old = carry[h][g]
                lg = s2[g * Q_BLOCK:(g + 1) * Q_BLOCK] + (ALIBI2[h][g] * rel + bias)
                m_new = jnp.maximum(m_old, jnp.max(lg, axis=-1, keepdims=True))
                alpha = jnp.exp2(m_old - m_new)
                e = jnp.exp2(lg - m_new)
                l_new = alpha * l_old + jnp.sum(e, axis=-1, keepdims=True)
                a_new = alpha * a_old + _dot(e.astype(BF16), kv)
                out.append((m_new, l_new, a_new))
            new.append(tuple(out))
        return tuple(new)

    init = tuple(tuple((jnp.full((Q_BLOCK, 1), NEG_INF, F32), jnp.zeros((Q_BLOCK, 1), F32),
                        jnp.zeros((Q_BLOCK, LANES), F32)) for _ in range(NSA_GROUP)) for _ in kv_heads)
    n_chunks = (t0 + Q_BLOCK + SLC_KEY_CHUNK - 1) // SLC_KEY_CHUNK
    fin = lax.fori_loop(0, n_chunks, chunk, init)

    heads = []
    for h in kv_heads:
        kvw = kvw_ref[pl.ds(w0, band), lane_sl[h]].astype(BF16)
        sm_w = _exp2_softmax(_dot_nt(qhs[h], kvw), rel_w, bias_w, ALIBI2[h], Q_BLOCK)
        o_w = _dot(jnp.concatenate([e for (e, _) in sm_w], axis=0).astype(BF16), kvw)
        for g in range(NSA_GROUP):
            col = h * NSA_GROUP + g
            rows = slice(g * Q_BLOCK, (g + 1) * Q_BLOCK)
            heads.append(gates[:, col:col + 1] * o_cs[h][rows]
                         + (gates[:, NSA_HEADS + col:NSA_HEADS + col + 1] / fin[h][g][1]) * fin[h][g][2]
                         + (gates[:, 2 * NSA_HEADS + col:2 * NSA_HEADS + col + 1] / sm_w[g][1]) * o_w[rows])
    _pack_heads(o_ref, heads)


def _nsa_prompt(u, kc, wov):
    b, t, _ = u.shape
    assert t % Q_BLOCK == 0 and t >= WINDOW + Q_BLOCK and t % SLC_KEY_CHUNK == 0
    n_slc = -(-t // SEL_BLOCK)
    assert n_slc % SUBLANES == 0 and n_slc <= LANES
    return pl.pallas_call(
        functools.partial(_nsa_prompt_body, seq=t),
        out_shape=jax.ShapeDtypeStruct((b, t, NSA_HEADS * HEAD_DIM), F32),
        grid=(b, t // Q_BLOCK),
        in_specs=[
            pl.BlockSpec((None, Q_BLOCK, NSA_HEADS * HEAD_DIM), lambda bi, i: (bi, i, 0)),
            pl.BlockSpec((None, Q_BLOCK, LANES), lambda bi, i: (bi, i, BLK_MISC)),
            pl.BlockSpec((None,) + kc.shape[1:], lambda bi, i: (bi, 0, 0)),
            pl.BlockSpec((None, t, 4 * LANES), lambda bi, i: (bi, 0, BLK_KVS)),
            pl.BlockSpec((None, t, 4 * LANES), lambda bi, i: (bi, 0, BLK_KVW)),
            pl.BlockSpec(wov.shape, lambda bi, i: (0, 0)),
        ],
        out_specs=pl.BlockSpec((None, Q_BLOCK, NSA_HEADS * HEAD_DIM), lambda bi, i: (bi, i, 0)),
        compiler_params=_cparams(("parallel", "arbitrary")),
        name="nsa_prompt",
    )(u, u, kc, u, u, wov)


def _cmp_select_body(q_ref, kc_ref, wov_ref, oc_ref, sel_ref, *, past_len, t_real):
    tp = q_ref.shape[0]
    n_cr = kc_ref.shape[0]
    n_slc = -(-(past_len + t_real) // SEL_BLOCK)
    t_pos = past_len + lax.broadcasted_iota(jnp.int32, (tp, 1), 0)
    for h in range(NSA_KV_HEADS):
        qh = _stack_q(q_ref, h * Q_HEAD_COLS, HEAD_DIM ** -0.5)
        kc = kc_ref[:, h * LANES:(h + 1) * LANES].astype(BF16)
        r_idx = lax.broadcasted_iota(jnp.int32, (1, n_cr), 1)
        dist = t_pos - (r_idx * CMP_STRIDE + (CMP_STRIDE - 1))
        mask = (r_idx >= 1) & (dist >= 0)
        p_c = _softmax_groups(_dot_nt(qh, kc), dist.astype(F32), mask, ALIBI[h], tp)
        oc_ref[h] = _dot(jnp.concatenate(p_c, axis=0).astype(BF16), kc)
        imp = p_c[0] + p_c[1] + p_c[2] + p_c[3]
        score, blk = _block_scores(imp, wov_ref, t_pos, n_slc)
        work = jnp.where(blk < n_slc, score, -3e38)
        blkf = blk.astype(F32)
        picked = jnp.zeros((tp, LANES), jnp.int32)
        lane = lax.broadcasted_iota(jnp.int32, (tp, LANES), 1)
        for k in range(N_SEL):
            m = jnp.max(work, axis=-1, keepdims=True)
            idx = jnp.min(jnp.where(work == m, blkf, 3e38), axis=-1, keepdims=True)
            picked = jnp.where(lane == k, idx.astype(jnp.int32), picked)
            work = jnp.where(blkf == idx, -3e38, work)
        sel_ref[h] = picked


def _cmp_select(u_s, kc, wov, past_len, t_real):
    b, tp, _ = u_s.shape
    assert -(-(past_len + t_real) // SEL_BLOCK) >= N_SEL
    return pl.pallas_call(
        functools.partial(_cmp_select_body, past_len=past_len, t_real=t_real),
        out_shape=(jax.ShapeDtypeStruct((b, NSA_KV_HEADS, NSA_GROUP * tp, LANES), F32),
                   jax.ShapeDtypeStruct((b, NSA_KV_HEADS, tp, LANES), jnp.int32)),
        grid=(b,),
        in_specs=[
            pl.BlockSpec((None, tp, NSA_HEADS * HEAD_DIM), lambda bi: (bi, 0, 0)),
            pl.BlockSpec((None,) + kc.shape[1:], lambda bi: (bi, 0, 0)),
            pl.BlockSpec(wov.shape, lambda bi: (0, 0)),
        ],
        out_specs=(pl.BlockSpec((None, NSA_KV_HEADS, NSA_GROUP * tp, LANES), lambda bi: (bi, 0, 0, 0)),
                   pl.BlockSpec((None, NSA_KV_HEADS, tp, LANES), lambda bi: (bi, 0, 0, 0))),
        compiler_params=_cparams(("parallel",)),
        name="cmp_select",
    )(u_s, kc, wov)


def _slc_sample_body(pidx_ref, pt_ref, *refs, past_len):
    del pidx_ref, pt_ref
    blk_refs = refs[:N_SEL]
    pos_ref, ok_ref, q_ref, kvn_ref, o_ref = refs[N_SEL:]
    h, t = pl.program_id(1), pl.program_id(2)
    tp = kvn_ref.shape[0]
    q_t = q_ref[pl.ds(t, 1), :]
    row = lax.broadcasted_iota(jnp.int32, (SUBLANES, 1), 0)
    q8 = jnp.zeros((SUBLANES, LANES), F32)
    slope = jnp.zeros((SUBLANES, 1), F32)
    for g in range(NSA_GROUP):
        c0 = (g // 2) * LANES
        q8 = jnp.where(row == g, _unpack_head(q_t[:, c0:c0 + LANES], g % 2 == 1), q8)
        slope_g = jnp.where(h == 0, ALIBI[0][g], jnp.where(h == 1, ALIBI[1][g],
                                                           jnp.where(h == 2, ALIBI[2][g], ALIBI[3][g])))
        slope = jnp.where(row == g, slope_g, slope)
    qh = (q8 * HEAD_DIM ** -0.5).astype(BF16)
    t_pos = past_len + t

    kv_t = jnp.concatenate([r[...].reshape(2 * HEAD_DIM, PAGE_SIZE) for r in blk_refs], axis=1).astype(BF16)
    d = t_pos - pos_ref[...]
    msk = (ok_ref[...] > 0) & (d >= 0)
    kvn = jnp.concatenate([kvn_ref[...], jnp.zeros((LANES - tp, LANES), F32)], axis=0).astype(BF16)
    idx_n = lax.broadcasted_iota(jnp.int32, (1, LANES), 1)
    dn = t - idx_n
    mskn = (idx_n < tp) & (dn >= 0)
    s_all = _dot(qh, kv_t)
    s_new = _dot_nt(qh, kvn)
    sg = jnp.where(msk, s_all - slope * d.astype(F32), NEG_INF)
    sn = jnp.where(mskn, s_new - slope * dn.astype(F32), NEG_INF)
    m = jnp.maximum(jnp.max(sg, axis=-1, keepdims=True), jnp.max(sn, axis=-1, keepdims=True))
    e = jnp.where(msk, jnp.exp(sg - m), 0.0)
    en = jnp.where(mskn, jnp.exp(sn - m), 0.0)
    l = jnp.sum(e, axis=-1, keepdims=True) + jnp.sum(en, axis=-1, keepdims=True)
    o_ref[...] = (_dot_nt(e.astype(BF16), kv_t) + _dot(en.astype(BF16), kvn)) / jnp.maximum(l, 1e-30)


def _slc_sample(u_s, cache, table, sel, layer, past_len, t_real):
    b, tp, _ = u_s.shape
    n_past_blk = past_len // SEL_BLOCK
    per_page = PAGE_SIZE // SEL_BLOCK
    assert past_len % SEL_BLOCK == 0 and t_real <= SEL_BLOCK and PAGE_SIZE % SEL_BLOCK == 0

    page_idx = (jnp.minimum(sel, n_past_blk - 1) // per_page).reshape(-1)
    blk_l = jnp.repeat(sel, PAGE_SIZE, axis=-1)
    in_page = jnp.asarray(np.arange(N_SEL * PAGE_SIZE) % PAGE_SIZE, jnp.int32)
    pos = ((blk_l // per_page) * PAGE_SIZE + in_page)[:, :, :, None, :]
    ok = ((blk_l < n_past_blk) & (in_page // SEL_BLOCK == blk_l % per_page)).astype(jnp.int32)[:, :, :, None, :]

    def blk_spec(k):
        def imap(bi, h, t, pidx, pt):
            return (pt[bi, pidx[((bi * NSA_KV_HEADS + h) * t_real + t) * N_SEL + k]], layer, h, 0, 0, 0)
        return pl.BlockSpec((None, None, None, 2, HEAD_DIM, PAGE_SIZE), imap)

    lane_spec = pl.BlockSpec((None, None, None, 1, N_SEL * PAGE_SIZE), lambda bi, h, t, pidx, pt: (bi, h, t, 0, 0))
    grid_spec = pltpu.PrefetchScalarGridSpec(
        num_scalar_prefetch=2,
        grid=(b, NSA_KV_HEADS, t_real),
        in_specs=[blk_spec(k) for k in range(N_SEL)] + [
            lane_spec, lane_spec,
            pl.BlockSpec((None, tp, Q_HEAD_COLS), lambda bi, h, t, pidx, pt: (bi, 0, h)),
            pl.BlockSpec((None, tp, LANES), lambda bi, h, t, pidx, pt: (bi, 0, BLK_KVS * 4 + h)),
        ],
        out_specs=pl.BlockSpec((None, None, None, SUBLANES, LANES), lambda bi, h, t, pidx, pt: (bi, h, t, 0, 0)),
    )
    return pl.pallas_call(
        functools.partial(_slc_sample_body, past_len=past_len),
        out_shape=jax.ShapeDtypeStruct((b, NSA_KV_HEADS, t_real, SUBLANES, LANES), F32),
        grid_spec=grid_spec,
        compiler_params=_cparams(("parallel", "parallel", "arbitrary")),
        name="slc_sample",
    )(page_idx, table, *([cache] * N_SEL), pos, ok, u_s, u_s)


def _win_combine_body(q_ref, misc_ref, win_ref, kvn_ref, oc_ref, os_ref, o_ref, *, past_len):
    tp = q_ref.shape[0]
    wb = win_ref.shape[0]
    t_pos = past_len + lax.broadcasted_iota(jnp.int32, (tp, 1), 0)
    gates = jax.nn.sigmoid(misc_ref[...])
    n_keys = wb + LANES
    idx = lax.broadcasted_iota(jnp.int32, (1, n_keys), 1)
    d = t_pos - (past_len - wb + idx)
    mask = (idx < wb + tp) & (d >= 0) & (d < WINDOW)
    heads = []
    for h in range(NSA_KV_HEADS):
        sl = slice(h * LANES, (h + 1) * LANES)
        qh = _stack_q(q_ref, h * Q_HEAD_COLS, HEAD_DIM ** -0.5)
        kv = jnp.concatenate([win_ref[:, sl], kvn_ref[:, sl], jnp.zeros((LANES - tp, LANES), F32)],
                             axis=0).astype(BF16)
        p_w = _softmax_groups(_dot_nt(qh, kv), d.astype(F32), mask, ALIBI[h], tp)
        o_w = _dot(jnp.concatenate(p_w, axis=0).astype(BF16), kv)
        o_c = oc_ref[h]
        t_live = os_ref.shape[1]
        for g in range(NSA_GROUP):
            col = h * NSA_GROUP + g
            rows = slice(g * tp, (g + 1) * tp)
            o_s = jnp.concatenate([os_ref[h, tt, g:g + 1, :] for tt in range(t_live)]
                                  + [jnp.zeros((tp - t_live, LANES), F32)], axis=0)
            heads.append(gates[:, col:col + 1] * o_c[rows]
                         + gates[:, NSA_HEADS + col:NSA_HEADS + col + 1] * o_s
                         + gates[:, 2 * NSA_HEADS + col:2 * NSA_HEADS + col + 1] * o_w[rows])
    _pack_heads(o_ref, heads)


def _win_combine(u_s, win, o_c, o_s, layer, past_len):
    b, tp, _ = u_s.shape
    wb = win.shape[2]
    return pl.pallas_call(
        functools.partial(_win_combine_body, past_len=past_len),
        out_shape=jax.ShapeDtypeStruct((b, tp, NSA_HEADS * HEAD_DIM), F32),
        grid=(b,),
        in_specs=[
            pl.BlockSpec((None, tp, NSA_HEADS * HEAD_DIM), lambda bi: (bi, 0, 0)),
            pl.BlockSpec((None, tp, LANES), lambda bi: (bi, 0, BLK_MISC)),
            pl.BlockSpec((None, None, wb, 4 * LANES), lambda bi: (bi, layer, 0, 0)),
            pl.BlockSpec((None, tp, 4 * LANES), lambda bi: (bi, 0, BLK_KVW)),
            pl.BlockSpec((None,) + o_c.shape[1:], lambda bi: (bi, 0, 0, 0)),
            pl.BlockSpec((None,) + o_s.shape[1:], lambda bi: (bi, 0, 0, 0, 0)),
        ],
        out_specs=pl.BlockSpec((None, tp, NSA_HEADS * HEAD_DIM), lambda bi: (bi, 0, 0)),
        compiler_params=_cparams(("parallel",)),
        name="win_combine",
    )(u_s, u_s, win, u_s, o_c, o_s)


def _cumsum_rows(x):
    n = x.shape[0]
    row = lax.broadcasted_iota(jnp.int32, x.shape, 0)
    shift = 1
    while shift < n:
        x = x + jnp.where(row >= shift, pltpu.roll(x, shift, 0), 0.0)
        shift *= 2
    return x


def _gla_body(gqk_ref, gv_ref, gog_ref, misc_ref, wdec_ref, bdec_ref, gn_ref, s0_ref,
              o_ref, sout_ref, st_ref, *, c_real, n_sub):
    ci = pl.program_id(1)

    @pl.when(ci == 0)
    def _():
        for h in range(GLA_HEADS):
            st_ref[h] = s0_ref[h].T

    cp = gqk_ref.shape[0] // n_sub
    for j in range(n_sub):
        _gla_chunk(gqk_ref, gv_ref, gog_ref, misc_ref, wdec_ref, bdec_ref, gn_ref, o_ref, st_ref,
                   slice(j * cp, (j + 1) * cp), cp, c_real)

    @pl.when(ci == pl.num_programs(1) - 1)
    def _():
        for h in range(GLA_HEADS):
            sout_ref[h] = st_ref[h].T[:GLA_DK, :]


def _gla_chunk(gqk_ref, gv_ref, gog_ref, misc_ref, wdec_ref, bdec_ref, gn_ref, o_ref, st_ref, rs, cp, c_real):
    k_base = GLA_HEADS * GLA_DK
    sb = min(16, cp)
    row = lax.broadcasted_iota(jnp.int32, (cp, 1), 0)
    live = row < c_real
    x = jnp.dot(misc_ref[rs, :], wdec_ref[...], preferred_element_type=F32, precision=HIGHEST) + bdec_ref[...]
    log_a = (jnp.minimum(x, 0.0) - jnp.log1p(jnp.exp(-jnp.abs(x)))) / GLA_GATE_TAU
    b_all = _cumsum_rows(jnp.where(live, log_a, 0.0))
    for h in range(GLA_HEADS):
        sl = slice(h * LANES, (h + 1) * LANES)
        pair = slice((h // 2) * LANES, (h // 2 + 1) * LANES)
        q = _unpack_head(gqk_ref[rs, pair], h % 2 == 1) * GLA_DK ** -0.5
        k_pair = slice(k_base + (h // 2) * LANES, k_base + (h // 2 + 1) * LANES)
        k = jnp.where(live, _unpack_head(gqk_ref[rs, k_pair], h % 2 == 1), 0.0)
        v = jnp.where(live, gv_ref[rs, sl], 0.0)
        b = b_all[:, sl]
        kb = k.astype(BF16)
        st = st_ref[h]
        attn_rows = []
        for i in range(cp // sb):
            qi = q[i * sb:(i + 1) * sb]
            bi = b[i * sb:(i + 1) * sb]
            sub_row = lax.broadcasted_iota(jnp.int32, (sb, 1), 0)
            ys = []
            for s_loc in range(sb):
                bs = b[i * sb + s_loc:i * sb + s_loc + 1]
                ys.append(qi * jnp.exp(jnp.where(sub_row >= s_loc, bi - bs, NEG_INF)))
            z = _dot_nt(jnp.concatenate(ys, axis=0).astype(BF16), kb)
            lane = lax.broadcasted_iota(jnp.int32, (sb, cp), 1)
            a_i = jnp.zeros((sb, cp), F32)
            for s_loc in range(sb):
                a_i += jnp.where(lane == i * sb + s_loc, z[s_loc * sb:(s_loc + 1) * sb], 0.0)
            if i > 0:
                ref_b = b[i * sb - 1:i * sb]
                qt = qi * jnp.exp(bi - ref_b)
                kt = k * jnp.exp(jnp.where(row < i * sb, ref_b - b, NEG_INF))
                a_i += _dot_nt(qt.astype(BF16), kt.astype(BF16))
            attn_rows.append(a_i)
        attn = jnp.concatenate(attn_rows, axis=0) if len(attn_rows) > 1 else attn_rows[0]
        o = _dot(attn.astype(BF16), v.astype(BF16))
        o += _dot_nt((q * jnp.exp(b)).astype(BF16), st.astype(BF16))
        b_last = b[c_real - 1:c_real]
        kd = k * jnp.exp(b_last - b)
        st_ref[h] = jnp.exp(b_last) * st + _dot(v.T.astype(BF16), kd.astype(BF16))
        o_ref[rs, sl] = _rms(o, gn_ref[...]) * jax.nn.silu(gog_ref[rs, sl])


def _recurrent_body(gqk_ref, gv_ref, gog_ref, misc_ref, wdec_ref, bdec_ref, gn_ref, s0_ref,
                    rq_ref, rk_ref, rv_ref, rg_ref, rn_ref, r0_ref,
                    og_ref, sout_ref, or_ref, rout_ref, st_ref, rt_ref, *, c_real, n_sub):
    _gla_body(gqk_ref, gv_ref, gog_ref, misc_ref, wdec_ref, bdec_ref, gn_ref, s0_ref,
              og_ref, sout_ref, st_ref, c_real=c_real, n_sub=n_sub)
    ret_real = c_real if n_sub == 1 else rq_ref.shape[0]
    _ret_body(rq_ref, rk_ref, rv_ref, rg_ref, rn_ref, r0_ref, or_ref, rout_ref, rt_ref, c_real=ret_real)


def _recurrent(u, s0, r0, w_dec, b_dec, gla_norm, ret_norm, layer, gla_chunk, c_real, n_sub=1):
    b, t, _ = u.shape
    cp = gla_chunk * n_sub
    assert t % cp == 0 and (n_sub == 1 or c_real == gla_chunk)
    blk = lambda idx: pl.BlockSpec((None, cp, 4 * LANES), lambda bi, i: (bi, i, idx))
    per_b = lambda shape: pl.BlockSpec((None,) + shape, lambda bi, i: (bi, 0, 0, 0))
    per_layer = lambda shape: pl.BlockSpec((None,) + shape, lambda bi, i: (layer, 0, 0))
    return pl.pallas_call(
        functools.partial(_recurrent_body, c_real=c_real, n_sub=n_sub),
        out_shape=(jax.ShapeDtypeStruct((b, t, GLA_HEADS * GLA_DV), F32),
                   jax.ShapeDtypeStruct((b, GLA_HEADS, GLA_DK, GLA_DV), F32),
                   jax.ShapeDtypeStruct((b, t, RET_HEADS * RET_DV), F32),
                   jax.ShapeDtypeStruct((b, RET_HEADS, RET_DK, RET_DV), F32)),
        grid=(b, t // cp),
        in_specs=[
            blk(BLK_GQK), blk(BLK_GV), blk(BLK_GOG),
            pl.BlockSpec((None, cp, LANES), lambda bi, i: (bi, i, BLK_MISC)),
            per_layer((LANES, 4 * LANES)), per_layer((1, 4 * LANES)), per_layer((1, GLA_DV)),
            per_b((GLA_HEADS, LANES, GLA_DV)),
            blk(BLK_RQ), blk(BLK_RK), blk(BLK_RV), blk(BLK_RG),
            per_layer((1, RET_HEADS * RET_DV)),
            per_b((RET_HEADS, RET_DK, RET_DV)),
        ],
        out_specs=(pl.BlockSpec((None, cp, GLA_HEADS * GLA_DV), lambda bi, i: (bi, i, 0)),
                   per_b((GLA_HEADS, GLA_DK, GLA_DV)),
                   pl.BlockSpec((None, cp, RET_HEADS * RET_DV), lambda bi, i: (bi, i, 0)),
                   per_b((RET_HEADS, RET_DK, RET_DV))),
        scratch_shapes=[pltpu.VMEM((GLA_HEADS, GLA_DV, LANES), F32),
                        pltpu.VMEM((RET_HEADS, RET_DV, RET_DK), F32)],
        compiler_params=_cparams(("parallel", "arbitrary")),
        name="recurrent",
    )(u, u, u, u, w_dec, b_dec, gla_norm, s0, u, u, u, u, ret_norm, r0)


def _ret_body(rq_ref, rk_ref, rv_ref, rg_ref, gn_ref, r0_ref, o_ref, rout_ref, rt_ref, *, c_real):
    ci = pl.program_id(1)
    cp = rq_ref.shape[0]

    @pl.when(ci == 0)
    def _():
        for h in range(RET_HEADS):
            rt_ref[h] = r0_ref[h].T

    row = lax.broadcasted_iota(jnp.int32, (cp, 1), 0)
    live = row < c_real
    rowf = row.astype(F32)
    rel = rowf - lax.broadcasted_iota(jnp.int32, (1, cp), 1).astype(F32)
    for h in range(RET_HEADS):
        sl = slice(h * LANES, (h + 1) * LANES)
        lg = RET_LOG_GAMMA[h]
        q = rq_ref[:, sl]
        k = jnp.where(live, rk_ref[:, sl] * RET_DK ** -0.5, 0.0)
        v = jnp.where(live, rv_ref[:, sl], 0.0)
        decay = jnp.where(rel >= 0, jnp.exp(jnp.maximum(rel, 0.0) * lg), 0.0)
        attn = _dot_nt(q.astype(BF16), k.astype(BF16)) * decay
        o = _dot(attn.astype(BF16), v.astype(BF16))
        rt = rt_ref[h]
        o += _dot_nt((q * jnp.exp((rowf + 1.0) * lg)).astype(BF16), rt.astype(BF16))
        kd = k * jnp.exp((c_real - 1.0 - rowf) * lg)
        rt_ref[h] = math.exp(c_real * lg) * rt + _dot(v.T.astype(BF16), kd.astype(BF16))
        mu = jnp.mean(o, axis=-1, keepdims=True)
        var = jnp.mean(jnp.square(o - mu), axis=-1, keepdims=True)
        o_ref[:, sl] = (o - mu) * lax.rsqrt(var + EPS) * gn_ref[:, sl] * jax.nn.silu(rg_ref[:, sl])

    @pl.when(ci == pl.num_programs(1) - 1)
    def _():
        for h in range(RET_HEADS):
            rout_ref[h] = rt_ref[h].T


def _pad_heads(w, n_heads):
    lead = w.shape[:-1]
    w = w.reshape(lead + (n_heads, HEAD_DIM))
    w = jnp.pad(w, [(0, 0)] * len(lead) + [(0, 0), (0, LANES - HEAD_DIM)])
    return w.reshape(lead + (n_heads * LANES,))


def _prep_w_in(w_in):
    cuts = np.cumsum((0,) + IN_SIZES)
    nq, kvc, kvs, kvw, ng, gq, gk, gv, glr, gog, rq, rk, rv, rg = [
        w_in[..., cuts[i]:cuts[i + 1]] for i in range(len(IN_SIZES))]
    misc = jnp.concatenate([ng, glr], axis=-1)
    misc = jnp.pad(misc, [(0, 0), (0, 0), (0, LANES - misc.shape[-1])])
    w = jnp.concatenate([nq, kvc, kvs, kvw, gq, gk, gv, gog, rq, rk, rv, rg, misc], axis=-1)
    assert w.shape[-1] == (BLK_MISC + 1) * LANES and U_WIDTH % MXU_COLS == 0
    w = jnp.pad(w, [(0, 0), (0, 0), (0, U_WIDTH - w.shape[-1])])
    return w.astype(BF16)


def _prep_cmp_weights(w_cmp1, w_cmp2):
    depth = w_cmp1.shape[0]
    half = CMP_STRIDE * HEAD_DIM
    w1 = w_cmp1.reshape(depth, 2, 2, CMP_STRIDE, HEAD_DIM, CMP_HIDDEN)
    w1 = w1.transpose(0, 3, 1, 4, 2, 5)
    z = jnp.zeros_like(w1[:, :, 0])
    k_rows = jnp.stack([w1[:, :, 0], z], axis=4)
    v_rows = jnp.stack([z, w1[:, :, 1]], axis=4)
    wblk = jnp.concatenate([k_rows, v_rows], axis=2)
    wblk = wblk.reshape(depth, CMP_STRIDE // 2, 4 * HEAD_DIM, 4 * CMP_HIDDEN).astype(BF16)
    del half
    z2 = jnp.zeros_like(w_cmp2[:, 0])
    w2blk = jnp.concatenate([jnp.concatenate([w_cmp2[:, 0], z2], axis=-1),
                             jnp.concatenate([z2, w_cmp2[:, 1]], axis=-1)], axis=1).astype(BF16)
    return wblk, w2blk


def _overlap_matrix(n_rows, n_slc, n_cols):
    ratio = SEL_BLOCK // CMP_STRIDE
    w = np.zeros((n_rows, n_cols), np.float32)
    for j in range(n_slc):
        for k, wk in enumerate(OVERLAP_W):
            if ratio * j + k < n_rows:
                w[ratio * j + k, j] = wk
    return jnp.asarray(w)


def kernel(x_prompt, x_sample, cache_cmp_kv, cache_slc_kv, cache_win_kv, state_gla, state_ret, page_table,
           p_prompt, p_sample, norm_pre, norm_post, w_ffn_gate, w_ffn_up, w_ffn_down, w_in, w_out,
           w_cmp1, w_cmp2, cmp_pos, nsa_norm, w_gla_decay, b_gla_decay, gla_norm, ret_norm, w_ple, w_ple_gate):
    depth = w_in.shape[0]
    bp, seq, d_model = x_prompt.shape
    bs, t_real, _ = x_sample.shape
    n_pool = cache_cmp_kv.shape[0]
    n_pages = page_table.shape[1]
    past_len = n_pages * PAGE_SIZE
    tp = -(-t_real // SUBLANES) * SUBLANES
    kvw = NSA_KV_HEADS * 2 * HEAD_DIM

    wg = w_ffn_gate.astype(BF16)
    wu = w_ffn_up.astype(BF16)
    wd = w_ffn_down.astype(BF16)
    w_in_r = _prep_w_in(w_in)
    w_out_b = w_out.astype(BF16)
    w_ple_b = w_ple.astype(BF16)
    w_gate_b = w_ple_gate.astype(BF16)
    wblk, w2blk = _prep_cmp_weights(w_cmp1, w_cmp2)
    g_pre = norm_pre.reshape(depth, 3, 1, d_model)
    g_post = norm_post.reshape(depth, 3, 1, d_model)
    nsa_g = nsa_norm.reshape(depth, 1, -1)
    gla_g = gla_norm.reshape(depth, 1, -1)
    ret_g = ret_norm.reshape(depth, 1, -1)
    w_dec = jnp.pad(_pad_heads(w_gla_decay, GLA_HEADS),
                    [(0, 0), (MISC_GLR, LANES - MISC_GLR - GLA_GATE_RANK), (0, 0)])
    b_dec = _pad_heads(b_gla_decay, GLA_HEADS).reshape(depth, 1, -1)
    pos_flat = cmp_pos.reshape(depth, 2, 1, CMP_BLOCK * HEAD_DIM)

    cmp_pages = cache_cmp_kv.transpose(0, 1, 3, 4, 5, 2)
    slc_rows = cache_slc_kv.transpose(0, 1, 3, 4, 5, 2)
    win_rows = cache_win_kv.reshape(bs, depth, -1, kvw)
    wb = win_rows.shape[2]

    n_cr_p = seq // CMP_STRIDE
    wov_p = _overlap_matrix(n_cr_p, -(-seq // SEL_BLOCK), LANES).T
    n_cr_s = past_len // CMP_STRIDE
    n_slc_s = -(-(past_len + t_real) // SEL_BLOCK)
    wov_s = _overlap_matrix(n_cr_s, n_slc_s, -(-n_slc_s // LANES) * LANES)
    table_p = jnp.arange(bp * (seq // PAGE_SIZE), dtype=jnp.int32).reshape(bp, seq // PAGE_SIZE)

    xp = x_prompt.reshape(bp * seq, d_model)
    xs = jnp.pad(x_sample, ((0, 0), (0, tp - t_real), (0, 0))).reshape(bs * tp, d_model)
    pp = p_prompt.reshape(depth, bp * seq, -1)
    ps = jnp.pad(p_sample, ((0, 0), (0, 0), (0, tp - t_real), (0, 0))).reshape(depth, bs * tp, -1)
    s0_gla_p = jnp.zeros((bp, GLA_HEADS, LANES, GLA_DV), F32)
    s0_ret_p = jnp.zeros((bp, RET_HEADS, RET_DK, RET_DV), F32)
    s0_gla_s = jnp.pad(state_gla, ((0, 0), (0, 0), (0, 0), (0, LANES - GLA_DK), (0, 0)))

    tm_p = 1024
    tm_s = bs * tp
    st_p = [[] for _ in range(5)]
    st_s = [[] for _ in range(5)]
    for l in range(depth):
        bias = _cmp_bias(pos_flat, w_cmp1, l).reshape(1, 2 * CMP_HIDDEN)

        xp = _ffn(xp, g_pre, g_post, wg, wu, wd, l, 0, 0, tm_p // 2)
        u2 = _proj_in(xp, g_pre, w_in_r, l, tm_p)
        u = u2.reshape(bp, seq, U_WIDTH)
        kv_c = u[..., BLK_KVC * kvw:(BLK_KVC + 1) * kvw]
        kv_s = u[..., BLK_KVS * kvw:(BLK_KVS + 1) * kvw]
        kv_w = u[..., BLK_KVW * kvw:(BLK_KVW + 1) * kvw]
        kc_pages = kv_c.reshape(bp * (seq // PAGE_SIZE), 1, PAGE_SIZE, NSA_KV_HEADS, LANES).transpose(0, 1, 3, 2, 4)
        kc = _compress(kc_pages, table_p, 0,
                       wblk[l], bias, w2blk[l])
        o_nsa = _nsa_prompt(u, kc, wov_p)
        gla_out, s_g, ret_out, s_r = _recurrent(u, s0_gla_p, s0_ret_p, w_dec, b_dec, gla_g, ret_g, l,
                                                GLA_CHUNK, GLA_CHUNK, n_sub=2)
        xp = _proj_out(xp, o_nsa.reshape(bp * seq, -1), gla_out.reshape(bp * seq, -1),
                       ret_out.reshape(bp * seq, -1), nsa_g, w_out_b, g_post, l, tm_p // 2)
        xp = _ffn(xp, g_pre, g_post, wg, wu, wd, l, 1, 2, tm_p // 2)
        xp = _ple(xp, pp, w_gate_b, w_ple_b, l, tm_p // 2)
        win_keep = min(WINDOW, seq)
        for j, a in enumerate((kv_c, kv_s, kv_w[:, seq - win_keep:], s_g, s_r)):
            st_p[j].append(a)

        xs = _ffn(xs, g_pre, g_post, wg, wu, wd, l, 0, 0, tm_s)
        us = _proj_in(xs, g_pre, w_in_r, l, tm_s).reshape(bs, tp, U_WIDTH)
        kv_c = us[:, :t_real, BLK_KVC * kvw:(BLK_KVC + 1) * kvw]
        kv_s = us[:, :t_real, BLK_KVS * kvw:(BLK_KVS + 1) * kvw]
        kv_w = us[:, :t_real, BLK_KVW * kvw:(BLK_KVW + 1) * kvw]
        kc = _compress(cmp_pages, page_table, l, wblk[l], bias, w2blk[l])
        o_c, sel = _cmp_select(us, kc, wov_s, past_len, t_real)
        o_s = _slc_sample(us, slc_rows, page_table, sel[:, :, :t_real, :N_SEL], l, past_len, t_real)
        o_nsa = _win_combine(us, win_rows, o_c, o_s, l, past_len)
        gla_out, s_g, ret_out, s_r = _recurrent(us, s0_gla_s[:, l], state_ret[:, l], w_dec, b_dec, gla_g, ret_g, l,
                                                tp, t_real)
        xs = _proj_out(xs, o_nsa.reshape(bs * tp, -1), gla_out.reshape(bs * tp, -1),
                       ret_out.reshape(bs * tp, -1), nsa_g, w_out_b, g_post, l, tm_s)
        xs = _ffn(xs, g_pre, g_post, wg, wu, wd, l, 1, 2, tm_s)
        xs = _ple(xs, ps, w_gate_b, w_ple_b, l, tm_s)
        new_win = jnp.concatenate([win_rows[:, l], kv_w], axis=1)[:, t_real:]
        for j, a in enumerate((kv_c, kv_s, new_win, s_g, s_r)):
            st_s[j].append(a)

    def kv_stack(parts):
        a = jnp.stack(parts, axis=1)
        return a.reshape(a.shape[:3] + (NSA_KV_HEADS, 2, HEAD_DIM))

    y_p = xp.reshape(bp, seq, d_model)
    y_s = xs.reshape(bs, tp, d_model)[:, :t_real]
    return (y_p, y_s,
            kv_stack(st_p[0]), kv_stack(st_p[1]), kv_stack(st_p[2]),
            jnp.stack(st_p[3], axis=1), jnp.stack(st_p[4], axis=1),
            kv_stack(st_s[0]), kv_stack(st_s[1]), kv_stack(st_s[2]),
            jnp.stack(st_s[3], axis=1), jnp.stack(st_s[4], axis=1))
```

```python
import functools
import math

import numpy as np
import jax
import jax.numpy as jnp
from jax import lax
from jax.experimental import pallas as pl
from jax.experimental.pallas import tpu as pltpu

F32 = jnp.float32
BF16 = jnp.bfloat16
HIGHEST = lax.Precision.HIGHEST

HEAD_DIM = 64
NSA_HEADS = 16
NSA_KV_HEADS = 4
NSA_GROUP = 4
CMP_BLOCK = 32
CMP_STRIDE = 16
CMP_HIDDEN = 128
SEL_BLOCK = 64
N_SEL = 16
WINDOW = 512
Q_BLOCK = 128
OVERLAP_W = (0.5, 1.0, 1.0, 1.0, 0.5)
GLA_HEADS = 4
GLA_DK = 64
GLA_DV = 128
GLA_GATE_RANK = 16
GLA_GATE_TAU = 16.0
GLA_CHUNK = 64
RET_HEADS = 4
RET_DK = 128
RET_DV = 128
RET_CHUNK = 64
PAGE_SIZE = 128
NEG_INF = -1e30
FORCE = 1e9
EPS = 1e-6
IN_SIZES = (1024, 512, 512, 512, 48, 256, 256, 512, 16, 512, 512, 512, 512, 512)

LANES = 128
SUBLANES = 8
VMEM_LIMIT = 56 * 1024 * 1024

MXU_COLS = 256
U_WIDTH = 6400
BLK_KVC, BLK_KVS, BLK_KVW = 2, 3, 4
BLK_GQK, BLK_GV, BLK_GOG = 5, 6, 7
BLK_RQ, BLK_RK, BLK_RV, BLK_RG = 8, 9, 10, 11
BLK_MISC = 48
MISC_GLR = 48
Q_HEAD_COLS = NSA_GROUP * HEAD_DIM

ALIBI = [[2.0 ** (-8.0 * (h * NSA_GROUP + g + 1) / NSA_HEADS) for g in range(NSA_GROUP)]
         for h in range(NSA_KV_HEADS)]
RET_LOG_GAMMA = [math.log1p(-(2.0 ** (-5.0 - h))) for h in range(RET_HEADS)]


def _cparams(sem):
    return pltpu.CompilerParams(dimension_semantics=sem, vmem_limit_bytes=VMEM_LIMIT)


def _rms(x, g=None):
    y = x * lax.rsqrt(jnp.mean(x * x, axis=-1, keepdims=True) + EPS)
    return y if g is None else y * g


def _dot(a, b):
    return jnp.dot(a, b, preferred_element_type=F32)


def _dot_nt(a, b):
    return lax.dot_general(a, b, (((1,), (1,)), ((), ())), preferred_element_type=F32)


def _ffn_body(x_ref, gpre_ref, wg_ref, wu_ref, wd_ref, gpost_ref, o_ref, xn_ref, acc_ref):
    f = pl.program_id(1)

    @pl.when(f == 0)
    def _():
        xn_ref[...] = _rms(x_ref[...], gpre_ref[...]).astype(BF16)
        acc_ref[...] = jnp.zeros_like(acc_ref)

    xn = xn_ref[...]
    h = jax.nn.silu(_dot(xn, wg_ref[...])) * _dot(xn, wu_ref[...])
    acc_ref[...] += _dot(h.astype(BF16), wd_ref[...])

    @pl.when(f == pl.num_programs(1) - 1)
    def _():
        o_ref[...] = x_ref[...] + 0.5 * _rms(acc_ref[...], gpost_ref[...])


def _ffn(x, g_pre, g_post, wg, wu, wd, layer, norm_idx, tm, tf=512):
    m, d = x.shape
    ff = wg.shape[-1]
    assert m % tm == 0 and ff % tf == 0
    return pl.pallas_call(
        _ffn_body,
        out_shape=jax.ShapeDtypeStruct((m, d), F32),
        grid=(m // tm, ff // tf),
        in_specs=[
            pl.BlockSpec((tm, d), lambda i, f: (i, 0)),
            pl.BlockSpec((None, None, 1, d), lambda i, f: (layer, norm_idx, 0, 0)),
            pl.BlockSpec((d, tf), lambda i, f: (0, f)),
            pl.BlockSpec((d, tf), lambda i, f: (0, f)),
            pl.BlockSpec((tf, d), lambda i, f: (f, 0)),
            pl.BlockSpec((None, None, 1, d), lambda i, f: (layer, norm_idx, 0, 0)),
        ],
        out_specs=pl.BlockSpec((tm, d), lambda i, f: (i, 0)),
        scratch_shapes=[pltpu.VMEM((tm, d), BF16), pltpu.VMEM((tm, d), F32)],
        compiler_params=_cparams(("parallel", "arbitrary")),
        name="ffn",
    )(x, g_pre, wg, wu, wd, g_post)


def _ffn_cast_body(x_ref, gpre_ref, wg_ref, wu_ref, wd_ref, gpost_ref, o_ref, wgb_ref, wub_ref, wdb_ref,
                   xn_ref, acc_ref):
    wgb_ref[...] = wg_ref[...].astype(BF16)
    wub_ref[...] = wu_ref[...].astype(BF16)
    wdb_ref[...] = wd_ref[...].astype(BF16)
    _ffn_body(x_ref, gpre_ref, wgb_ref, wub_ref, wdb_ref, gpost_ref, o_ref, xn_ref, acc_ref)


def _ffn_cast(x, g_pre, g_post, wg, wu, wd, layer, which, norm_idx, tf=512):
    m, d = x.shape
    ff = wg.shape[-1]
    assert ff % tf == 0
    return pl.pallas_call(
        _ffn_cast_body,
        out_shape=(jax.ShapeDtypeStruct((m, d), F32), jax.ShapeDtypeStruct((d, ff), BF16),
                   jax.ShapeDtypeStruct((d, ff), BF16), jax.ShapeDtypeStruct((ff, d), BF16)),
        grid=(1, ff // tf),
        in_specs=[
            pl.BlockSpec((m, d), lambda i, f: (0, 0)),
            pl.BlockSpec((None, None, 1, d), lambda i, f: (layer, norm_idx, 0, 0)),
            pl.BlockSpec((None, None, d, tf), lambda i, f: (layer, which, 0, f)),
            pl.BlockSpec((None, None, d, tf), lambda i, f: (layer, which, 0, f)),
            pl.BlockSpec((None, None, tf, d), lambda i, f: (layer, which, f, 0)),
            pl.BlockSpec((None, None, 1, d), lambda i, f: (layer, norm_idx, 0, 0)),
        ],
        out_specs=(pl.BlockSpec((m, d), lambda i, f: (0, 0)), pl.BlockSpec((d, tf), lambda i, f: (0, f)),
                   pl.BlockSpec((d, tf), lambda i, f: (0, f)), pl.BlockSpec((tf, d), lambda i, f: (f, 0))),
        scratch_shapes=[pltpu.VMEM((m, d), BF16), pltpu.VMEM((m, d), F32)],
        compiler_params=_cparams(("arbitrary", "arbitrary")),
        name="ffn_cast",
    )(x, g_pre, wg, wu, wd, g_post)


def _proj_in_body(x_ref, g_ref, w_ref, o_ref, xn_ref):
    @pl.when(pl.program_id(1) == 0)
    def _():
        xn_ref[...] = _rms(x_ref[...], g_ref[...]).astype(BF16)

    o_ref[...] = _dot(xn_ref[...], w_ref[...])


def _proj_in(x, g_pre, w_in, layer, tm, tn=5 * MXU_COLS):
    m, d = x.shape
    n = w_in.shape[-1]
    return pl.pallas_call(
        _proj_in_body,
        out_shape=jax.ShapeDtypeStruct((m, n), F32),
        grid=(m // tm, n // tn),
        in_specs=[
            pl.BlockSpec((tm, d), lambda i, j: (i, 0)),
            pl.BlockSpec((None, None, 1, d), lambda i, j: (layer, 1, 0, 0)),
            pl.BlockSpec((None, d, tn), lambda i, j: (layer, 0, j)),
        ],
        out_specs=pl.BlockSpec((tm, tn), lambda i, j: (i, j)),
        scratch_shapes=[pltpu.VMEM((tm, d), BF16)],
        compiler_params=_cparams(("parallel", "arbitrary")),
        name="proj_in",
    )(x, g_pre, w_in)


def _proj_out_body(x_ref, nsa_ref, gla_ref, ret_ref, gn_ref, w_ref, gpost_ref, o_ref):
    nsa_w = nsa_ref.shape[-1]
    gla_w = gla_ref.shape[-1]
    nsa = _rms(nsa_ref[...], gn_ref[...]).astype(BF16)
    y = _dot(nsa, w_ref[0:nsa_w, :])
    y += _dot(gla_ref[...].astype(BF16), w_ref[nsa_w:nsa_w + gla_w, :])
    y += _dot(ret_ref[...].astype(BF16), w_ref[nsa_w + gla_w:, :])
    o_ref[...] = x_ref[...] + _rms(y, gpost_ref[...])


def _proj_out(x, o_nsa, gla_out, ret_out, nsa_norm, w_out, g_post, layer, tm):
    m, d = x.shape
    row = lambda a: pl.BlockSpec((tm, a.shape[-1]), lambda i: (i, 0))
    return pl.pallas_call(
        _proj_out_body,
        out_shape=jax.ShapeDtypeStruct((m, d), F32),
        grid=(m // tm,),
        in_specs=[
            row(x), row(o_nsa), row(gla_out), row(ret_out),
            pl.BlockSpec((None, 1, o_nsa.shape[-1]), lambda i: (layer, 0, 0)),
            pl.BlockSpec((None, w_out.shape[1], d), lambda i: (layer, 0, 0)),
            pl.BlockSpec((None, None, 1, d), lambda i: (layer, 1, 0, 0)),
        ],
        out_specs=row(x),
        compiler_params=_cparams(("parallel",)),
        name="proj_out",
    )(x, o_nsa, gla_out, ret_out, nsa_norm, w_out, g_post)


def _ple_body(x_ref, p_ref, wg_ref, wp_ref, o_ref):
    x = x_ref[...]
    gate = jax.nn.sigmoid(_dot(_rms(x).astype(BF16), wg_ref[...]))
    o_ref[...] = x + gate * _dot(p_ref[...].astype(BF16), wp_ref[...])


def _ple(x, p, w_gate, w_ple, layer, tm):
    m, d = x.shape
    return pl.pallas_call(
        _ple_body,
        out_shape=jax.ShapeDtypeStruct((m, d), F32),
        grid=(m // tm,),
        in_specs=[
            pl.BlockSpec((tm, d), lambda i: (i, 0)),
            pl.BlockSpec((None, tm, p.shape[-1]), lambda i: (layer, i, 0)),
            pl.BlockSpec((None, d, d), lambda i: (layer, 0, 0)),
            pl.BlockSpec((None, p.shape[-1], d), lambda i: (layer, 0, 0)),
        ],
        out_specs=pl.BlockSpec((tm, d), lambda i: (i, 0)),
        compiler_params=_cparams(("parallel",)),
        name="ple",
    )(x, p, w_gate, w_ple)


COMPRESS_PAGE_STEPS = (32, 16)
CHUNKS_PER_PAGE = PAGE_SIZE // CMP_STRIDE
KV_ROW = NSA_KV_HEADS * 2 * HEAD_DIM


def _cmp_bias_body(pos_ref, w1_ref, o_ref):
    o_ref[...] = jnp.dot(pos_ref[...], w1_ref[...], preferred_element_type=F32, precision=HIGHEST)


def _cmp_bias(cmp_pos, w_cmp1, layer):
    kdim = w_cmp1.shape[2]
    return pl.pallas_call(
        _cmp_bias_body,
        out_shape=jax.ShapeDtypeStruct((2, 1, CMP_HIDDEN), F32),
        grid=(2,),
        in_specs=[pl.BlockSpec((None, None, 1, kdim), lambda c: (layer, c, 0, 0)),
                  pl.BlockSpec((None, None, kdim, CMP_HIDDEN), lambda c: (layer, c, 0, 0))],
        out_specs=pl.BlockSpec((None, 1, CMP_HIDDEN), lambda c: (c, 0, 0)),
        compiler_params=_cparams(("arbitrary",)),
        name="cmp_bias",
    )(cmp_pos, w_cmp1)


def _compress_body(pt_ref, *refs, feature_major, pps):
    del pt_ref
    page_refs = refs[:pps]
    wblk_ref, bias_ref, w2_ref, o_ref, prev_ref = refs[pps:pps + 5]
    rows = pps * CHUNKS_PER_PAGE
    half = 2 * CMP_HIDDEN

    @pl.when(pl.program_id(1) == 0)
    def _():
        prev_ref[...] = jnp.zeros_like(prev_ref)

    first_row = lax.broadcasted_iota(jnp.int32, (rows, half), 0) == 0

    if feature_major:
        tok_ref = refs[pps + 5]
        for k, pr in enumerate(page_refs):
            for h in range(NSA_KV_HEADS):
                tok_ref[k * NSA_KV_HEADS + h] = pr[h].reshape(LANES, PAGE_SIZE).T

        def token_rows(t, h):
            return jnp.concatenate([tok_ref[k * NSA_KV_HEADS + h, pl.ds(t, CHUNKS_PER_PAGE, stride=CMP_STRIDE), :]
                                    for k in range(pps)], axis=0)
    else:
        def token_rows(t, h):
            return jnp.concatenate([pr[h, pl.ds(t, CHUNKS_PER_PAGE, stride=CMP_STRIDE), :]
                                    for pr in page_refs], axis=0)

    for h in range(NSA_KV_HEADS):
        acc = jnp.zeros((rows, 2 * half), F32)
        for tt in range(CMP_STRIDE // 2):
            lhs = jnp.concatenate([token_rows(2 * tt, h), token_rows(2 * tt + 1, h)], axis=1).astype(BF16)
            acc += _dot(lhs, wblk_ref[tt])
        first = acc[:, :half]
        second = acc[:, half:]
        carry = prev_ref[h][SUBLANES - 1:SUBLANES, :]
        shifted = jnp.where(first_row, carry, pltpu.roll(first, 1, 0))
        prev_ref[h] = first[rows - SUBLANES:, :]
        hidden = jax.nn.gelu(shifted + second + bias_ref[...])
        o_ref[:, h * LANES:(h + 1) * LANES] = _dot(hidden.astype(BF16), w2_ref[...])


def _compress(pages, table, layer, wblk, bias, w2blk):
    b, n_pages = table.shape
    pps = next(p for p in COMPRESS_PAGE_STEPS if n_pages % p == 0)
    rows = pps * CHUNKS_PER_PAGE
    feature_major = pages.ndim == 6
    page_block = (None, None) + pages.shape[2:]
    zeros = (0,) * (pages.ndim - 2)

    def page_spec(k):
        return pl.BlockSpec(page_block, lambda bi, i, pt: (pt[bi, i * pps + k], layer) + zeros)

    scratch = [pltpu.VMEM((NSA_KV_HEADS, SUBLANES, 2 * CMP_HIDDEN), F32)]
    if feature_major:
        scratch.append(pltpu.VMEM((pps * NSA_KV_HEADS, PAGE_SIZE, LANES), F32))

    grid_spec = pltpu.PrefetchScalarGridSpec(
        num_scalar_prefetch=1,
        grid=(b, n_pages // pps),
        in_specs=[page_spec(k) for k in range(pps)] + [
            pl.BlockSpec(wblk.shape, lambda bi, i, pt: (0, 0, 0)),
            pl.BlockSpec(bias.shape, lambda bi, i, pt: (0, 0)),
            pl.BlockSpec(w2blk.shape, lambda bi, i, pt: (0, 0)),
        ],
        out_specs=pl.BlockSpec((None, rows, NSA_KV_HEADS * LANES), lambda bi, i, pt: (bi, i, 0)),
        scratch_shapes=scratch,
    )
    return pl.pallas_call(
        functools.partial(_compress_body, feature_major=feature_major, pps=pps),
        out_shape=jax.ShapeDtypeStruct((b, n_pages * CHUNKS_PER_PAGE, NSA_KV_HEADS * LANES), F32),
        grid_spec=grid_spec,
        compiler_params=_cparams(("arbitrary", "arbitrary")),
        name="compress",
    )(table, *([pages] * pps), wblk, bias, w2blk)


def _softmax_groups(s, distf, mask, slopes, rows):
    parts = []
    for g in range(NSA_GROUP):
        sg = s[g * rows:(g + 1) * rows] - slopes[g] * distf
        sg = jnp.where(mask, sg, NEG_INF)
        m = jnp.max(sg, axis=-1, keepdims=True)
        e = jnp.where(mask, jnp.exp(sg - m), 0.0)
        parts.append(e / jnp.maximum(jnp.sum(e, axis=-1, keepdims=True), 1e-30))
    return parts


def _unpack_head(pair, odd):
    lane = lax.broadcasted_iota(jnp.int32, pair.shape, 1)
    return jnp.where(lane < HEAD_DIM, pltpu.roll(pair, HEAD_DIM, 1) if odd else pair, 0.0)


def _stack_q(q_ref, base, scale):
    parts = []
    for g in range(NSA_GROUP):
        c0 = base + (g // 2) * LANES
        parts.append(_unpack_head(q_ref[:, c0:c0 + LANES], g % 2 == 1))
    return (jnp.concatenate(parts, axis=0) * scale).astype(BF16)


def _block_scores(imp, wov_ref, t_pos, n_slc):
    score = jnp.dot(imp, wov_ref[...], preferred_element_type=F32, precision=HIGHEST)
    blk = lax.broadcasted_iota(jnp.int32, score.shape, 1)
    cur = t_pos // SEL_BLOCK
    valid = blk <= cur
    forced = valid & ((blk == 0) | (blk == cur) | (blk == cur - 1))
    score = jnp.where(forced, FORCE, jnp.where(valid, score, -FORCE))
    return score, blk


def _pack_heads(o_ref, heads):
    lane = lax.broadcasted_iota(jnp.int32, heads[0].shape, 1)
    for p in range(NSA_HEADS // 2):
        even = pltpu.roll(heads[2 * p], HEAD_DIM, 1)
        o_ref[:, p * LANES:(p + 1) * LANES] = jnp.where(lane < HEAD_DIM, even, heads[2 * p + 1])


SLC_KEY_CHUNK = 512
LOG2E = 1.4426950408889634
MASK_BIG = 2.0 ** 100
ALIBI2 = [[s * LOG2E for s in row] for row in ALIBI]


def _exp2_softmax(s2, rel_row, bias, slopes2, rows):
    out = []
    for g in range(NSA_GROUP):
        lg = s2[g * rows:(g + 1) * rows] + (slopes2[g] * rel_row + bias)
        e = jnp.exp2(lg - jnp.max(lg, axis=-1, keepdims=True))
        out.append((e, jnp.sum(e, axis=-1, keepdims=True)))
    return out


def _nsa_prompt_body(q_ref, misc_ref, kc_ref, kvs_ref, kvw_ref, wovt_ref, o_ref, *, seq):
    qb = pl.program_id(1)
    t0 = qb * Q_BLOCK
    t0f = t0.astype(F32)
    n_slc = -(-seq // SEL_BLOCK)
    n_sel = min(N_SEL, n_slc)
    n_cr = kc_ref.shape[0]
    t_col = (t0 + lax.broadcasted_iota(jnp.int32, (Q_BLOCK, 1), 0)).astype(F32)
    gates = jax.nn.sigmoid(misc_ref[...])

    r_idx = lax.broadcasted_iota(jnp.int32, (1, n_cr), 1)
    end_c = (r_idx * CMP_STRIDE + (CMP_STRIDE - 1)).astype(F32)
    bias_c = jnp.minimum(t_col - end_c, 0.0) * MASK_BIG + jnp.where(r_idx >= 1, 0.0, -MASK_BIG)
    rel_c = end_c - t0f
    row_live = jnp.where(t_col >= CMP_BLOCK - 1, 1.0, 0.0)
    band = WINDOW + Q_BLOCK
    w0 = pl.multiple_of(jnp.maximum(t0 - WINDOW, 0), Q_BLOCK)
    pos_w = (w0 + lax.broadcasted_iota(jnp.int32, (1, band), 1)).astype(F32)
    d_w = t_col - pos_w
    bias_w = (jnp.minimum(d_w, 0.0) + jnp.minimum((WINDOW - 1.0) - d_w, 0.0)) * MASK_BIG
    rel_w = pos_w - t0f

    blk_t = lax.broadcasted_iota(jnp.int32, (n_slc, Q_BLOCK), 0)
    cur_t = (t0 + lax.broadcasted_iota(jnp.int32, (1, Q_BLOCK), 1)) // SEL_BLOCK
    valid_t = blk_t <= cur_t
    forced_t = valid_t & ((blk_t == 0) | (blk_t == cur_t) | (blk_t == cur_t - 1))
    tm1 = t_col - 1.0

    kv_heads = range(NSA_KV_HEADS)
    lane_sl = [slice(h * LANES, (h + 1) * LANES) for h in kv_heads]
    qhs = [_stack_q(q_ref, h * Q_HEAD_COLS, HEAD_DIM ** -0.5 * LOG2E) for h in kv_heads]

    o_cs, scores = [], []
    for h in kv_heads:
        kc = kc_ref[:, lane_sl[h]].astype(BF16)
        sm_c = _exp2_softmax(_dot_nt(qhs[h], kc), rel_c, bias_c, ALIBI2[h], Q_BLOCK)
        p_c = [e * (row_live / l) for (e, l) in sm_c]
        o_cs.append(_dot(jnp.concatenate(p_c, axis=0).astype(BF16), kc))
        imp = p_c[0] + p_c[1] + p_c[2] + p_c[3]
        score_t = lax.dot_general(wovt_ref[...], imp, (((1,), (1,)), ((), ())),
                                  preferred_element_type=F32, precision=HIGHEST)[:n_slc]
        scores.append(jnp.where(forced_t, FORCE, jnp.where(valid_t, score_t, -FORCE)))

    def ranked():
        out = []
        for score_t in scores:
            rank = jnp.zeros(score_t.shape, F32)
            for i in range(n_slc):
                ci = score_t[i:i + 1, :]
                tie = jnp.where(blk_t > i, 1.0, 0.0)
                rank += jnp.where(ci > score_t, 1.0, jnp.where(ci == score_t, tie, 0.0))
            out.append(jnp.where(rank < n_sel, MASK_BIG, 0.0))
        return tuple(out)

    sel_ts = lax.cond(t0 + Q_BLOCK > n_sel * SEL_BLOCK, ranked,
                      lambda: tuple(jnp.where(valid_t, MASK_BIG, 0.0) for _ in kv_heads))
    sels = [jnp.concatenate([s, jnp.zeros((LANES - n_slc, Q_BLOCK), F32)], axis=0).T.astype(BF16)
            for s in sel_ts]

    def chunk(c, carry):
        k0 = pl.multiple_of(c * SLC_KEY_CHUNK, SLC_KEY_CHUNK)
        pos = k0 + lax.broadcasted_iota(jnp.int32, (1, SLC_KEY_CHUNK), 1)
        posf = pos.astype(F32)
        expand = jnp.where(
            (pos // SEL_BLOCK) == lax.broadcasted_iota(jnp.int32, (LANES, SLC_KEY_CHUNK), 0),
            1.0, 0.0).astype(BF16)
        causal = jnp.minimum(tm1 - posf, -1.0) * MASK_BIG
        rel = posf - t0f
        new = []
        for h in kv_heads:
            kv = kvs_ref[pl.ds(k0, SLC_KEY_CHUNK), lane_sl[h]].astype(BF16)
            s2 = _dot_nt(qhs[h], kv)
            bias = _dot(sels[h], expand) + causal
            out = []
            for g in range(NSA_GROUP):
                m_old, l_old, a_old = carry[h][g]
                lg = s2[g * Q_BLOCK:(g + 1) * Q_BLOCK] + (ALIBI2[h][g] * rel + bias)
                m_new = jnp.maximum(m_old, jnp.max(lg, axis=-1, keepdims=True))
                alpha = jnp.exp2(m_old - m_new)
                e = jnp.exp2(lg - m_new)
                l_new = alpha * l_old + jnp.sum(e, axis=-1, keepdims=True)
                a_new = alpha * a_old + _dot(e.astype(BF16), kv)
                out.append((m_new, l_new, a_new))
            new.append(tuple(out))
        return tuple(new)

    init = tuple(tuple((jnp.full((Q_BLOCK, 1), NEG_INF, F32), jnp.zeros((Q_BLOCK, 1), F32),
                        jnp.zeros((Q_BLOCK, LANES), F32)) for _ in range(NSA_GROUP)) for _ in kv_heads)
    n_chunks = (t0 + Q_BLOCK + SLC_KEY_CHUNK - 1) // SLC_KEY_CHUNK
    fin = lax.fori_loop(0, n_chunks, chunk, init)

    heads = []
    for h in kv_heads:
        kvw = kvw_ref[pl.ds(w0, band), lane_sl[h]].astype(BF16)
        sm_w = _exp2_softmax(_dot_nt(qhs[h], kvw), rel_w, bias_w, ALIBI2[h], Q_BLOCK)
        o_w = _dot(jnp.concatenate([e for (e, _) in sm_w], axis=0).astype(BF16), kvw)
        for g in range(NSA_GROUP):
            col = h * NSA_GROUP + g
            rows = slice(g * Q_BLOCK, (g + 1) * Q_BLOCK)
            heads.append(gates[:, col:col + 1] * o_cs[h][rows]
                         + (gates[:, NSA_HEADS + col:NSA_HEADS + col + 1] / fin[h][g][1]) * fin[h][g][2]
                         + (gates[:, 2 * NSA_HEADS + col:2 * NSA_HEADS + col + 1] / sm_w[g][1]) * o_w[rows])
    _pack_heads(o_ref, heads)


def _nsa_prompt(u, kc, wov):
    b, t, _ = u.shape
    assert t % Q_BLOCK == 0 and t >= WINDOW + Q_BLOCK and t % SLC_KEY_CHUNK == 0
    n_slc = -(-t // SEL_BLOCK)
    assert n_slc % SUBLANES == 0 and n_slc <= LANES
    return pl.pallas_call(
        functools.partial(_nsa_prompt_body, seq=t),
        out_shape=jax.ShapeDtypeStruct((b, t, NSA_HEADS * HEAD_DIM), F32),
        grid=(b, t // Q_BLOCK),
        in_specs=[
            pl.BlockSpec((None, Q_BLOCK, NSA_HEADS * HEAD_DIM), lambda bi, i: (bi, i, 0)),
            pl.BlockSpec((None, Q_BLOCK, LANES), lambda bi, i: (bi, i, BLK_MISC)),
            pl.BlockSpec((None,) + kc.shape[1:], lambda bi, i: (bi, 0, 0)),
            pl.BlockSpec((None, t, 4 * LANES), lambda bi, i: (bi, 0, BLK_KVS)),
            pl.BlockSpec((None, t, 4 * LANES), lambda bi, i: (bi, 0, BLK_KVW)),
            pl.BlockSpec(wov.shape, lambda bi, i: (0, 0)),
        ],
        out_specs=pl.BlockSpec((None, Q_BLOCK, NSA_HEADS * HEAD_DIM), lambda bi, i: (bi, i, 0)),
        compiler_params=_cparams(("parallel", "arbitrary")),
        name="nsa_prompt",
    )(u, u, kc, u, u, wov)


def _cmp_select_body(q_ref, kc_ref, wov_ref, oc_ref, sel_ref, *, past_len, t_real):
    tp = q_ref.shape[0]
    n_cr = kc_ref.shape[0]
    n_slc = -(-(past_len + t_real) // SEL_BLOCK)
    t_pos = past_len + lax.broadcasted_iota(jnp.int32, (tp, 1), 0)
    for h in range(NSA_KV_HEADS):
        qh = _stack_q(q_ref, h * Q_HEAD_COLS, HEAD_DIM ** -0.5)
        kc = kc_ref[:, h * LANES:(h + 1) * LANES].astype(BF16)
        r_idx = lax.broadcasted_iota(jnp.int32, (1, n_cr), 1)
        dist = t_pos - (r_idx * CMP_STRIDE + (CMP_STRIDE - 1))
        mask = (r_idx >= 1) & (dist >= 0)
        p_c = _softmax_groups(_dot_nt(qh, kc), dist.astype(F32), mask, ALIBI[h], tp)
        oc_ref[h] = _dot(jnp.concatenate(p_c, axis=0).astype(BF16), kc)
        imp = p_c[0] + p_c[1] + p_c[2] + p_c[3]
        score, blk = _block_scores(imp, wov_ref, t_pos, n_slc)
        work = jnp.where(blk < n_slc, score, -3e38)
        blkf = blk.astype(F32)
        picked = jnp.zeros((tp, LANES), jnp.int32)
        lane = lax.broadcasted_iota(jnp.int32, (tp, LANES), 1)
        for k in range(N_SEL):
            m = jnp.max(work, axis=-1, keepdims=True)
            idx = jnp.min(jnp.where(work == m, blkf, 3e38), axis=-1, keepdims=True)
            picked = jnp.where(lane == k, idx.astype(jnp.int32), picked)
            work = jnp.where(blkf == idx, -3e38, work)
        sel_ref[h] = picked


def _cmp_select(u_s, kc, wov, past_len, t_real):
    b, tp, _ = u_s.shape
    assert -(-(past_len + t_real) // SEL_BLOCK) >= N_SEL
    return pl.pallas_call(
        functools.partial(_cmp_select_body, past_len=past_len, t_real=t_real),
        out_shape=(jax.ShapeDtypeStruct((b, NSA_KV_HEADS, NSA_GROUP * tp, LANES), F32),
                   jax.ShapeDtypeStruct((b, NSA_KV_HEADS, tp, LANES), jnp.int32)),
        grid=(b,),
        in_specs=[
            pl.BlockSpec((None, tp, NSA_HEADS * HEAD_DIM), lambda bi: (bi, 0, 0)),
            pl.BlockSpec((None,) + kc.shape[1:], lambda bi: (bi, 0, 0)),
            pl.BlockSpec(wov.shape, lambda bi: (0, 0)),
        ],
        out_specs=(pl.BlockSpec((None, NSA_KV_HEADS, NSA_GROUP * tp, LANES), lambda bi: (bi, 0, 0, 0)),
                   pl.BlockSpec((None, NSA_KV_HEADS, tp, LANES), lambda bi: (bi, 0, 0, 0))),
        compiler_params=_cparams(("parallel",)),
        name="cmp_select",
    )(u_s, kc, wov)


def _slc_sample_body(pidx_ref, pt_ref, *refs, past_len):
    del pidx_ref, pt_ref
    blk_refs = refs[:N_SEL]
    pos_ref, ok_ref, q_ref, kvn_ref, o_ref = refs[N_SEL:]
    h, t = pl.program_id(1), pl.program_id(2)
    tp = kvn_ref.shape[0]
    q_t = q_ref[pl.ds(t, 1), :]
    row = lax.broadcasted_iota(jnp.int32, (SUBLANES, 1), 0)
    q8 = jnp.zeros((SUBLANES, LANES), F32)
    slope = jnp.zeros((SUBLANES, 1), F32)
    for g in range(NSA_GROUP):
        c0 = (g // 2) * LANES
        q8 = jnp.where(row == g, _unpack_head(q_t[:, c0:c0 + LANES], g % 2 == 1), q8)
        slope_g = jnp.where(h == 0, ALIBI[0][g], jnp.where(h == 1, ALIBI[1][g],
                                                           jnp.where(h == 2, ALIBI[2][g], ALIBI[3][g])))
        slope = jnp.where(row == g, slope_g, slope)
    qh = (q8 * HEAD_DIM ** -0.5).astype(BF16)
    t_pos = past_len + t

    kv_t = jnp.concatenate([r[...].reshape(2 * HEAD_DIM, PAGE_SIZE) for r in blk_refs], axis=1).astype(BF16)
    d = t_pos - pos_ref[...]
    msk = (ok_ref[...] > 0) & (d >= 0)
    kvn = jnp.concatenate([kvn_ref[...], jnp.zeros((LANES - tp, LANES), F32)], axis=0).astype(BF16)
    idx_n = lax.broadcasted_iota(jnp.int32, (1, LANES), 1)
    dn = t - idx_n
    mskn = (idx_n < tp) & (dn >= 0)
    s_all = _dot(qh, kv_t)
    s_new = _dot_nt(qh, kvn)
    sg = jnp.where(msk, s_all - slope * d.astype(F32), NEG_INF)
    sn = jnp.where(mskn, s_new - slope * dn.astype(F32), NEG_INF)
    m = jnp.maximum(jnp.max(sg, axis=-1, keepdims=True), jnp.max(sn, axis=-1, keepdims=True))
    e = jnp.where(msk, jnp.exp(sg - m), 0.0)
    en = jnp.where(mskn, jnp.exp(sn - m), 0.0)
    l = jnp.sum(e, axis=-1, keepdims=True) + jnp.sum(en, axis=-1, keepdims=True)
    o_ref[...] = (_dot_nt(e.astype(BF16), kv_t) + _dot(en.astype(BF16), kvn)) / jnp.maximum(l, 1e-30)


def _slc_sample(u_s, cache, table, sel, layer, past_len, t_real):
    b, tp, _ = u_s.shape
    n_past_blk = past_len // SEL_BLOCK
    per_page = PAGE_SIZE // SEL_BLOCK
    assert past_len % SEL_BLOCK == 0 and t_real <= SEL_BLOCK and PAGE_SIZE % SEL_BLOCK == 0

    page_idx = (jnp.minimum(sel, n_past_blk - 1) // per_page).reshape(-1)
    blk_l = jnp.repeat(sel, PAGE_SIZE, axis=-1)
    in_page = jnp.asarray(np.arange(N_SEL * PAGE_SIZE) % PAGE_SIZE, jnp.int32)
    pos = ((blk_l // per_page) * PAGE_SIZE + in_page)[:, :, :, None, :]
    ok = ((blk_l < n_past_blk) & (in_page // SEL_BLOCK == blk_l % per_page)).astype(jnp.int32)[:, :, :, None, :]

    def blk_spec(k):
        def imap(bi, h, t, pidx, pt):
            return (pt[bi, pidx[((bi * NSA_KV_HEADS + h) * t_real + t) * N_SEL + k]], layer, h, 0, 0, 0)
        return pl.BlockSpec((None, None, None, 2, HEAD_DIM, PAGE_SIZE), imap)

    lane_spec = pl.BlockSpec((None, None, None, 1, N_SEL * PAGE_SIZE), lambda bi, h, t, pidx, pt: (bi, h, t, 0, 0))
    grid_spec = pltpu.PrefetchScalarGridSpec(
        num_scalar_prefetch=2,
        grid=(b, NSA_KV_HEADS, t_real),
        in_specs=[blk_spec(k) for k in range(N_SEL)] + [
            lane_spec, lane_spec,
            pl.BlockSpec((None, tp, Q_HEAD_COLS), lambda bi, h, t, pidx, pt: (bi, 0, h)),
            pl.BlockSpec((None, tp, LANES), lambda bi, h, t, pidx, pt: (bi, 0, BLK_KVS * 4 + h)),
        ],
        out_specs=pl.BlockSpec((None, None, None, SUBLANES, LANES), lambda bi, h, t, pidx, pt: (bi, h, t, 0, 0)),
    )
    return pl.pallas_call(
        functools.partial(_slc_sample_body, past_len=past_len),
        out_shape=jax.ShapeDtypeStruct((b, NSA_KV_HEADS, t_real, SUBLANES, LANES), F32),
        grid_spec=grid_spec,
        compiler_params=_cparams(("parallel", "parallel", "arbitrary")),
        name="slc_sample",
    )(page_idx, table, *([cache] * N_SEL), pos, ok, u_s, u_s)


def _win_combine_body(q_ref, misc_ref, win_ref, kvn_ref, oc_ref, os_ref, o_ref, *, past_len):
    tp = q_ref.shape[0]
    wb = win_ref.shape[0]
    t_pos = past_len + lax.broadcasted_iota(jnp.int32, (tp, 1), 0)
    gates = jax.nn.sigmoid(misc_ref[...])
    n_keys = wb + LANES
    idx = lax.broadcasted_iota(jnp.int32, (1, n_keys), 1)
    d = t_pos - (past_len - wb + idx)
    mask = (idx < wb + tp) & (d >= 0) & (d < WINDOW)
    heads = []
    for h in range(NSA_KV_HEADS):
        sl = slice(h * LANES, (h + 1) * LANES)
        qh = _stack_q(q_ref, h * Q_HEAD_COLS, HEAD_DIM ** -0.5)
        kv = jnp.concatenate([win_ref[:, sl], kvn_ref[:, sl], jnp.zeros((LANES - tp, LANES), F32)],
                             axis=0).astype(BF16)
        p_w = _softmax_groups(_dot_nt(qh, kv), d.astype(F32), mask, ALIBI[h], tp)
        o_w = _dot(jnp.concatenate(p_w, axis=0).astype(BF16), kv)
        o_c = oc_ref[h]
        t_live = os_ref.shape[1]
        for g in range(NSA_GROUP):
            col = h * NSA_GROUP + g
            rows = slice(g * tp, (g + 1) * tp)
            o_s = jnp.concatenate([os_ref[h, tt, g:g + 1, :] for tt in range(t_live)]
                                  + [jnp.zeros((tp - t_live, LANES), F32)], axis=0)
            heads.append(gates[:, col:col + 1] * o_c[rows]
                         + gates[:, NSA_HEADS + col:NSA_HEADS + col + 1] * o_s
                         + gates[:, 2 * NSA_HEADS + col:2 * NSA_HEADS + col + 1] * o_w[rows])
    _pack_heads(o_ref, heads)


def _win_combine(u_s, win, o_c, o_s, layer, past_len):
    b, tp, _ = u_s.shape
    wb = win.shape[2]
    return pl.pallas_call(
        functools.partial(_win_combine_body, past_len=past_len),
        out_shape=jax.ShapeDtypeStruct((b, tp, NSA_HEADS * HEAD_DIM), F32),
        grid=(b,),
        in_specs=[
            pl.BlockSpec((None, tp, NSA_HEADS * HEAD_DIM), lambda bi: (bi, 0, 0)),
            pl.BlockSpec((None, tp, LANES), lambda bi: (bi, 0, BLK_MISC)),
            pl.BlockSpec((None, None, wb, 4 * LANES), lambda bi: (bi, layer, 0, 0)),
            pl.BlockSpec((None, tp, 4 * LANES), lambda bi: (bi, 0, BLK_KVW)),
            pl.BlockSpec((None,) + o_c.shape[1:], lambda bi: (bi, 0, 0, 0)),
            pl.BlockSpec((None,) + o_s.shape[1:], lambda bi: (bi, 0, 0, 0, 0)),
        ],
        out_specs=pl.BlockSpec((None, tp, NSA_HEADS * HEAD_DIM), lambda bi: (bi, 0, 0)),
        compiler_params=_cparams(("parallel",)),
        name="win_combine",
    )(u_s, u_s, win, u_s, o_c, o_s)


def _cumsum_rows(x):
    n = x.shape[0]
    row = lax.broadcasted_iota(jnp.int32, x.shape, 0)
    shift = 1
    while shift < n:
        x = x + jnp.where(row >= shift, pltpu.roll(x, shift, 0), 0.0)
        shift *= 2
    return x


def _gla_body(gqk_ref, gv_ref, gog_ref, misc_ref, wdec_ref, bdec_ref, gn_ref, s0_ref,
              o_ref, sout_ref, st_ref, *, c_real, n_sub):
    ci = pl.program_id(1)

    @pl.when(ci == 0)
    def _():
        for h in range(GLA_HEADS):
            st_ref[h] = s0_ref[h].T

    cp = gqk_ref.shape[0] // n_sub
    for j in range(n_sub):
        _gla_chunk(gqk_ref, gv_ref, gog_ref, misc_ref, wdec_ref, bdec_ref, gn_ref, o_ref, st_ref,
                   slice(j * cp, (j + 1) * cp), cp, c_real)

    @pl.when(ci == pl.num_programs(1) - 1)
    def _():
        for h in range(GLA_HEADS):
            sout_ref[h] = st_ref[h].T[:GLA_DK, :]


def _gla_chunk(gqk_ref, gv_ref, gog_ref, misc_ref, wdec_ref, bdec_ref, gn_ref, o_ref, st_ref, rs, cp, c_real):
    k_base = GLA_HEADS * GLA_DK
    sb = min(16, cp)
    row = lax.broadcasted_iota(jnp.int32, (cp, 1), 0)
    live = row < c_real
    x = jnp.dot(misc_ref[rs, :], wdec_ref[...], preferred_element_type=F32, precision=HIGHEST) + bdec_ref[...]
    log_a = (jnp.minimum(x, 0.0) - jnp.log1p(jnp.exp(-jnp.abs(x)))) / GLA_GATE_TAU
    b_all = _cumsum_rows(jnp.where(live, log_a, 0.0))
    for h in range(GLA_HEADS):
        sl = slice(h * LANES, (h + 1) * LANES)
        pair = slice((h // 2) * LANES, (h // 2 + 1) * LANES)
        q = _unpack_head(gqk_ref[rs, pair], h % 2 == 1) * GLA_DK ** -0.5
        k_pair = slice(k_base + (h // 2) * LANES, k_base + (h // 2 + 1) * LANES)
        k = jnp.where(live, _unpack_head(gqk_ref[rs, k_pair], h % 2 == 1), 0.0)
        v = jnp.where(live, gv_ref[rs, sl], 0.0)
        b = b_all[:, sl]
        kb = k.astype(BF16)
        st = st_ref[h]
        attn_rows = []
        for i in range(cp // sb):
            qi = q[i * sb:(i + 1) * sb]
            bi = b[i * sb:(i + 1) * sb]
            sub_row = lax.broadcasted_iota(jnp.int32, (sb, 1), 0)
            ys = []
            for s_loc in range(sb):
                bs = b[i * sb + s_loc:i * sb + s_loc + 1]
                ys.append(qi * jnp.exp(jnp.where(sub_row >= s_loc, bi - bs, NEG_INF)))
            z = _dot_nt(jnp.concatenate(ys, axis=0).astype(BF16), kb)
            lane = lax.broadcasted_iota(jnp.int32, (sb, cp), 1)
            a_i = jnp.zeros((sb, cp), F32)
            for s_loc in range(sb):
                a_i += jnp.where(lane == i * sb + s_loc, z[s_loc * sb:(s_loc + 1) * sb], 0.0)
            if i > 0:
                ref_b = b[i * sb - 1:i * sb]
                qt = qi * jnp.exp(bi - ref_b)
                kt = k * jnp.exp(jnp.where(row < i * sb, ref_b - b, NEG_INF))
                a_i += _dot_nt(qt.astype(BF16), kt.astype(BF16))
            attn_rows.append(a_i)
        attn = jnp.concatenate(attn_rows, axis=0) if len(attn_rows) > 1 else attn_rows[0]
        o = _dot(attn.astype(BF16), v.astype(BF16))
        o += _dot_nt((q * jnp.exp(b)).astype(BF16), st.astype(BF16))
        b_last = b[c_real - 1:c_real]
        kd = k * jnp.exp(b_last - b)
        st_ref[h] = jnp.exp(b_last) * st + _dot(v.T.astype(BF16), kd.astype(BF16))
        o_ref[rs, sl] = _rms(o, gn_ref[...]) * jax.nn.silu(gog_ref[rs, sl])


def _recurrent_body(gqk_ref, gv_ref, gog_ref, misc_ref, wdec_ref, bdec_ref, gn_ref, s0_ref,
                    rq_ref, rk_ref, rv_ref, rg_ref, rn_ref, r0_ref,
                    og_ref, sout_ref, or_ref, rout_ref, st_ref, rt_ref, *, c_real, n_sub):
    _gla_body(gqk_ref, gv_ref, gog_ref, misc_ref, wdec_ref, bdec_ref, gn_ref, s0_ref,
              og_ref, sout_ref, st_ref, c_real=c_real, n_sub=n_sub)
    ret_real = c_real if n_sub == 1 else rq_ref.shape[0]
    _ret_body(rq_ref, rk_ref, rv_ref, rg_ref, rn_ref, r0_ref, or_ref, rout_ref, rt_ref, c_real=ret_real)


def _recurrent(u, s0, r0, w_dec, b_dec, gla_norm, ret_norm, layer, gla_chunk, c_real, n_sub=1):
    b, t, _ = u.shape
    cp = gla_chunk * n_sub
    assert t % cp == 0 and (n_sub == 1 or c_real == gla_chunk)
    blk = lambda idx: pl.BlockSpec((None, cp, 4 * LANES), lambda bi, i: (bi, i, idx))
    per_b = lambda shape: pl.BlockSpec((None,) + shape, lambda bi, i: (bi, 0, 0, 0))
    per_layer = lambda shape: pl.BlockSpec((None,) + shape, lambda bi, i: (layer, 0, 0))
    return pl.pallas_call(
        functools.partial(_recurrent_body, c_real=c_real, n_sub=n_sub),
        out_shape=(jax.ShapeDtypeStruct((b, t, GLA_HEADS * GLA_DV), F32),
                   jax.ShapeDtypeStruct((b, GLA_HEADS, GLA_DK, GLA_DV), F32),
                   jax.ShapeDtypeStruct((b, t, RET_HEADS * RET_DV), F32),
                   jax.ShapeDtypeStruct((b, RET_HEADS, RET_DK, RET_DV), F32)),
        grid=(b, t // cp),
        in_specs=[
            blk(BLK_GQK), blk(BLK_GV), blk(BLK_GOG),
            pl.BlockSpec((None, cp, LANES), lambda bi, i: (bi, i, BLK_MISC)),
            per_layer((LANES, 4 * LANES)), per_layer((1, 4 * LANES)), per_layer((1, GLA_DV)),
            per_b((GLA_HEADS, LANES, GLA_DV)),
            blk(BLK_RQ), blk(BLK_RK), blk(BLK_RV), blk(BLK_RG),
            per_layer((1, RET_HEADS * RET_DV)),
            per_b((RET_HEADS, RET_DK, RET_DV)),
        ],
        out_specs=(pl.BlockSpec((None, cp, GLA_HEADS * GLA_DV), lambda bi, i: (bi, i, 0)),
                   per_b((GLA_HEADS, GLA_DK, GLA_DV)),
                   pl.BlockSpec((None, cp, RET_HEADS * RET_DV), lambda bi, i: (bi, i, 0)),
                   per_b((RET_HEADS, RET_DK, RET_DV))),
        scratch_shapes=[pltpu.VMEM((GLA_HEADS, GLA_DV, LANES), F32),
                        pltpu.VMEM((RET_HEADS, RET_DV, RET_DK), F32)],
        compiler_params=_cparams(("parallel", "arbitrary")),
        name="recurrent",
    )(u, u, u, u, w_dec, b_dec, gla_norm, s0, u, u, u, u, ret_norm, r0)


def _ret_body(rq_ref, rk_ref, rv_ref, rg_ref, gn_ref, r0_ref, o_ref, rout_ref, rt_ref, *, c_real):
    ci = pl.program_id(1)
    cp = rq_ref.shape[0]

    @pl.when(ci == 0)
    def _():
        for h in range(RET_HEADS):
            rt_ref[h] = r0_ref[h].T

    row = lax.broadcasted_iota(jnp.int32, (cp, 1), 0)
    live = row < c_real
    rowf = row.astype(F32)
    rel = rowf - lax.broadcasted_iota(jnp.int32, (1, cp), 1).astype(F32)
    for h in range(RET_HEADS):
        sl = slice(h * LANES, (h + 1) * LANES)
        lg = RET_LOG_GAMMA[h]
        q = rq_ref[:, sl]
        k = jnp.where(live, rk_ref[:, sl] * RET_DK ** -0.5, 0.0)
        v = jnp.where(live, rv_ref[:, sl], 0.0)
        decay = jnp.where(rel >= 0, jnp.exp(jnp.maximum(rel, 0.0) * lg), 0.0)
        attn = _dot_nt(q.astype(BF16), k.astype(BF16)) * decay
        o = _dot(attn.astype(BF16), v.astype(BF16))
        rt = rt_ref[h]
        o += _dot_nt((q * jnp.exp((rowf + 1.0) * lg)).astype(BF16), rt.astype(BF16))
        kd = k * jnp.exp((c_real - 1.0 - rowf) * lg)
        rt_ref[h] = math.exp(c_real * lg) * rt + _dot(v.T.astype(BF16), kd.astype(BF16))
        mu = jnp.mean(o, axis=-1, keepdims=True)
        var = jnp.mean(jnp.square(o - mu), axis=-1, keepdims=True)
        o_ref[:, sl] = (o - mu) * lax.rsqrt(var + EPS) * gn_ref[:, sl] * jax.nn.silu(rg_ref[:, sl])

    @pl.when(ci == pl.num_programs(1) - 1)
    def _():
        for h in range(RET_HEADS):
            rout_ref[h] = rt_ref[h].T


def _pad_heads(w, n_heads):
    lead = w.shape[:-1]
    w = w.reshape(lead + (n_heads, HEAD_DIM))
    w = jnp.pad(w, [(0, 0)] * len(lead) + [(0, 0), (0, LANES - HEAD_DIM)])
    return w.reshape(lead + (n_heads * LANES,))


def _prep_w_in(w_in):
    cuts = np.cumsum((0,) + IN_SIZES)
    nq, kvc, kvs, kvw, ng, gq, gk, gv, glr, gog, rq, rk, rv, rg = [
        w_in[..., cuts[i]:cuts[i + 1]] for i in range(len(IN_SIZES))]
    misc = jnp.concatenate([ng, glr], axis=-1)
    misc = jnp.pad(misc, [(0, 0), (0, 0), (0, LANES - misc.shape[-1])])
    w = jnp.concatenate([nq, kvc, kvs, kvw, gq, gk, gv, gog, rq, rk, rv, rg, misc], axis=-1)
    assert w.shape[-1] == (BLK_MISC + 1) * LANES and U_WIDTH % MXU_COLS == 0
    w = jnp.pad(w, [(0, 0), (0, 0), (0, U_WIDTH - w.shape[-1])])
    return w.astype(BF16)


def _prep_cmp_weights(w_cmp1, w_cmp2):
    depth = w_cmp1.shape[0]
    half = CMP_STRIDE * HEAD_DIM
    w1 = w_cmp1.reshape(depth, 2, 2, CMP_STRIDE, HEAD_DIM, CMP_HIDDEN)
    w1 = w1.transpose(0, 3, 1, 4, 2, 5)
    z = jnp.zeros_like(w1[:, :, 0])
    k_rows = jnp.stack([w1[:, :, 0], z], axis=4)
    v_rows = jnp.stack([z, w1[:, :, 1]], axis=4)
    wblk = jnp.concatenate([k_rows, v_rows], axis=2)
    wblk = wblk.reshape(depth, CMP_STRIDE // 2, 4 * HEAD_DIM, 4 * CMP_HIDDEN).astype(BF16)
    del half
    z2 = jnp.zeros_like(w_cmp2[:, 0])
    w2blk = jnp.concatenate([jnp.concatenate([w_cmp2[:, 0], z2], axis=-1),
                             jnp.concatenate([z2, w_cmp2[:, 1]], axis=-1)], axis=1).astype(BF16)
    return wblk, w2blk


def _overlap_matrix(n_rows, n_slc, n_cols):
    ratio = SEL_BLOCK // CMP_STRIDE
    w = np.zeros((n_rows, n_cols), np.float32)
    for j in range(n_slc):
        for k, wk in enumerate(OVERLAP_W):
            if ratio * j + k < n_rows:
                w[ratio * j + k, j] = wk
    return jnp.asarray(w)


def kernel(x_prompt, x_sample, cache_cmp_kv, cache_slc_kv, cache_win_kv, state_gla, state_ret, page_table,
           p_prompt, p_sample, norm_pre, norm_post, w_ffn_gate, w_ffn_up, w_ffn_down, w_in, w_out,
           w_cmp1, w_cmp2, cmp_pos, nsa_norm, w_gla_decay, b_gla_decay, gla_norm, ret_norm, w_ple, w_ple_gate):
    depth = w_in.shape[0]
    bp, seq, d_model = x_prompt.shape
    bs, t_real, _ = x_sample.shape
    n_pool = cache_cmp_kv.shape[0]
    n_pages = page_table.shape[1]
    past_len = n_pages * PAGE_SIZE
    tp = -(-t_real // SUBLANES) * SUBLANES
    kvw = NSA_KV_HEADS * 2 * HEAD_DIM

    w_in_r = _prep_w_in(w_in)
    w_out_b = w_out.astype(BF16)
    w_ple_b = w_ple.astype(BF16)
    w_gate_b = w_ple_gate.astype(BF16)
    wblk, w2blk = _prep_cmp_weights(w_cmp1, w_cmp2)
    g_pre = norm_pre.reshape(depth, 3, 1, d_model)
    g_post = norm_post.reshape(depth, 3, 1, d_model)
    nsa_g = nsa_norm.reshape(depth, 1, -1)
    gla_g = gla_norm.reshape(depth, 1, -1)
    ret_g = ret_norm.reshape(depth, 1, -1)
    w_dec = jnp.pad(_pad_heads(w_gla_decay, GLA_HEADS),
                    [(0, 0), (MISC_GLR, LANES - MISC_GLR - GLA_GATE_RANK), (0, 0)])
    b_dec = _pad_heads(b_gla_decay, GLA_HEADS).reshape(depth, 1, -1)
    pos_flat = cmp_pos.reshape(depth, 2, 1, CMP_BLOCK * HEAD_DIM)

    cmp_pages = cache_cmp_kv.transpose(0, 1, 3, 4, 5, 2)
    slc_rows = cache_slc_kv.transpose(0, 1, 3, 4, 5, 2)
    win_rows = cache_win_kv.reshape(bs, depth, -1, kvw)
    wb = win_rows.shape[2]

    n_cr_p = seq // CMP_STRIDE
    wov_p = _overlap_matrix(n_cr_p, -(-seq // SEL_BLOCK), LANES).T
    n_cr_s = past_len // CMP_STRIDE
    n_slc_s = -(-(past_len + t_real) // SEL_BLOCK)
    wov_s = _overlap_matrix(n_cr_s, n_slc_s, -(-n_slc_s // LANES) * LANES)
    table_p = jnp.arange(bp * (seq // PAGE_SIZE), dtype=jnp.int32).reshape(bp, seq // PAGE_SIZE)

    xp = x_prompt.reshape(bp * seq, d_model)
    xs = jnp.pad(x_sample, ((0, 0), (0, tp - t_real), (0, 0))).reshape(bs * tp, d_model)
    pp = p_prompt.reshape(depth, bp * seq, -1)
    ps = jnp.pad(p_sample, ((0, 0), (0, 0), (0, tp - t_real), (0, 0))).reshape(depth, bs * tp, -1)
    s0_gla_p = jnp.zeros((bp, GLA_HEADS, LANES, GLA_DV), F32)
    s0_ret_p = jnp.zeros((bp, RET_HEADS, RET_DK, RET_DV), F32)
    s0_gla_s = jnp.pad(state_gla, ((0, 0), (0, 0), (0, 0), (0, LANES - GLA_DK), (0, 0)))

    tm_p = 1024
    tm_s = bs * tp
    st_p = [[] for _ in range(5)]
    st_s = [[] for _ in range(5)]
    for l in range(depth):
        bias = _cmp_bias(pos_flat, w_cmp1, l).reshape(1, 2 * CMP_HIDDEN)

        xs, wg_a, wu_a, wd_a = _ffn_cast(xs, g_pre, g_post, w_ffn_gate, w_ffn_up, w_ffn_down, l, 0, 0)

        xp = _ffn(xp, g_pre, g_post, wg_a, wu_a, wd_a, l, 0, tm_p // 2)
        u2 = _proj_in(xp, g_pre, w_in_r, l, tm_p)
        u = u2.reshape(bp, seq, U_WIDTH)
        kv_c = u[..., BLK_KVC * kvw:(BLK_KVC + 1) * kvw]
        kv_s = u[..., BLK_KVS * kvw:(BLK_KVS + 1) * kvw]
        kv_w = u[..., BLK_KVW * kvw:(BLK_KVW + 1) * kvw]
        kc_pages = kv_c.reshape(bp * (seq // PAGE_SIZE), 1, PAGE_SIZE, NSA_KV_HEADS, LANES).transpose(0, 1, 3, 2, 4)
        kc = _compress(kc_pages, table_p, 0,
                       wblk[l], bias, w2blk[l])
        o_nsa = _nsa_prompt(u, kc, wov_p)
        gla_out, s_g, ret_out, s_r = _recurrent(u, s0_gla_p, s0_ret_p, w_dec, b_dec, gla_g, ret_g, l,
                                                GLA_CHUNK, GLA_CHUNK, n_sub=4)
        xp = _proj_out(xp, o_nsa.reshape(bp * seq, -1), gla_out.reshape(bp * seq, -1),
                       ret_out.reshape(bp * seq, -1), nsa_g, w_out_b, g_post, l, tm_p // 2)
        win_keep = min(WINDOW, seq)
        for j, a in enumerate((kv_c, kv_s, kv_w[:, seq - win_keep:], s_g, s_r)):
            st_p[j].append(a)

        us = _proj_in(xs, g_pre, w_in_r, l, tm_s).reshape(bs, tp, U_WIDTH)
        kv_c = us[:, :t_real, BLK_KVC * kvw:(BLK_KVC + 1) * kvw]
        kv_s = us[:, :t_real, BLK_KVS * kvw:(BLK_KVS + 1) * kvw]
        kv_w = us[:, :t_real, BLK_KVW * kvw:(BLK_KVW + 1) * kvw]
        kc = _compress(cmp_pages, page_table, l, wblk[l], bias, w2blk[l])
        o_c, sel = _cmp_select(us, kc, wov_s, past_len, t_real)
        o_s = _slc_sample(us, slc_rows, page_table, sel[:, :, :t_real, :N_SEL], l, past_len, t_real)
        o_nsa = _win_combine(us, win_rows, o_c, o_s, l, past_len)
        gla_out, s_g, ret_out, s_r = _recurrent(us, s0_gla_s[:, l], state_ret[:, l], w_dec, b_dec, gla_g, ret_g, l,
                                                tp, t_real)
        xs = _proj_out(xs, o_nsa.reshape(bs * tp, -1), gla_out.reshape(bs * tp, -1),
                       ret_out.reshape(bs * tp, -1), nsa_g, w_out_b, g_post, l, tm_s)
        xs, wg_b, wu_b, wd_b = _ffn_cast(xs, g_pre, g_post, w_ffn_gate, w_ffn_up, w_ffn_down, l, 1, 2)
        xs = _ple(xs, ps, w_gate_b, w_ple_b, l, tm_s)
        xp = _ffn(xp, g_pre, g_post, wg_b, wu_b, wd_b, l, 2, tm_p // 2)
        xp = _ple(xp, pp, w_gate_b, w_ple_b, l, tm_p // 2)
        new_win = jnp.concatenate([win_rows[:, l], kv_w], axis=1)[:, t_real:]
        for j, a in enumerate((kv_c, kv_s, new_win, s_g, s_r)):
            st_s[j].append(a)

    def kv_stack(parts):
        a = jnp.stack(parts, axis=1)
        return a.reshape(a.shape[:3] + (NSA_KV_HEADS, 2, HEAD_DIM))

    y_p = xp.reshape(bp, seq, d_model)
    y_s = xs.reshape(bs, tp, d_model)[:, :t_real]
    return (y_p, y_s,
            kv_stack(st_p[0]), kv_stack(st_p[1]), kv_stack(st_p[2]),
            jnp.stack(st_p[3], axis=1), jnp.stack(st_p[4], axis=1),
            kv_stack(st_s[0]), kv_stack(st_s[1]), kv_stack(st_s[2]),
            jnp.stack(st_s[3], axis=1), jnp.stack(st_s[4], axis=1))
```

```python
import functools
import math

import numpy as np
import jax
import jax.numpy as jnp
from jax import lax
from jax.experimental import pallas as pl
from jax.experimental.pallas import tpu as pltpu

F32 = jnp.float32
BF16 = jnp.bfloat16
HIGHEST = lax.Precision.HIGHEST

HEAD_DIM = 64
NSA_HEADS = 16
NSA_KV_HEADS = 4
NSA_GROUP = 4
CMP_BLOCK = 32
CMP_STRIDE = 16
CMP_HIDDEN = 128
SEL_BLOCK = 64
N_SEL = 16
WINDOW = 512
Q_BLOCK = 128
OVERLAP_W = (0.5, 1.0, 1.0, 1.0, 0.5)
GLA_HEADS = 4
GLA_DK = 64
GLA_DV = 128
GLA_GATE_RANK = 16
GLA_GATE_TAU = 16.0
GLA_CHUNK = 64
RET_HEADS = 4
RET_DK = 128
RET_DV = 128
RET_CHUNK = 64
PAGE_SIZE = 128
NEG_INF = -1e30
FORCE = 1e9
EPS = 1e-6
IN_SIZES = (1024, 512, 512, 512, 48, 256, 256, 512, 16, 512, 512, 512, 512, 512)

LANES = 128
SUBLANES = 8
VMEM_LIMIT = 56 * 1024 * 1024

MXU_COLS = 256
U_WIDTH = 6400
BLK_KVC, BLK_KVS, BLK_KVW = 2, 3, 4
BLK_GQK, BLK_GV, BLK_GOG = 5, 6, 7
BLK_RQ, BLK_RK, BLK_RV, BLK_RG = 8, 9, 10, 11
BLK_MISC = 48
MISC_GLR = 48
Q_HEAD_COLS = NSA_GROUP * HEAD_DIM

ALIBI = [[2.0 ** (-8.0 * (h * NSA_GROUP + g + 1) / NSA_HEADS) for g in range(NSA_GROUP)]
         for h in range(NSA_KV_HEADS)]
RET_LOG_GAMMA = [math.log1p(-(2.0 ** (-5.0 - h))) for h in range(RET_HEADS)]


def _cparams(sem):
    return pltpu.CompilerParams(dimension_semantics=sem, vmem_limit_bytes=VMEM_LIMIT)


def _rms(x, g=None):
    y = x * lax.rsqrt(jnp.mean(x * x, axis=-1, keepdims=True) + EPS)
    return y if g is None else y * g


def _dot(a, b):
    return jnp.dot(a, b, preferred_element_type=F32)


def _dot_nt(a, b):
    return lax.dot_general(a, b, (((1,), (1,)), ((), ())), preferred_element_type=F32)


def _ffn_body(x_ref, gpre_ref, wg_ref, wu_ref, wd_ref, gpost_ref, o_ref, xn_ref, acc_ref):
    f = pl.program_id(1)

    @pl.when(f == 0)
    def _():
        xn_ref[...] = _rms(x_ref[...], gpre_ref[...]).astype(BF16)
        acc_ref[...] = jnp.zeros_like(acc_ref)

    xn = xn_ref[...]
    h = jax.nn.silu(_dot(xn, wg_ref[...])) * _dot(xn, wu_ref[...])
    acc_ref[...] += _dot(h.astype(BF16), wd_ref[...])

    @pl.when(f == pl.num_programs(1) - 1)
    def _():
        o_ref[...] = x_ref[...] + 0.5 * _rms(acc_ref[...], gpost_ref[...])


def _ffn(x, g_pre, g_post, wg, wu, wd, layer, norm_idx, tm, tf=512):
    m, d = x.shape
    ff = wg.shape[-1]
    assert m % tm == 0 and ff % tf == 0
    return pl.pallas_call(
        _ffn_body,
        out_shape=jax.ShapeDtypeStruct((m, d), F32),
        grid=(m // tm, ff // tf),
        in_specs=[
            pl.BlockSpec((tm, d), lambda i, f: (i, 0)),
            pl.BlockSpec((None, None, 1, d), lambda i, f: (layer, norm_idx, 0, 0)),
            pl.BlockSpec((d, tf), lambda i, f: (0, f)),
            pl.BlockSpec((d, tf), lambda i, f: (0, f)),
            pl.BlockSpec((tf, d), lambda i, f: (f, 0)),
            pl.BlockSpec((None, None, 1, d), lambda i, f: (layer, norm_idx, 0, 0)),
        ],
        out_specs=pl.BlockSpec((tm, d), lambda i, f: (i, 0)),
        scratch_shapes=[pltpu.VMEM((tm, d), BF16), pltpu.VMEM((tm, d), F32)],
        compiler_params=_cparams(("parallel", "arbitrary")),
        name="ffn",
    )(x, g_pre, wg, wu, wd, g_post)


def _ffn_cast_body(x_ref, gpre_ref, wg_ref, wu_ref, wd_ref, gpost_ref, o_ref, wgb_ref, wub_ref, wdb_ref,
                   xn_ref, acc_ref):
    wgb_ref[...] = wg_ref[...].astype(BF16)
    wub_ref[...] = wu_ref[...].astype(BF16)
    wdb_ref[...] = wd_ref[...].astype(BF16)
    _ffn_body(x_ref, gpre_ref, wgb_ref, wub_ref, wdb_ref, gpost_ref, o_ref, xn_ref, acc_ref)


def _ffn_cast(x, g_pre, g_post, wg, wu, wd, layer, which, norm_idx, tf=512):
    m, d = x.shape
    ff = wg.shape[-1]
    assert ff % tf == 0
    return pl.pallas_call(
        _ffn_cast_body,
        out_shape=(jax.ShapeDtypeStruct((m, d), F32), jax.ShapeDtypeStruct((d, ff), BF16),
                   jax.ShapeDtypeStruct((d, ff), BF16), jax.ShapeDtypeStruct((ff, d), BF16)),
        grid=(1, ff // tf),
        in_specs=[
            pl.BlockSpec((m, d), lambda i, f: (0, 0)),
            pl.BlockSpec((None, None, 1, d), lambda i, f: (layer, norm_idx, 0, 0)),
            pl.BlockSpec((None, None, d, tf), lambda i, f: (layer, which, 0, f)),
            pl.BlockSpec((None, None, d, tf), lambda i, f: (layer, which, 0, f)),
            pl.BlockSpec((None, None, tf, d), lambda i, f: (layer, which, f, 0)),
            pl.BlockSpec((None, None, 1, d), lambda i, f: (layer, norm_idx, 0, 0)),
        ],
        out_specs=(pl.BlockSpec((m, d), lambda i, f: (0, 0)), pl.BlockSpec((d, tf), lambda i, f: (0, f)),
                   pl.BlockSpec((d, tf), lambda i, f: (0, f)), pl.BlockSpec((tf, d), lambda i, f: (f, 0))),
        scratch_shapes=[pltpu.VMEM((m, d), BF16), pltpu.VMEM((m, d), F32)],
        compiler_params=_cparams(("arbitrary", "arbitrary")),
        name="ffn_cast",
    )(x, g_pre, wg, wu, wd, g_post)


def _proj_in_body(x_ref, g_ref, w_ref, o_ref, xn_ref):
    @pl.when(pl.program_id(1) == 0)
    def _():
        xn_ref[...] = _rms(x_ref[...], g_ref[...]).astype(BF16)

    o_ref[...] = _dot(xn_ref[...], w_ref[...])


def _proj_in(x, g_pre, w_in, layer, tm, tn=5 * MXU_COLS):
    m, d = x.shape
    n = w_in.shape[-1]
    return pl.pallas_call(
        _proj_in_body,
        out_shape=jax.ShapeDtypeStruct((m, n), F32),
        grid=(m // tm, n // tn),
        in_specs=[
            pl.BlockSpec((tm, d), lambda i, j: (i, 0)),
            pl.BlockSpec((None, None, 1, d), lambda i, j: (layer, 1, 0, 0)),
            pl.BlockSpec((None, d, tn), lambda i, j: (layer, 0, j)),
        ],
        out_specs=pl.BlockSpec((tm, tn), lambda i, j: (i, j)),
        scratch_shapes=[pltpu.VMEM((tm, d), BF16)],
        compiler_params=_cparams(("parallel", "arbitrary")),
        name="proj_in",
    )(x, g_pre, w_in)


def _proj_out_body(x_ref, nsa_ref, gla_ref, ret_ref, gn_ref, w_ref, gpost_ref, o_ref):
    nsa_w = nsa_ref.shape[-1]
    gla_w = gla_ref.shape[-1]
    nsa = _rms(nsa_ref[...], gn_ref[...]).astype(BF16)
    y = _dot(nsa, w_ref[0:nsa_w, :])
    y += _dot(gla_ref[...].astype(BF16), w_ref[nsa_w:nsa_w + gla_w, :])
    y += _dot(ret_ref[...].astype(BF16), w_ref[nsa_w + gla_w:, :])
    o_ref[...] = x_ref[...] + _rms(y, gpost_ref[...])


def _proj_out(x, o_nsa, gla_out, ret_out, nsa_norm, w_out, g_post, layer, tm):
    m, d = x.shape
    row = lambda a: pl.BlockSpec((tm, a.shape[-1]), lambda i: (i, 0))
    return pl.pallas_call(
        _proj_out_body,
        out_shape=jax.ShapeDtypeStruct((m, d), F32),
        grid=(m // tm,),
        in_specs=[
            row(x), row(o_nsa), row(gla_out), row(ret_out),
            pl.BlockSpec((None, 1, o_nsa.shape[-1]), lambda i: (layer, 0, 0)),
            pl.BlockSpec((None, w_out.shape[1], d), lambda i: (layer, 0, 0)),
            pl.BlockSpec((None, None, 1, d), lambda i: (layer, 1, 0, 0)),
        ],
        out_specs=row(x),
        compiler_params=_cparams(("parallel",)),
        name="proj_out",
    )(x, o_nsa, gla_out, ret_out, nsa_norm, w_out, g_post)


def _ple_body(x_ref, p_ref, wg_ref, wp_ref, o_ref):
    x = x_ref[...]
    gate = jax.nn.sigmoid(_dot(_rms(x).astype(BF16), wg_ref[...]))
    o_ref[...] = x + gate * _dot(p_ref[...].astype(BF16), wp_ref[...])


def _ple(x, p, w_gate, w_ple, layer, tm):
    m, d = x.shape
    return pl.pallas_call(
        _ple_body,
        out_shape=jax.ShapeDtypeStruct((m, d), F32),
        grid=(m // tm,),
        in_specs=[
            pl.BlockSpec((tm, d), lambda i: (i, 0)),
            pl.BlockSpec((None, tm, p.shape[-1]), lambda i: (layer, i, 0)),
            pl.BlockSpec((None, d, d), lambda i: (layer, 0, 0)),
            pl.BlockSpec((None, p.shape[-1], d), lambda i: (layer, 0, 0)),
        ],
        out_specs=pl.BlockSpec((tm, d), lambda i: (i, 0)),
        compiler_params=_cparams(("parallel",)),
        name="ple",
    )(x, p, w_gate, w_ple)


COMPRESS_PAGE_STEPS = (32, 16)
CHUNKS_PER_PAGE = PAGE_SIZE // CMP_STRIDE
KV_ROW = NSA_KV_HEADS * 2 * HEAD_DIM


def _cmp_bias_body(pos_ref, w1_ref, o_ref):
    o_ref[...] = jnp.dot(pos_ref[...], w1_ref[...], preferred_element_type=F32, precision=HIGHEST)


def _cmp_bias(cmp_pos, w_cmp1, layer):
    kdim = w_cmp1.shape[2]
    return pl.pallas_call(
        _cmp_bias_body,
        out_shape=jax.ShapeDtypeStruct((2, 1, CMP_HIDDEN), F32),
        grid=(2,),
        in_specs=[pl.BlockSpec((None, None, 1, kdim), lambda c: (layer, c, 0, 0)),
                  pl.BlockSpec((None, None, kdim, CMP_HIDDEN), lambda c: (layer, c, 0, 0))],
        out_specs=pl.BlockSpec((None, 1, CMP_HIDDEN), lambda c: (c, 0, 0)),
        compiler_params=_cparams(("arbitrary",)),
        name="cmp_bias",
    )(cmp_pos, w_cmp1)


def _compress_body(pt_ref, *refs, feature_major, pps):
    del pt_ref
    n_pg = pps if feature_major else pps * NSA_KV_HEADS
    page_refs = refs[:n_pg]
    wblk_ref, bias_ref, w2_ref, o_ref, prev_ref = refs[n_pg:n_pg + 5]
    rows = pps * CHUNKS_PER_PAGE
    half = 2 * CMP_HIDDEN

    @pl.when(pl.program_id(1) == 0)
    def _():
        prev_ref[...] = jnp.zeros_like(prev_ref)

    first_row = lax.broadcasted_iota(jnp.int32, (rows, half), 0) == 0

    if feature_major:
        tok_ref = refs[n_pg + 5]
        for k, pr in enumerate(page_refs):
            for h in range(NSA_KV_HEADS):
                tok_ref[k * NSA_KV_HEADS + h] = pr[h].reshape(LANES, PAGE_SIZE).T

        def token_rows(t, h):
            return jnp.concatenate([tok_ref[k * NSA_KV_HEADS + h, pl.ds(t, CHUNKS_PER_PAGE, stride=CMP_STRIDE), :]
                                    for k in range(pps)], axis=0)
    else:
        def token_rows(t, h):
            return jnp.concatenate([page_refs[k * NSA_KV_HEADS + h][pl.ds(t, CHUNKS_PER_PAGE, stride=CMP_STRIDE), :]
                                    for k in range(pps)], axis=0)

    for h in range(NSA_KV_HEADS):
        acc = jnp.zeros((rows, 2 * half), F32)
        for tt in range(CMP_STRIDE // 2):
            lhs = jnp.concatenate([token_rows(2 * tt, h), token_rows(2 * tt + 1, h)], axis=1).astype(BF16)
            acc += _dot(lhs, wblk_ref[tt])
        first = acc[:, :half]
        second = acc[:, half:]
        carry = prev_ref[h][SUBLANES - 1:SUBLANES, :]
        shifted = jnp.where(first_row, carry, pltpu.roll(first, 1, 0))
        prev_ref[h] = first[rows - SUBLANES:, :]
        hidden = jax.nn.gelu(shifted + second + bias_ref[...])
        o_ref[:, h * LANES:(h + 1) * LANES] = _dot(hidden.astype(BF16), w2_ref[...])


def _compress(pages, table, layer, wblk, bias, w2blk):
    b, n_pages = table.shape
    pps = next(p for p in COMPRESS_PAGE_STEPS if n_pages % p == 0)
    rows = pps * CHUNKS_PER_PAGE
    feature_major = pages.ndim == 6
    if feature_major:
        n_pg = pps
        page_block = (None, None) + pages.shape[2:]
        zeros = (0,) * (pages.ndim - 2)

        def page_spec(k):
            return pl.BlockSpec(page_block, lambda bi, i, pt: (pt[bi, i * pps + k], layer) + zeros)
    else:
        n_pg = pps * NSA_KV_HEADS

        def page_spec(kh):
            k, h = divmod(kh, NSA_KV_HEADS)
            return pl.BlockSpec((None, PAGE_SIZE, LANES),
                                lambda bi, i, pt: (bi, i * pps + k, BLK_KVC * NSA_KV_HEADS + h))

    scratch = [pltpu.VMEM((NSA_KV_HEADS, SUBLANES, 2 * CMP_HIDDEN), F32)]
    if feature_major:
        scratch.append(pltpu.VMEM((pps * NSA_KV_HEADS, PAGE_SIZE, LANES), F32))

    grid_spec = pltpu.PrefetchScalarGridSpec(
        num_scalar_prefetch=1,
        grid=(b, n_pages // pps),
        in_specs=[page_spec(k) for k in range(n_pg)] + [
            pl.BlockSpec(wblk.shape, lambda bi, i, pt: (0, 0, 0)),
            pl.BlockSpec(bias.shape, lambda bi, i, pt: (0, 0)),
            pl.BlockSpec(w2blk.shape, lambda bi, i, pt: (0, 0)),
        ],
        out_specs=pl.BlockSpec((None, rows, NSA_KV_HEADS * LANES), lambda bi, i, pt: (bi, i, 0)),
        scratch_shapes=scratch,
    )
    return pl.pallas_call(
        functools.partial(_compress_body, feature_major=feature_major, pps=pps),
        out_shape=jax.ShapeDtypeStruct((b, n_pages * CHUNKS_PER_PAGE, NSA_KV_HEADS * LANES), F32),
        grid_spec=grid_spec,
        compiler_params=_cparams(("arbitrary", "arbitrary")),
        name="compress",
    )(table, *([pages] * n_pg), wblk, bias, w2blk)


def _softmax_groups(s, distf, mask, slopes, rows):
    parts = []
    for g in range(NSA_GROUP):
        sg = s[g * rows:(g + 1) * rows] - slopes[g] * distf
        sg = jnp.where(mask, sg, NEG_INF)
        m = jnp.max(sg, axis=-1, keepdims=True)
        e = jnp.where(mask, jnp.exp(sg - m), 0.0)
        parts.append(e / jnp.maximum(jnp.sum(e, axis=-1, keepdims=True), 1e-30))
    return parts


def _unpack_head(pair, odd):
    lane = lax.broadcasted_iota(jnp.int32, pair.shape, 1)
    return jnp.where(lane < HEAD_DIM, pltpu.roll(pair, HEAD_DIM, 1) if odd else pair, 0.0)


def _stack_q(q_ref, base, scale):
    parts = []
    for g in range(NSA_GROUP):
        c0 = base + (g // 2) * LANES
        parts.append(_unpack_head(q_ref[:, c0:c0 + LANES], g % 2 == 1))
    return (jnp.concatenate(parts, axis=0) * scale).astype(BF16)


def _block_scores(imp, wov_ref, t_pos, n_slc):
    score = jnp.dot(imp, wov_ref[...], preferred_element_type=F32, precision=HIGHEST)
    blk = lax.broadcasted_iota(jnp.int32, score.shape, 1)
    cur = t_pos // SEL_BLOCK
    valid = blk <= cur
    forced = valid & ((blk == 0) | (blk == cur) | (blk == cur - 1))
    score = jnp.where(forced, FORCE, jnp.where(valid, score, -FORCE))
    return score, blk


def _pack_heads(o_ref, heads):
    lane = lax.broadcasted_iota(jnp.int32, heads[0].shape, 1)
    for p in range(NSA_HEADS // 2):
        even = pltpu.roll(heads[2 * p], HEAD_DIM, 1)
        o_ref[:, p * LANES:(p + 1) * LANES] = jnp.where(lane < HEAD_DIM, even, heads[2 * p + 1])


SLC_KEY_CHUNK = 512
LOG2E = 1.4426950408889634
MASK_BIG = 2.0 ** 100
ALIBI2 = [[s * LOG2E for s in row] for row in ALIBI]


def _exp2_softmax(s2, rel_row, bias, slopes2, rows):
    out = []
    for g in range(NSA_GROUP):
        lg = s2[g * rows:(g + 1) * rows] + (slopes2[g] * rel_row + bias)
        e = jnp.exp2(lg - jnp.max(lg, axis=-1, keepdims=True))
        out.append((e, jnp.sum(e, axis=-1, keepdims=True)))
    return out


def _nsa_prompt_body(q_ref, misc_ref, kc_ref, kvs_ref, kvw_ref, wovt_ref, o_ref, *, seq):
    qb = pl.program_id(1)
    t0 = qb * Q_BLOCK
    t0f = t0.astype(F32)
    n_slc = -(-seq // SEL_BLOCK)
    n_sel = min(N_SEL, n_slc)
    n_cr = kc_ref.shape[0]
    t_col = (t0 + lax.broadcasted_iota(jnp.int32, (Q_BLOCK, 1), 0)).astype(F32)
    gates = jax.nn.sigmoid(misc_ref[...])

    r_idx = lax.broadcasted_iota(jnp.int32, (1, n_cr), 1)
    end_c = (r_idx * CMP_STRIDE + (CMP_STRIDE - 1)).astype(F32)
    bias_c = jnp.minimum(t_col - end_c, 0.0) * MASK_BIG + jnp.where(r_idx >= 1, 0.0, -MASK_BIG)
    rel_c = end_c - t0f
    row_live = jnp.where(t_col >= CMP_BLOCK - 1, 1.0, 0.0)
    band = WINDOW + Q_BLOCK
    w0 = pl.multiple_of(jnp.maximum(t0 - WINDOW, 0), Q_BLOCK)
    pos_w = (w0 + lax.broadcasted_iota(jnp.int32, (1, band), 1)).astype(F32)
    d_w = t_col - pos_w
    bias_w = (jnp.minimum(d_w, 0.0) + jnp.minimum((WINDOW - 1.0) - d_w, 0.0)) * MASK_BIG
    rel_w = pos_w - t0f

    blk_t = lax.broadcasted_iota(jnp.int32, (n_slc, Q_BLOCK), 0)
    cur_t = (t0 + lax.broadcasted_iota(jnp.int32, (1, Q_BLOCK), 1)) // SEL_BLOCK
    valid_t = blk_t <= cur_t
    forced_t = valid_t & ((blk_t == 0) | (blk_t == cur_t) | (blk_t == cur_t - 1))
    tm1 = t_col - 1.0

    kv_heads = range(NSA_KV_HEADS)
    lane_sl = [slice(h * LANES, (h + 1) * LANES) for h in kv_heads]
    qhs = [_stack_q(q_ref, h * Q_HEAD_COLS, HEAD_DIM ** -0.5 * LOG2E) for h in kv_heads]

    o_cs, scores = [], []
    for h in kv_heads:
        kc = kc_ref[:, lane_sl[h]].astype(BF16)
        sm_c = _exp2_softmax(_dot_nt(qhs[h], kc), rel_c, bias_c, ALIBI2[h], Q_BLOCK)
        p_c = [e * (row_live / l) for (e, l) in sm_c]
        o_cs.append(_dot(jnp.concatenate(p_c, axis=0).astype(BF16), kc))
        imp = p_c[0] + p_c[1] + p_c[2] + p_c[3]
        score_t = lax.dot_general(wovt_ref[...], imp, (((1,), (1,)), ((), ())),
                                  preferred_element_type=F32, precision=HIGHEST)[:n_slc]
        scores.append(jnp.where(forced_t, FORCE, jnp.where(valid_t, score_t, -FORCE)))

    def ranked():
        out = []
        for score_t in scores:
            rank = jnp.zeros(score_t.shape, F32)
            for i in range(n_slc):
                ci = score_t[i:i + 1, :]
                tie = jnp.where(blk_t > i, 1.0, 0.0)
                rank += jnp.where(ci > score_t, 1.0, jnp.where(ci == score_t, tie, 0.0))
            out.append(jnp.where(rank < n_sel, MASK_BIG, 0.0))
        return tuple(out)

    sel_ts = lax.cond(t0 + Q_BLOCK > n_sel * SEL_BLOCK, ranked,
                      lambda: tuple(jnp.where(valid_t, MASK_BIG, 0.0) for _ in kv_heads))
    sels = [jnp.concatenate([s, jnp.zeros((LANES - n_slc, Q_BLOCK), F32)], axis=0).T.astype(BF16)
            for s in sel_ts]

    def chunk(c, carry):
        k0 = pl.multiple_of(c * SLC_KEY_CHUNK, SLC_KEY_CHUNK)
        pos = k0 + lax.broadcasted_iota(jnp.int32, (1, SLC_KEY_CHUNK), 1)
        posf = pos.astype(F32)
        expand = jnp.where(
            (pos // SEL_BLOCK) == lax.broadcasted_iota(jnp.int32, (LANES, SLC_KEY_CHUNK), 0),
            1.0, 0.0).astype(BF16)
        causal = jnp.minimum(tm1 - posf, -1.0) * MASK_BIG
        rel = posf - t0f
        new = []
        for h in kv_heads:
            kv = kvs_ref[pl.ds(k0, SLC_KEY_CHUNK), lane_sl[h]].astype(BF16)
            s2 = _dot_nt(qhs[h], kv)
            bias = _dot(sels[h], expand) + causal
            out = []
            for g in range(NSA_GROUP):
                m_old, l_old, a_old = carry[h][g]
                lg = s2[g * Q_BLOCK:(g + 1) * Q_BLOCK] + (ALIBI2[h][g] * rel + bias)
                m_new = jnp.maximum(m_old, jnp.max(lg, axis=-1, keepdims=True))
                alpha = jnp.exp2(m_old - m_new)
                e = jnp.exp2(lg - m_new)
                l_new = alpha * l_old + jnp.sum(e, axis=-1, keepdims=True)
                a_new = alpha * a_old + _dot(e.astype(BF16), kv)
                out.append((m_new, l_new, a_new))
            new.append(tuple(out))
        return tuple(new)

    init = tuple(tuple((jnp.full((Q_BLOCK, 1), NEG_INF, F32), jnp.zeros((Q_BLOCK, 1), F32),
                        jnp.zeros((Q_BLOCK, LANES), F32)) for _ in range(NSA_GROUP)) for _ in kv_heads)
    n_chunks = (t0 + Q_BLOCK + SLC_KEY_CHUNK - 1) // SLC_KEY_CHUNK
    fin = lax.fori_loop(0, n_chunks, chunk, init)

    heads = []
    for h in kv_heads:
        kvw = kvw_ref[pl.ds(w0, band), lane_sl[h]].astype(BF16)
        sm_w = _exp2_softmax(_dot_nt(qhs[h], kvw), rel_w, bias_w, ALIBI2[h], Q_BLOCK)
        o_w = _dot(jnp.concatenate([e for (e, _) in sm_w], axis=0).astype(BF16), kvw)
        for g in range(NSA_GROUP):
            col = h * NSA_GROUP + g
            rows = slice(g * Q_BLOCK, (g + 1) * Q_BLOCK)
            heads.append(gates[:, col:col + 1] * o_cs[h][rows]
                         + (gates[:, NSA_HEADS + col:NSA_HEADS + col + 1] / fin[h][g][1]) * fin[h][g][2]
                         + (gates[:, 2 * NSA_HEADS + col:2 * NSA_HEADS + col + 1] / sm_w[g][1]) * o_w[rows])
    _pack_heads(o_ref, heads)


def _nsa_prompt(u, kc, wov):
    b, t, _ = u.shape
    assert t % Q_BLOCK == 0 and t >= WINDOW + Q_BLOCK and t % SLC_KEY_CHUNK == 0
    n_slc = -(-t // SEL_BLOCK)
    assert n_slc % SUBLANES == 0 and n_slc <= LANES
    return pl.pallas_call(
        functools.partial(_nsa_prompt_body, seq=t),
        out_shape=jax.ShapeDtypeStruct((b, t, NSA_HEADS * HEAD_DIM), F32),
        grid=(b, t // Q_BLOCK),
        in_specs=[
            pl.BlockSpec((None, Q_BLOCK, NSA_HEADS * HEAD_DIM), lambda bi, i: (bi, i, 0)),
            pl.BlockSpec((None, Q_BLOCK, LANES), lambda bi, i: (bi, i, BLK_MISC)),
            pl.BlockSpec((None,) + kc.shape[1:], lambda bi, i: (bi, 0, 0)),
            pl.BlockSpec((None, t, 4 * LANES), lambda bi, i: (bi, 0, BLK_KVS)),
            pl.BlockSpec((None, t, 4 * LANES), lambda bi, i: (bi, 0, BLK_KVW)),
            pl.BlockSpec(wov.shape, lambda bi, i: (0, 0)),
        ],
        out_specs=pl.BlockSpec((None, Q_BLOCK, NSA_HEADS * HEAD_DIM), lambda bi, i: (bi, i, 0)),
        compiler_params=_cparams(("parallel", "arbitrary")),
        name="nsa_prompt",
    )(u, u, kc, u, u, wov)


def _cmp_select_body(q_ref, kc_ref, wov_ref, oc_ref, sel_ref, *, past_len, t_real):
    tp = q_ref.shape[0]
    n_cr = kc_ref.shape[0]
    n_slc = -(-(past_len + t_real) // SEL_BLOCK)
    t_pos = past_len + lax.broadcasted_iota(jnp.int32, (tp, 1), 0)
    for h in range(NSA_KV_HEADS):
        qh = _stack_q(q_ref, h * Q_HEAD_COLS, HEAD_DIM ** -0.5)
        kc = kc_ref[:, h * LANES:(h + 1) * LANES].astype(BF16)
        r_idx = lax.broadcasted_iota(jnp.int32, (1, n_cr), 1)
        dist = t_pos - (r_idx * CMP_STRIDE + (CMP_STRIDE - 1))
        mask = (r_idx >= 1) & (dist >= 0)
        p_c = _softmax_groups(_dot_nt(qh, kc), dist.astype(F32), mask, ALIBI[h], tp)
        oc_ref[h] = _dot(jnp.concatenate(p_c, axis=0).astype(BF16), kc)
        imp = p_c[0] + p_c[1] + p_c[2] + p_c[3]
        score, blk = _block_scores(imp, wov_ref, t_pos, n_slc)
        work = jnp.where(blk < n_slc, score, -3e38)
        blkf = blk.astype(F32)
        picked = jnp.zeros((tp, LANES), jnp.int32)
        lane = lax.broadcasted_iota(jnp.int32, (tp, LANES), 1)
        for k in range(N_SEL):
            m = jnp.max(work, axis=-1, keepdims=True)
            idx = jnp.min(jnp.where(work == m, blkf, 3e38), axis=-1, keepdims=True)
            picked = jnp.where(lane == k, idx.astype(jnp.int32), picked)
            work = jnp.where(blkf == idx, -3e38, work)
        sel_ref[h] = picked


def _cmp_select(u_s, kc, wov, past_len, t_real):
    b, tp, _ = u_s.shape
    assert -(-(past_len + t_real) // SEL_BLOCK) >= N_SEL
    return pl.pallas_call(
        functools.partial(_cmp_select_body, past_len=past_len, t_real=t_real),
        out_shape=(jax.ShapeDtypeStruct((b, NSA_KV_HEADS, NSA_GROUP * tp, LANES), F32),
                   jax.ShapeDtypeStruct((b, NSA_KV_HEADS, tp, LANES), jnp.int32)),
        grid=(b,),
        in_specs=[
            pl.BlockSpec((None, tp, NSA_HEADS * HEAD_DIM), lambda bi: (bi, 0, 0)),
            pl.BlockSpec((None,) + kc.shape[1:], lambda bi: (bi, 0, 0)),
            pl.BlockSpec(wov.shape, lambda bi: (0, 0)),
        ],
        out_specs=(pl.BlockSpec((None, NSA_KV_HEADS, NSA_GROUP * tp, LANES), lambda bi: (bi, 0, 0, 0)),
                   pl.BlockSpec((None, NSA_KV_HEADS, tp, LANES), lambda bi: (bi, 0, 0, 0))),
        compiler_params=_cparams(("parallel",)),
        name="cmp_select",
    )(u_s, kc, wov)


def _slc_sample_body(pidx_ref, pt_ref, *refs, past_len):
    del pidx_ref, pt_ref
    blk_refs = refs[:N_SEL]
    pos_ref, ok_ref, q_ref, kvn_ref, o_ref = refs[N_SEL:]
    h, t = pl.program_id(1), pl.program_id(2)
    tp = kvn_ref.shape[0]
    q_t = q_ref[pl.ds(t, 1), :]
    row = lax.broadcasted_iota(jnp.int32, (SUBLANES, 1), 0)
    q8 = jnp.zeros((SUBLANES, LANES), F32)
    slope = jnp.zeros((SUBLANES, 1), F32)
    for g in range(NSA_GROUP):
        c0 = (g // 2) * LANES
        q8 = jnp.where(row == g, _unpack_head(q_t[:, c0:c0 + LANES], g % 2 == 1), q8)
        slope_g = jnp.where(h == 0, ALIBI[0][g], jnp.where(h == 1, ALIBI[1][g],
                                                           jnp.where(h == 2, ALIBI[2][g], ALIBI[3][g])))
        slope = jnp.where(row == g, slope_g, slope)
    qh = (q8 * HEAD_DIM ** -0.5).astype(BF16)
    t_pos = past_len + t

    kv_t = jnp.concatenate([r[...].reshape(2 * HEAD_DIM, PAGE_SIZE) for r in blk_refs], axis=1).astype(BF16)
    d = t_pos - pos_ref[...]
    msk = (ok_ref[...] > 0) & (d >= 0)
    kvn = jnp.concatenate([kvn_ref[...], jnp.zeros((LANES - tp, LANES), F32)], axis=0).astype(BF16)
    idx_n = lax.broadcasted_iota(jnp.int32, (1, LANES), 1)
    dn = t - idx_n
    mskn = (idx_n < tp) & (dn >= 0)
    s_all = _dot(qh, kv_t)
    s_new = _dot_nt(qh, kvn)
    sg = jnp.where(msk, s_all - slope * d.astype(F32), NEG_INF)
    sn = jnp.where(mskn, s_new - slope * dn.astype(F32), NEG_INF)
    m = jnp.maximum(jnp.max(sg, axis=-1, keepdims=True), jnp.max(sn, axis=-1, keepdims=True))
    e = jnp.where(msk, jnp.exp(sg - m), 0.0)
    en = jnp.where(mskn, jnp.exp(sn - m), 0.0)
    l = jnp.sum(e, axis=-1, keepdims=True) + jnp.sum(en, axis=-1, keepdims=True)
    o_ref[...] = (_dot_nt(e.astype(BF16), kv_t) + _dot(en.astype(BF16), kvn)) / jnp.maximum(l, 1e-30)


def _slc_sample(u_s, cache, table, sel, layer, past_len, t_real):
    b, tp, _ = u_s.shape
    n_past_blk = past_len // SEL_BLOCK
    per_page = PAGE_SIZE // SEL_BLOCK
    assert past_len % SEL_BLOCK == 0 and t_real <= SEL_BLOCK and PAGE_SIZE % SEL_BLOCK == 0

    page_idx = (jnp.minimum(sel, n_past_blk - 1) // per_page).reshape(-1)
    blk_l = jnp.repeat(sel, PAGE_SIZE, axis=-1)
    in_page = jnp.asarray(np.arange(N_SEL * PAGE_SIZE) % PAGE_SIZE, jnp.int32)
    pos = ((blk_l // per_page) * PAGE_SIZE + in_page)[:, :, :, None, :]
    ok = ((blk_l < n_past_blk) & (in_page // SEL_BLOCK == blk_l % per_page)).astype(jnp.int32)[:, :, :, None, :]

    def blk_spec(k):
        def imap(bi, h, t, pidx, pt):
            return (pt[bi, pidx[((bi * NSA_KV_HEADS + h) * t_real + t) * N_SEL + k]], layer, h, 0, 0, 0)
        return pl.BlockSpec((None, None, None, 2, HEAD_DIM, PAGE_SIZE), imap)

    lane_spec = pl.BlockSpec((None, None, None, 1, N_SEL * PAGE_SIZE), lambda bi, h, t, pidx, pt: (bi, h, t, 0, 0))
    grid_spec = pltpu.PrefetchScalarGridSpec(
        num_scalar_prefetch=2,
        grid=(b, NSA_KV_HEADS, t_real),
        in_specs=[blk_spec(k) for k in range(N_SEL)] + [
            lane_spec, lane_spec,
            pl.BlockSpec((None, tp, Q_HEAD_COLS), lambda bi, h, t, pidx, pt: (bi, 0, h)),
            pl.BlockSpec((None, tp, LANES), lambda bi, h, t, pidx, pt: (bi, 0, BLK_KVS * 4 + h)),
        ],
        out_specs=pl.BlockSpec((None, None, None, SUBLANES, LANES), lambda bi, h, t, pidx, pt: (bi, h, t, 0, 0)),
    )
    return pl.pallas_call(
        functools.partial(_slc_sample_body, past_len=past_len),
        out_shape=jax.ShapeDtypeStruct((b, NSA_KV_HEADS, t_real, SUBLANES, LANES), F32),
        grid_spec=grid_spec,
        compiler_params=_cparams(("parallel", "parallel", "arbitrary")),
        name="slc_sample",
    )(page_idx, table, *([cache] * N_SEL), pos, ok, u_s, u_s)


def _win_combine_body(q_ref, misc_ref, win_ref, kvn_ref, oc_ref, os_ref, o_ref, *, past_len):
    tp = q_ref.shape[0]
    wb = win_ref.shape[0]
    t_pos = past_len + lax.broadcasted_iota(jnp.int32, (tp, 1), 0)
    gates = jax.nn.sigmoid(misc_ref[...])
    n_keys = wb + LANES
    idx = lax.broadcasted_iota(jnp.int32, (1, n_keys), 1)
    d = t_pos - (past_len - wb + idx)
    mask = (idx < wb + tp) & (d >= 0) & (d < WINDOW)
    heads = []
    for h in range(NSA_KV_HEADS):
        sl = slice(h * LANES, (h + 1) * LANES)
        qh = _stack_q(q_ref, h * Q_HEAD_COLS, HEAD_DIM ** -0.5)
        kv = jnp.concatenate([win_ref[:, sl], kvn_ref[:, sl], jnp.zeros((LANES - tp, LANES), F32)],
                             axis=0).astype(BF16)
        p_w = _softmax_groups(_dot_nt(qh, kv), d.astype(F32), mask, ALIBI[h], tp)
        o_w = _dot(jnp.concatenate(p_w, axis=0).astype(BF16), kv)
        o_c = oc_ref[h]
        t_live = os_ref.shape[1]
        for g in range(NSA_GROUP):
            col = h * NSA_GROUP + g
            rows = slice(g * tp, (g + 1) * tp)
            o_s = jnp.concatenate([os_ref[h, tt, g:g + 1, :] for tt in range(t_live)]
                                  + [jnp.zeros((tp - t_live, LANES), F32)], axis=0)
            heads.append(gates[:, col:col + 1] * o_c[rows]
                         + gates[:, NSA_HEADS + col:NSA_HEADS + col + 1] * o_s
                         + gates[:, 2 * NSA_HEADS + col:2 * NSA_HEADS + col + 1] * o_w[rows])
    _pack_heads(o_ref, heads)


def _win_combine(u_s, win, o_c, o_s, layer, past_len):
    b, tp, _ = u_s.shape
    wb = win.shape[2]
    return pl.pallas_call(
        functools.partial(_win_combine_body, past_len=past_len),
        out_shape=jax.ShapeDtypeStruct((b, tp, NSA_HEADS * HEAD_DIM), F32),
        grid=(b,),
        in_specs=[
            pl.BlockSpec((None, tp, NSA_HEADS * HEAD_DIM), lambda bi: (bi, 0, 0)),
            pl.BlockSpec((None, tp, LANES), lambda bi: (bi, 0, BLK_MISC)),
            pl.BlockSpec((None, None, wb, 4 * LANES), lambda bi: (bi, layer, 0, 0)),
            pl.BlockSpec((None, tp, 4 * LANES), lambda bi: (bi, 0, BLK_KVW)),
            pl.BlockSpec((None,) + o_c.shape[1:], lambda bi: (bi, 0, 0, 0)),
            pl.BlockSpec((None,) + o_s.shape[1:], lambda bi: (bi, 0, 0, 0, 0)),
        ],
        out_specs=pl.BlockSpec((None, tp, NSA_HEADS * HEAD_DIM), lambda bi: (bi, 0, 0)),
        compiler_params=_cparams(("parallel",)),
        name="win_combine",
    )(u_s, u_s, win, u_s, o_c, o_s)


def _cumsum_rows(x):
    n = x.shape[0]
    row = lax.broadcasted_iota(jnp.int32, x.shape, 0)
    shift = 1
    while shift < n:
        x = x + jnp.where(row >= shift, pltpu.roll(x, shift, 0), 0.0)
        shift *= 2
    return x


def _gla_body(gqk_ref, gv_ref, gog_ref, misc_ref, wdec_ref, bdec_ref, gn_ref, s0_ref,
              o_ref, sout_ref, st_ref, *, c_real, n_sub):
    ci = pl.program_id(1)

    @pl.when(ci == 0)
    def _():
        for h in range(GLA_HEADS):
            st_ref[h] = s0_ref[h].T

    cp = gqk_ref.shape[0] // n_sub
    for j in range(n_sub):
        _gla_chunk(gqk_ref, gv_ref, gog_ref, misc_ref, wdec_ref, bdec_ref, gn_ref, o_ref, st_ref,
                   slice(j * cp, (j + 1) * cp), cp, c_real)

    @pl.when(ci == pl.num_programs(1) - 1)
    def _():
        for h in range(GLA_HEADS):
            sout_ref[h] = st_ref[h].T[:GLA_DK, :]


def _gla_chunk(gqk_ref, gv_ref, gog_ref, misc_ref, wdec_ref, bdec_ref, gn_ref, o_ref, st_ref, rs, cp, c_real):
    k_base = GLA_HEADS * GLA_DK
    sb = min(16, cp)
    row = lax.broadcasted_iota(jnp.int32, (cp, 1), 0)
    live = row < c_real
    x = jnp.dot(misc_ref[rs, :], wdec_ref[...], preferred_element_type=F32, precision=HIGHEST) + bdec_ref[...]
    log_a = (jnp.minimum(x, 0.0) - jnp.log1p(jnp.exp(-jnp.abs(x)))) / GLA_GATE_TAU
    b_all = _cumsum_rows(jnp.where(live, log_a, 0.0))
    for h in range(GLA_HEADS):
        sl = slice(h * LANES, (h + 1) * LANES)
        pair = slice((h // 2) * LANES, (h // 2 + 1) * LANES)
        q = _unpack_head(gqk_ref[rs, pair], h % 2 == 1) * GLA_DK ** -0.5
        k_pair = slice(k_base + (h // 2) * LANES, k_base + (h // 2 + 1) * LANES)
        k = jnp.where(live, _unpack_head(gqk_ref[rs, k_pair], h % 2 == 1), 0.0)
        v = jnp.where(live, gv_ref[rs, sl], 0.0)
        b = b_all[:, sl]
        kb = k.astype(BF16)
        st = st_ref[h]
        attn_rows = []
        for i in range(cp // sb):
            qi = q[i * sb:(i + 1) * sb]
            bi = b[i * sb:(i + 1) * sb]
            sub_row = lax.broadcasted_iota(jnp.int32, (sb, 1), 0)
            ys = []
            for s_loc in range(sb):
                bs = b[i * sb + s_loc:i * sb + s_loc + 1]
                ys.append(qi * jnp.exp(jnp.where(sub_row >= s_loc, bi - bs, NEG_INF)))
            z = _dot_nt(jnp.concatenate(ys, axis=0).astype(BF16), kb)
            lane = lax.broadcasted_iota(jnp.int32, (sb, cp), 1)
            a_i = jnp.zeros((sb, cp), F32)
            for s_loc in range(sb):
                a_i += jnp.where(lane == i * sb + s_loc, z[s_loc * sb:(s_loc + 1) * sb], 0.0)
            if i > 0:
                ref_b = b[i * sb - 1:i * sb]
                qt = qi * jnp.exp(bi - ref_b)
                kt = k * jnp.exp(jnp.where(row < i * sb, ref_b - b, NEG_INF))
                a_i += _dot_nt(qt.astype(BF16), kt.astype(BF16))
            attn_rows.append(a_i)
        attn = jnp.concatenate(attn_rows, axis=0) if len(attn_rows) > 1 else attn_rows[0]
        o = _dot(attn.astype(BF16), v.astype(BF16))
        o += _dot_nt((q * jnp.exp(b)).astype(BF16), st.astype(BF16))
        b_last = b[c_real - 1:c_real]
        kd = k * jnp.exp(b_last - b)
        st_ref[h] = jnp.exp(b_last) * st + _dot(v.T.astype(BF16), kd.astype(BF16))
        o_ref[rs, sl] = _rms(o, gn_ref[...]) * jax.nn.silu(gog_ref[rs, sl])


def _recurrent_body(gqk_ref, gv_ref, gog_ref, misc_ref, wdec_ref, bdec_ref, gn_ref, s0_ref,
                    rq_ref, rk_ref, rv_ref, rg_ref, rn_ref, r0_ref,
                    og_ref, sout_ref, or_ref, rout_ref, st_ref, rt_ref, *, c_real, n_sub):
    _gla_body(gqk_ref, gv_ref, gog_ref, misc_ref, wdec_ref, bdec_ref, gn_ref, s0_ref,
              og_ref, sout_ref, st_ref, c_real=c_real, n_sub=n_sub)
    ret_real = c_real if n_sub == 1 else rq_ref.shape[0]
    _ret_body(rq_ref, rk_ref, rv_ref, rg_ref, rn_ref, r0_ref, or_ref, rout_ref, rt_ref, c_real=ret_real)


def _recurrent(u, s0, r0, w_dec, b_dec, gla_norm, ret_norm, layer, gla_chunk, c_real, n_sub=1):
    b, t, _ = u.shape
    cp = gla_chunk * n_sub
    assert t % cp == 0 and (n_sub == 1 or c_real == gla_chunk)
    blk = lambda idx: pl.BlockSpec((None, cp, 4 * LANES), lambda bi, i: (bi, i, idx))
    per_b = lambda shape: pl.BlockSpec((None,) + shape, lambda bi, i: (bi, 0, 0, 0))
    per_layer = lambda shape: pl.BlockSpec((None,) + shape, lambda bi, i: (layer, 0, 0))
    return pl.pallas_call(
        functools.partial(_recurrent_body, c_real=c_real, n_sub=n_sub),
        out_shape=(jax.ShapeDtypeStruct((b, t, GLA_HEADS * GLA_DV), F32),
                   jax.ShapeDtypeStruct((b, GLA_HEADS, GLA_DK, GLA_DV), F32),
                   jax.ShapeDtypeStruct((b, t, RET_HEADS * RET_DV), F32),
                   jax.ShapeDtypeStruct((b, RET_HEADS, RET_DK, RET_DV), F32)),
        grid=(b, t // cp),
        in_specs=[
            blk(BLK_GQK), blk(BLK_GV), blk(BLK_GOG),
            pl.BlockSpec((None, cp, LANES), lambda bi, i: (bi, i, BLK_MISC)),
            per_layer((LANES, 4 * LANES)), per_layer((1, 4 * LANES)), per_layer((1, GLA_DV)),
            per_b((GLA_HEADS, LANES, GLA_DV)),
            blk(BLK_RQ), blk(BLK_RK), blk(BLK_RV), blk(BLK_RG),
            per_layer((1, RET_HEADS * RET_DV)),
            per_b((RET_HEADS, RET_DK, RET_DV)),
        ],
        out_specs=(pl.BlockSpec((None, cp, GLA_HEADS * GLA_DV), lambda bi, i: (bi, i, 0)),
                   per_b((GLA_HEADS, GLA_DK, GLA_DV)),
                   pl.BlockSpec((None, cp, RET_HEADS * RET_DV), lambda bi, i: (bi, i, 0)),
                   per_b((RET_HEADS, RET_DK, RET_DV))),
        scratch_shapes=[pltpu.VMEM((GLA_HEADS, GLA_DV, LANES), F32),
                        pltpu.VMEM((RET_HEADS, RET_DV, RET_DK), F32)],
        compiler_params=_cparams(("parallel", "arbitrary")),
        name="recurrent",
    )(u, u, u, u, w_dec, b_dec, gla_norm, s0, u, u, u, u, ret_norm, r0)


def _ret_body(rq_ref, rk_ref, rv_ref, rg_ref, gn_ref, r0_ref, o_ref, rout_ref, rt_ref, *, c_real):
    ci = pl.program_id(1)
    cp = rq_ref.shape[0]

    @pl.when(ci == 0)
    def _():
        for h in range(RET_HEADS):
            rt_ref[h] = r0_ref[h].T

    row = lax.broadcasted_iota(jnp.int32, (cp, 1), 0)
    live = row < c_real
    rowf = row.astype(F32)
    rel = rowf - lax.broadcasted_iota(jnp.int32, (1, cp), 1).astype(F32)
    for h in range(RET_HEADS):
        sl = slice(h * LANES, (h + 1) * LANES)
        lg = RET_LOG_GAMMA[h]
        q = rq_ref[:, sl]
        k = jnp.where(live, rk_ref[:, sl] * RET_DK ** -0.5, 0.0)
        v = jnp.where(live, rv_ref[:, sl], 0.0)
        decay = jnp.where(rel >= 0, jnp.exp(jnp.maximum(rel, 0.0) * lg), 0.0)
        attn = _dot_nt(q.astype(BF16), k.astype(BF16)) * decay
        o = _dot(attn.astype(BF16), v.astype(BF16))
        rt = rt_ref[h]
        o += _dot_nt((q * jnp.exp((rowf + 1.0) * lg)).astype(BF16), rt.astype(BF16))
        kd = k * jnp.exp((c_real - 1.0 - rowf) * lg)
        rt_ref[h] = math.exp(c_real * lg) * rt + _dot(v.T.astype(BF16), kd.astype(BF16))
        mu = jnp.mean(o, axis=-1, keepdims=True)
        var = jnp.mean(jnp.square(o - mu), axis=-1, keepdims=True)
        o_ref[:, sl] = (o - mu) * lax.rsqrt(var + EPS) * gn_ref[:, sl] * jax.nn.silu(rg_ref[:, sl])

    @pl.when(ci == pl.num_programs(1) - 1)
    def _():
        for h in range(RET_HEADS):
            rout_ref[h] = rt_ref[h].T


def _pad_heads(w, n_heads):
    lead = w.shape[:-1]
    w = w.reshape(lead + (n_heads, HEAD_DIM))
    w = jnp.pad(w, [(0, 0)] * len(lead) + [(0, 0), (0, LANES - HEAD_DIM)])
    return w.reshape(lead + (n_heads * LANES,))


def _prep_w_in(w_in):
    cuts = np.cumsum((0,) + IN_SIZES)
    nq, kvc, kvs, kvw, ng, gq, gk, gv, glr, gog, rq, rk, rv, rg = [
        w_in[..., cuts[i]:cuts[i + 1]] for i in range(len(IN_SIZES))]
    misc = jnp.concatenate([ng, glr], axis=-1)
    misc = jnp.pad(misc, [(0, 0), (0, 0), (0, LANES - misc.shape[-1])])
    w = jnp.concatenate([nq, kvc, kvs, kvw, gq, gk, gv, gog, rq, rk, rv, rg, misc], axis=-1)
    assert w.shape[-1] == (BLK_MISC + 1) * LANES and U_WIDTH % MXU_COLS == 0
    w = jnp.pad(w, [(0, 0), (0, 0), (0, U_WIDTH - w.shape[-1])])
    return w.astype(BF16)


def _prep_cmp_weights(w_cmp1, w_cmp2):
    depth = w_cmp1.shape[0]
    half = CMP_STRIDE * HEAD_DIM
    w1 = w_cmp1.reshape(depth, 2, 2, CMP_STRIDE, HEAD_DIM, CMP_HIDDEN)
    w1 = w1.transpose(0, 3, 1, 4, 2, 5)
    z = jnp.zeros_like(w1[:, :, 0])
    k_rows = jnp.stack([w1[:, :, 0], z], axis=4)
    v_rows = jnp.stack([z, w1[:, :, 1]], axis=4)
    wblk = jnp.concatenate([k_rows, v_rows], axis=2)
    wblk = wblk.reshape(depth, CMP_STRIDE // 2, 4 * HEAD_DIM, 4 * CMP_HIDDEN).astype(BF16)
    del half
    z2 = jnp.zeros_like(w_cmp2[:, 0])
    w2blk = jnp.concatenate([jnp.concatenate([w_cmp2[:, 0], z2], axis=-1),
                             jnp.concatenate([z2, w_cmp2[:, 1]], axis=-1)], axis=1).astype(BF16)
    return wblk, w2blk


def _overlap_matrix(n_rows, n_slc, n_cols):
    ratio = SEL_BLOCK // CMP_STRIDE
    w = np.zeros((n_rows, n_cols), np.float32)
    for j in range(n_slc):
        for k, wk in enumerate(OVERLAP_W):
            if ratio * j + k < n_rows:
                w[ratio * j + k, j] = wk
    return jnp.asarray(w)


def kernel(x_prompt, x_sample, cache_cmp_kv, cache_slc_kv, cache_win_kv, state_gla, state_ret, page_table,
           p_prompt, p_sample, norm_pre, norm_post, w_ffn_gate, w_ffn_up, w_ffn_down, w_in, w_out,
           w_cmp1, w_cmp2, cmp_pos, nsa_norm, w_gla_decay, b_gla_decay, gla_norm, ret_norm, w_ple, w_ple_gate):
    depth = w_in.shape[0]
    bp, seq, d_model = x_prompt.shape
    bs, t_real, _ = x_sample.shape
    n_pool = cache_cmp_kv.shape[0]
    n_pages = page_table.shape[1]
    past_len = n_pages * PAGE_SIZE
    tp = -(-t_real // SUBLANES) * SUBLANES
    kvw = NSA_KV_HEADS * 2 * HEAD_DIM

    w_in_r = _prep_w_in(w_in)
    w_out_b = w_out.astype(BF16)
    w_ple_b = w_ple.astype(BF16)
    w_gate_b = w_ple_gate.astype(BF16)
    wblk, w2blk = _prep_cmp_weights(w_cmp1, w_cmp2)
    g_pre = norm_pre.reshape(depth, 3, 1, d_model)
    g_post = norm_post.reshape(depth, 3, 1, d_model)
    nsa_g = nsa_norm.reshape(depth, 1, -1)
    gla_g = gla_norm.reshape(depth, 1, -1)
    ret_g = ret_norm.reshape(depth, 1, -1)
    w_dec = jnp.pad(_pad_heads(w_gla_decay, GLA_HEADS),
                    [(0, 0), (MISC_GLR, LANES - MISC_GLR - GLA_GATE_RANK), (0, 0)])
    b_dec = _pad_heads(b_gla_decay, GLA_HEADS).reshape(depth, 1, -1)
    pos_flat = cmp_pos.reshape(depth, 2, 1, CMP_BLOCK * HEAD_DIM)

    cmp_pages = cache_cmp_kv.transpose(0, 1, 3, 4, 5, 2)
    slc_rows = cache_slc_kv.transpose(0, 1, 3, 4, 5, 2)
    win_rows = cache_win_kv.reshape(bs, depth, -1, kvw)
    wb = win_rows.shape[2]

    n_cr_p = seq // CMP_STRIDE
    wov_p = _overlap_matrix(n_cr_p, -(-seq // SEL_BLOCK), LANES).T
    n_cr_s = past_len // CMP_STRIDE
    n_slc_s = -(-(past_len + t_real) // SEL_BLOCK)
    wov_s = _overlap_matrix(n_cr_s, n_slc_s, -(-n_slc_s // LANES) * LANES)
    table_p = jnp.arange(bp * (seq // PAGE_SIZE), dtype=jnp.int32).reshape(bp, seq // PAGE_SIZE)

    xp = x_prompt.reshape(bp * seq, d_model)
    xs = jnp.pad(x_sample, ((0, 0), (0, tp - t_real), (0, 0))).reshape(bs * tp, d_model)
    pp = p_prompt.reshape(depth, bp * seq, -1)
    ps = jnp.pad(p_sample, ((0, 0), (0, 0), (0, tp - t_real), (0, 0))).reshape(depth, bs * tp, -1)
    s0_gla_p = jnp.zeros((bp, GLA_HEADS, LANES, GLA_DV), F32)
    s0_ret_p = jnp.zeros((bp, RET_HEADS, RET_DK, RET_DV), F32)
    s0_gla_s = jnp.pad(state_gla, ((0, 0), (0, 0), (0, 0), (0, LANES - GLA_DK), (0, 0)))

    tm_p = 1024
    tm_s = bs * tp
    st_p = [[] for _ in range(5)]
    st_s = [[] for _ in range(5)]
    for l in range(depth):
        bias = _cmp_bias(pos_flat, w_cmp1, l).reshape(1, 2 * CMP_HIDDEN)

        xs, wg_a, wu_a, wd_a = _ffn_cast(xs, g_pre, g_post, w_ffn_gate, w_ffn_up, w_ffn_down, l, 0, 0)

        xp = _ffn(xp, g_pre, g_post, wg_a, wu_a, wd_a, l, 0, tm_p // 2)
        u2 = _proj_in(xp, g_pre, w_in_r, l, tm_p)
        u = u2.reshape(bp, seq, U_WIDTH)
        kv_c = u[..., BLK_KVC * kvw:(BLK_KVC + 1) * kvw]
        kv_s = u[..., BLK_KVS * kvw:(BLK_KVS + 1) * kvw]
        kv_w = u[..., BLK_KVW * kvw:(BLK_KVW + 1) * kvw]
        kc = _compress(u, table_p, 0,
                       wblk[l], bias, w2blk[l])
        o_nsa = _nsa_prompt(u, kc, wov_p)
        gla_out, s_g, ret_out, s_r = _recurrent(u, s0_gla_p, s0_ret_p, w_dec, b_dec, gla_g, ret_g, l,
                                                GLA_CHUNK, GLA_CHUNK, n_sub=4)
        xp = _proj_out(xp, o_nsa.reshape(bp * seq, -1), gla_out.reshape(bp * seq, -1),
                       ret_out.reshape(bp * seq, -1), nsa_g, w_out_b, g_post, l, tm_p // 2)
        win_keep = min(WINDOW, seq)
        for j, a in enumerate((kv_c, kv_s, kv_w[:, seq - win_keep:], s_g, s_r)):
            st_p[j].append(a)

        us = _proj_in(xs, g_pre, w_in_r, l, tm_s).reshape(bs, tp, U_WIDTH)
        kv_c = us[:, :t_real, BLK_KVC * kvw:(BLK_KVC + 1) * kvw]
        kv_s = us[:, :t_real, BLK_KVS * kvw:(BLK_KVS + 1) * kvw]
        kv_w = us[:, :t_real, BLK_KVW * kvw:(BLK_KVW + 1) * kvw]
        kc = _compress(cmp_pages, page_table, l, wblk[l], bias, w2blk[l])
        o_c, sel = _cmp_select(us, kc, wov_s, past_len, t_real)
        o_s = _slc_sample(us, slc_rows, page_table, sel[:, :, :t_real, :N_SEL], l, past_len, t_real)
        o_nsa = _win_combine(us, win_rows, o_c, o_s, l, past_len)
        gla_out, s_g, ret_out, s_r = _recurrent(us, s0_gla_s[:, l], state_ret[:, l], w_dec, b_dec, gla_g, ret_g, l,
                                                tp, t_real)
        xs = _proj_out(xs, o_nsa.reshape(bs * tp, -1), gla_out.reshape(bs * tp, -1),
                       ret_out.reshape(bs * tp, -1), nsa_g, w_out_b, g_post, l, tm_s)
        xs, wg_b, wu_b, wd_b = _ffn_cast(xs, g_pre, g_post, w_ffn_gate, w_ffn_up, w_ffn_down, l, 1, 2)
        xs = _ple(xs, ps, w_gate_b, w_ple_b, l, tm_s)
        xp = _ffn(xp, g_pre, g_post, wg_b, wu_b, wd_b, l, 2, tm_p // 2)
        xp = _ple(xp, pp, w_gate_b, w_ple_b, l, tm_p // 2)
        new_win = jnp.concatenate([win_rows[:, l], kv_w], axis=1)[:, t_real:]
        for j, a in enumerate((kv_c, kv_s, new_win, s_g, s_r)):
            st_s[j].append(a)

    def kv_stack(parts):
        a = jnp.stack(parts, axis=1)
        return a.reshape(a.shape[:3] + (NSA_KV_HEADS, 2, HEAD_DIM))

    y_p = xp.reshape(bp, seq, d_model)
    y_s = xs.reshape(bs, tp, d_model)[:, :t_real]
    return (y_p, y_s,
            kv_stack(st_p[0]), kv_stack(st_p[1]), kv_stack(st_p[2]),
            jnp.stack(st_p[3], axis=1), jnp.stack(st_p[4], axis=1),
            kv_stack(st_s[0]), kv_stack(st_s[1]), kv_stack(st_s[2]),
            jnp.stack(st_s[3], axis=1), jnp.stack(st_s[4], axis=1))
```
